```python
import math
import jax, jax.numpy as jnp
from jax import lax
import numpy as np

D_MODEL = 4096
BATCH = 4
SEQ = 4096
DEPTH = 4

N_MIXERS = 3
N_FOX = (DEPTH + 2) // 3
N_SSM = (DEPTH + 1) // 3
N_DIFF = DEPTH // 3

Q_BLOCK = 128
NORM_EPS = 1e-6

FOX_HEADS = 32
FOX_HEAD_DIM = D_MODEL // FOX_HEADS
FOX_IN = 3 * D_MODEL + FOX_HEADS

SSM_EXPAND = 2
SSM_D_INNER = SSM_EXPAND * D_MODEL
SSM_HEAD_DIM = 64
SSM_HEADS = SSM_D_INNER // SSM_HEAD_DIM
SSM_GROUPS = 8
SSM_HEADS_PER_GROUP = SSM_HEADS // SSM_GROUPS
SSM_STATE = 128
SSM_CONV = 4
SSM_CHUNK = 128
SSM_CONV_DIM = SSM_D_INNER + 2 * SSM_GROUPS * SSM_STATE
SSM_IN = 2 * SSM_D_INNER + 2 * SSM_GROUPS * SSM_STATE + SSM_HEADS
SSM_NORM_EPS = 1e-5

DIFF_HEADS = 16
DIFF_HEAD_DIM = D_MODEL // (2 * DIFF_HEADS)
DIFF_IN = 3 * D_MODEL
DIFF_SUBLN_EPS = 1e-5

D_FF = 11008
FFN_CONV = 3

kernel_name = "hybrid_fox_ssd_diffattn_convffn"


def rmsnorm(x, g, eps=NORM_EPS):
    xf = x.astype(jnp.float32)
    y = xf * lax.rsqrt(jnp.mean(xf * xf, axis=-1, keepdims=True) + eps)
    return y.astype(x.dtype) * g


def causal_dwconv(u, w, b):
    k = w.shape[0]
    s = u.shape[1]
    up = jnp.pad(u, ((0, 0), (k - 1, 0), (0, 0)))
    out = b.astype(u.dtype)
    for j in range(k):
        out = out + up[:, j:j + s] * w[j].astype(u.dtype)
    return out


def to_blocks(a):
    bsz, s = a.shape[:2]
    return jnp.moveaxis(a.reshape(bsz, s // Q_BLOCK, Q_BLOCK, *a.shape[2:]), 1, 0)


def from_blocks(a):
    a = jnp.moveaxis(a, 0, 1)
    return a.reshape(a.shape[0], a.shape[1] * a.shape[2], *a.shape[3:])


def alibi_slopes(n_heads):
    return jnp.exp2(-8.0 * jnp.arange(1, n_heads + 1, dtype=jnp.float32) / n_heads)


def fox_mixer(h, w_in, b_f, w_o):
    bsz, s, _ = h.shape
    proj = h @ w_in
    q, k, v, f_logit = jnp.split(proj, [D_MODEL, 2 * D_MODEL, 3 * D_MODEL], axis=-1)
    q = q.reshape(bsz, s, FOX_HEADS, FOX_HEAD_DIM)
    k = k.reshape(bsz, s, FOX_HEADS, FOX_HEAD_DIM)
    v = v.reshape(bsz, s, FOX_HEADS, FOX_HEAD_DIM)
    log_f = jax.nn.log_sigmoid(f_logit.astype(jnp.float32) + b_f.astype(jnp.float32))
    cum = jnp.cumsum(log_f, axis=1)
    cum_k = jnp.transpose(cum, (0, 2, 1))
    kpos = jnp.arange(s)
    scale = FOX_HEAD_DIM ** -0.5

    def block(args):
        qb, cq, start = args
        sc = jnp.einsum('bqhd,bkhd->bhqk', qb, k).astype(jnp.float32) * scale
        sc = sc + jnp.transpose(cq, (0, 2, 1))[..., None] - cum_k[:, :, None, :]
        qpos = start + jnp.arange(Q_BLOCK)
        sc = jnp.where(kpos[None, :] <= qpos[:, None], sc, -jnp.inf)
        p = jax.nn.softmax(sc, axis=-1).astype(v.dtype)
        return jnp.einsum('bhqk,bkhd->bqhd', p, v)

    starts = jnp.arange(s // Q_BLOCK, dtype=jnp.int32) * Q_BLOCK
    out = from_blocks(lax.map(block, (to_blocks(q), to_blocks(cum), starts)))
    return out.reshape(bsz, s, D_MODEL) @ w_o


def ssd_scan(x, dt, a, b, c):
    bsz, s = x.shape[:2]
    nc = s // SSM_CHUNK

    def chunks(t):
        return jnp.moveaxis(t.reshape(bsz, nc, SSM_CHUNK, *t.shape[2:]), 1, 0)

    tri = jnp.tril(jnp.ones((SSM_CHUNK, SSM_CHUNK), dtype=bool))[None, :, :, None, None]

    def step(state, inp):
        xc, dtc, bc, cc = inp
        la = jnp.cumsum(dtc * a, axis=1)
        seg = la[:, :, None] - la[:, None, :]
        decay = jnp.exp(jnp.where(tri, seg, -jnp.inf))
        cb = jnp.einsum('btgn,bsgn->btsg', cc, bc)
        y_intra = jnp.einsum('btsg,btsgh,bsghp->btghp', cb, decay, xc * dtc[..., None])
        y_inter = jnp.einsum('btgn,bghpn->btghp', cc, state) * jnp.exp(la)[..., None]
        to_end = jnp.exp(la[:, -1:] - la) * dtc
        new_state = (state * jnp.exp(la[:, -1])[..., None, None]
                     + jnp.einsum('bsgn,bsgh,bsghp->bghpn', bc, to_end, xc))
        return new_state, y_intra + y_inter

    state0 = jnp.zeros((bsz, SSM_GROUPS, SSM_HEADS_PER_GROUP, SSM_HEAD_DIM, SSM_STATE), jnp.float32)
    _, ys = lax.scan(step, state0, (chunks(x), chunks(dt), chunks(b), chunks(c)))
    ys = jnp.moveaxis(ys, 0, 1)
    return ys.reshape(bsz, s, *ys.shape[3:])


def ssd_mixer(h, w_in, conv_w, conv_b, dt_bias, a_log, d_skip, norm_g, w_out):
    bsz, s, _ = h.shape
    f32 = jnp.float32
    zxbcdt = h @ w_in
    z, xbc, dt = jnp.split(zxbcdt, [SSM_D_INNER, SSM_D_INNER + SSM_CONV_DIM], axis=-1)
    xbc = jax.nn.silu(causal_dwconv(xbc, conv_w, conv_b))
    xs, b_in, c_in = jnp.split(xbc, [SSM_D_INNER, SSM_D_INNER + SSM_GROUPS * SSM_STATE], axis=-1)
    dt = jax.nn.softplus(dt.astype(f32) + dt_bias.astype(f32))
    a = -jnp.exp(a_log.astype(f32))
    grp = (SSM_GROUPS, SSM_HEADS_PER_GROUP)
    xh = xs.astype(f32).reshape(bsz, s, *grp, SSM_HEAD_DIM)
    y = ssd_scan(xh, dt.reshape(bsz, s, *grp), a.reshape(grp),
                 b_in.astype(f32).reshape(bsz, s, SSM_GROUPS, SSM_STATE),
                 c_in.astype(f32).reshape(bsz, s, SSM_GROUPS, SSM_STATE))
    y = y + d_skip.astype(f32).reshape(grp)[..., None] * xh
    y = y.reshape(bsz, s, SSM_D_INNER) * jax.nn.silu(z.astype(f32))
    yg = y.reshape(bsz, s, SSM_GROUPS, SSM_D_INNER // SSM_GROUPS)
    yg = yg * lax.rsqrt(jnp.mean(yg * yg, axis=-1, keepdims=True) + SSM_NORM_EPS)
    y = yg.reshape(bsz, s, SSM_D_INNER).astype(h.dtype) * norm_g
    return y @ w_out


def diff_mixer(h, w_in, lam, subln_g, w_o, lambda_init):
    bsz, s, _ = h.shape
    proj = h @ w_in
    q, k, v = jnp.split(proj, 3, axis=-1)
    q = q.reshape(bsz, s, DIFF_HEADS, 2, DIFF_HEAD_DIM)
    k = k.reshape(bsz, s, DIFF_HEADS, 2, DIFF_HEAD_DIM)
    v = v.reshape(bsz, s, DIFF_HEADS, 2 * DIFF_HEAD_DIM)
    lam = lam.astype(jnp.float32)
    lam_full = (jnp.exp(jnp.sum(lam[0] * lam[1])) - jnp.exp(jnp.sum(lam[2] * lam[3])) + lambda_init)
    slopes = alibi_slopes(DIFF_HEADS)[:, None, None]
    kpos = jnp.arange(s)
    scale = DIFF_HEAD_DIM ** -0.5

    def block(args):
        qb, start = args
        sc = jnp.einsum('bqhmd,bkhmd->bmhqk', qb, k).astype(jnp.float32) * scale
        qpos = start + jnp.arange(Q_BLOCK)
        rel = (qpos[:, None] - kpos[None, :]).astype(jnp.float32)
        sc = jnp.where(rel >= 0, sc - slopes * rel, -jnp.inf)
        p = jax.nn.softmax(sc, axis=-1)
        attn = (p[:, 0] - lam_full * p[:, 1]).astype(v.dtype)
        return jnp.einsum('bhqk,bkhe->bqhe', attn, v)

    starts = jnp.arange(s // Q_BLOCK, dtype=jnp.int32) * Q_BLOCK
    out = from_blocks(lax.map(block, (to_blocks(q), starts)))
    out = rmsnorm(out, subln_g, eps=DIFF_SUBLN_EPS) * (1.0 - lambda_init)
    return out.reshape(bsz, s, D_MODEL) @ w_o


def conv_ffn(h, w_up, conv_w, conv_b, w_down):
    u = causal_dwconv(h @ w_up, conv_w, conv_b)
    g, val = jnp.split(u, 2, axis=-1)
    return (jax.nn.silu(g) * val) @ w_down


def setup_inputs(seed: int = 0) -> dict:
    key = jax.random.key(seed)
    ks = jax.random.split(key, 24)
    f32 = jnp.float32

    def nrm(k, shape, scale):
        return jax.random.normal(k, shape, f32) * scale

    def gain(k, shape):
        return 1.0 + 0.02 * jax.random.normal(k, shape, f32)

    dt = jnp.exp(jax.random.uniform(ks[10], (N_SSM, SSM_HEADS), f32, math.log(1e-3), math.log(1e-1)))
    return {
        "x": nrm(ks[0], (BATCH, SEQ, D_MODEL), 1.0),
        "mix_norm_g": gain(ks[1], (DEPTH, D_MODEL)),
        "ffn_norm_g": gain(ks[2], (DEPTH, D_MODEL)),
        "fox_w_in": nrm(ks[3], (N_FOX, D_MODEL, FOX_IN), D_MODEL ** -0.5),
        "fox_b_f": jax.random.uniform(ks[4], (N_FOX, FOX_HEADS), f32, 1.0, 6.0),
        "fox_w_o": nrm(ks[5], (N_FOX, D_MODEL, D_MODEL), D_MODEL ** -0.5),
        "ssm_w_in": nrm(ks[6], (N_SSM, D_MODEL, SSM_IN), D_MODEL ** -0.5),
        "ssm_conv_w": nrm(ks[7], (N_SSM, SSM_CONV, SSM_CONV_DIM), SSM_CONV ** -0.5),
        "ssm_conv_b": nrm(ks[8], (N_SSM, SSM_CONV_DIM), 0.02),
        "ssm_dt_bias": dt + jnp.log(-jnp.expm1(-dt)),
        "ssm_a_log": jnp.log(jax.random.uniform(ks[11], (N_SSM, SSM_HEADS), f32, 1.0, 16.0)),
        "ssm_d": gain(ks[12], (N_SSM, SSM_HEADS)),
        "ssm_norm_g": gain(ks[13], (N_SSM, SSM_D_INNER)),
        "ssm_w_out": nrm(ks[14], (N_SSM, SSM_D_INNER, D_MODEL), SSM_D_INNER ** -0.5),
        "diff_w_in": nrm(ks[15], (N_DIFF, D_MODEL, DIFF_IN), D_MODEL ** -0.5),
        "diff_lambda": nrm(ks[16], (N_DIFF, 4, DIFF_HEAD_DIM), 0.1),
        "diff_subln_g": gain(ks[17], (N_DIFF, 2 * DIFF_HEAD_DIM)),
        "diff_w_o": nrm(ks[18], (N_DIFF, D_MODEL, D_MODEL), D_MODEL ** -0.5),
        "ffn_w_up": nrm(ks[19], (DEPTH, D_MODEL, 2 * D_FF), D_MODEL ** -0.5),
        "ffn_conv_w": nrm(ks[20], (DEPTH, FFN_CONV, 2 * D_FF), FFN_CONV ** -0.5),
        "ffn_conv_b": nrm(ks[21], (DEPTH, 2 * D_FF), 0.02),
        "ffn_w_down": nrm(ks[22], (DEPTH, D_FF, D_MODEL), D_FF ** -0.5),
        "final_norm_g": gain(ks[23], (D_MODEL,)),
    }


def reference(x, mix_norm_g, ffn_norm_g, fox_w_in, fox_b_f, fox_w_o, ssm_w_in, ssm_conv_w,
              ssm_conv_b, ssm_dt_bias, ssm_a_log, ssm_d, ssm_norm_g, ssm_w_out, diff_w_in,
              diff_lambda, diff_subln_g, diff_w_o, ffn_w_up, ffn_conv_w, ffn_conv_b, ffn_w_down,
              final_norm_g):
    h = x
    for i in range(DEPTH):
        kind, j = i % N_MIXERS, i // N_MIXERS
        hn = rmsnorm(h, mix_norm_g[i])
        if kind == 0:
            y = fox_mixer(hn, fox_w_in[j], fox_b_f[j], fox_w_o[j])
        elif kind == 1:
            y = ssd_mixer(hn, ssm_w_in[j], ssm_conv_w[j], ssm_conv_b[j], ssm_dt_bias[j],
                          ssm_a_log[j], ssm_d[j], ssm_norm_g[j], ssm_w_out[j])
        else:
            lambda_init = 0.8 - 0.6 * math.exp(-0.3 * i)
            y = diff_mixer(hn, diff_w_in[j], diff_lambda[j], diff_subln_g[j], diff_w_o[j], lambda_init)
        h = h + y
        h = h + conv_ffn(rmsnorm(h, ffn_norm_g[i]), ffn_w_up[i], ffn_conv_w[i], ffn_conv_b[i], ffn_w_down[i])
    return rmsnorm(h, final_norm_g)
```

```python
import functools
import math

import jax
import jax.numpy as jnp
from jax import lax
from jax.experimental import pallas as pl
from jax.experimental.pallas import tpu as pltpu

F32 = jnp.float32
BF16 = jnp.bfloat16

V7X_VMEM_LIMIT_BYTES = 56 * 1024 * 1024
LANES = 128
BF16_SUBLANES = 16

NORM_EPS = 1e-6
SSM_NORM_EPS = 1e-5
DIFF_SUBLN_EPS = 1e-5
SSM_GROUPS = 8
SSM_STATE = 128
SSM_CHUNK = 128
N_MIXERS = 3

MM_TM = 1024
MM_TN = 1024
MM_TK_MAX = 4096
MM_TK_SPLIT = 3072
FFN_TM = 1024
FFN_TN = 512
ATTN_TQ = 512
NORM_TM = 256
CONV_TS = 512
CONV_TC = 1024


def _tile(dim, pref):
    if dim <= pref:
        return dim
    t = pref
    while dim % t:
        t //= 2
    return t


def _params(*sem):
    return pltpu.CompilerParams(dimension_semantics=sem, vmem_limit_bytes=V7X_VMEM_LIMIT_BYTES)


def _rmsnorm_kernel(x_ref, g_ref, o_ref, *, eps):
    x = x_ref[...]
    ms = jnp.mean(x * x, axis=-1, keepdims=True)
    o_ref[...] = ((x * lax.rsqrt(ms + eps)) * g_ref[...]).astype(o_ref.dtype)


def rmsnorm(x, g, eps, out_dtype):
    m, d = x.shape
    tm = _tile(m, NORM_TM)
    return pl.pallas_call(
        functools.partial(_rmsnorm_kernel, eps=eps),
        out_shape=jax.ShapeDtypeStruct((m, d), out_dtype),
        grid=(m // tm,),
        in_specs=[pl.BlockSpec((tm, d), lambda i: (i, 0)),
                  pl.BlockSpec((1, d), lambda i: (0, 0))],
        out_specs=pl.BlockSpec((tm, d), lambda i: (i, 0)),
        compiler_params=_params("parallel"),
        name="rmsnorm",
    )(x, g.reshape(1, d).astype(F32))


def _mm_kernel(*refs, nk, has_res):
    if has_res:
        x_ref, w_ref, r_ref, o_ref = refs[:4]
    else:
        x_ref, w_ref, o_ref = refs[:3]
        r_ref = None
    part = jnp.dot(x_ref[...], w_ref[...], preferred_element_type=F32)

    def finish(acc):
        if has_res:
            acc = acc + r_ref[...]
        o_ref[...] = acc.astype(o_ref.dtype)

    if nk == 1:
        finish(part)
        return
    acc_ref = refs[-1]
    k = pl.program_id(2)

    @pl.when(k == 0)
    def _():
        acc_ref[...] = part

    @pl.when(jnp.logical_and(k > 0, k < nk - 1))
    def _():
        acc_ref[...] += part

    @pl.when(k == nk - 1)
    def _():
        finish(acc_ref[...] + part)


def matmul(x, w, res=None, out_dtype=F32):
    m, kdim = x.shape
    n = w.shape[1]
    tm, tn = _tile(m, MM_TM), _tile(n, MM_TN)
    tk = kdim
    if kdim > MM_TK_MAX:
        tk = MM_TK_SPLIT
        while kdim % tk:
            tk -= 2 * LANES
    nk = kdim // tk
    in_specs = [pl.BlockSpec((tm, tk), lambda i, j, k: (i, k)),
                pl.BlockSpec((tk, tn), lambda i, j, k: (k, j))]
    args = [x, w]
    if res is not None:
        in_specs.append(pl.BlockSpec((tm, tn), lambda i, j, k: (i, j)))
        args.append(res)
    return pl.pallas_call(
        functools.partial(_mm_kernel, nk=nk, has_res=res is not None),
        out_shape=jax.ShapeDtypeStruct((m, n), out_dtype),
        grid=(m // tm, n // tn, nk),
        in_specs=in_specs,
        out_specs=pl.BlockSpec((tm, tn), lambda i, j, k: (i, j)),
        scratch_shapes=[pltpu.VMEM((tm, tn), F32)] if nk > 1 else [],
        compiler_params=_params("parallel", "parallel", "arbitrary"),
        name="matmul",
    )(*args)


def _causal_conv_rows(p, halo, w, b):
    kw = w.shape[0]
    hr = halo.shape[0]
    top = p[0:hr]
    u = b + w[kw - 1:kw] * p
    ut = b + w[kw - 1:kw] * top
    row = lax.broadcasted_iota(jnp.int32, top.shape, 0)
    for k in range(1, kw):
        wk = w[kw - 1 - k:kw - k]
        u = u + wk * pltpu.roll(p, k, 0)
        shifted_top = jnp.where(row < k, pltpu.roll(halo, k, 0), pltpu.roll(top, k, 0))
        ut = ut + wk * shifted_top
    return jnp.concatenate([ut, u[hr:]], axis=0)


def _silu(x):
    return x / (1.0 + jnp.exp(-x))


def _ffn_up_kernel(x_ref, wg_ref, wv_ref, cwg_ref, cwv_ref, cbg_ref, cbv_ref, o_ref,
                   halo_g, halo_v, *, tiles_per_seq):
    i = pl.program_id(1)

    @pl.when(i % tiles_per_seq == 0)
    def _():
        halo_g[...] = jnp.zeros_like(halo_g)
        halo_v[...] = jnp.zeros_like(halo_v)

    x = x_ref[...]
    pg = jnp.dot(x, wg_ref[...], preferred_element_type=F32)
    pv = jnp.dot(x, wv_ref[...], preferred_element_type=F32)
    ug = _causal_conv_rows(pg, halo_g[...], cwg_ref[...], cbg_ref[...])
    uv = _causal_conv_rows(pv, halo_v[...], cwv_ref[...], cbv_ref[...])
    hr = halo_g.shape[0]
    halo_g[...] = pg[pg.shape[0] - hr:]
    halo_v[...] = pv[pv.shape[0] - hr:]
    o_ref[...] = (_silu(ug) * uv).astype(o_ref.dtype)


def ffn_up(x, wg, wv, cwg, cwv, cbg, cbv, seq):
    m, d = x.shape
    f = wg.shape[1]
    tm, tn = _tile(seq, FFN_TM), _tile(f, FFN_TN)
    kw = cwg.shape[0]
    wspec = pl.BlockSpec((d, tn), lambda j, i: (0, j))
    cspec = pl.BlockSpec((kw, tn), lambda j, i: (0, j))
    bspec = pl.BlockSpec((1, tn), lambda j, i: (0, j))
    return pl.pallas_call(
        functools.partial(_ffn_up_kernel, tiles_per_seq=seq // tm),
        out_shape=jax.ShapeDtypeStruct((m, f), BF16),
        grid=(f // tn, m // tm),
        in_specs=[pl.BlockSpec((tm, d), lambda j, i: (i, 0)), wspec, wspec, cspec, cspec, bspec, bspec],
        out_specs=pl.BlockSpec((tm, tn), lambda j, i: (i, j)),
        scratch_shapes=[pltpu.VMEM((BF16_SUBLANES, tn), F32), pltpu.VMEM((BF16_SUBLANES, tn), F32)],
        compiler_params=_params("arbitrary", "arbitrary"),
        name="ffn_up",
    )(x, wg, wv, cwg, cwv, cbg, cbv)


def _cumsum_rows(y):
    n = y.shape[0]
    row = lax.broadcasted_iota(jnp.int32, y.shape, 0)
    shift = 1
    while shift < n:
        y = y + jnp.where(row >= shift, pltpu.roll(y, shift, 0), 0.0)
        shift *= 2
    return y


def _fox_gate_kernel(f_ref, b_ref, o_ref):
    x = f_ref[...] + b_ref[...]
    log_f = jnp.minimum(x, 0.0) - jnp.log(1.0 + jnp.exp(-jnp.abs(x)))
    o_ref[...] = _cumsum_rows(log_f)


def fox_gate_cumsum(f_logit, b_f, seq):
    m, hp = f_logit.shape
    return pl.pallas_call(
        _fox_gate_kernel,
        out_shape=jax.ShapeDtypeStruct((m, hp), F32),
        grid=(m // seq,),
        in_specs=[pl.BlockSpec((seq, hp), lambda b: (b, 0)), pl.BlockSpec((1, hp), lambda b: (0, 0))],
        out_specs=pl.BlockSpec((seq, hp), lambda b: (b, 0)),
        compiler_params=_params("parallel"),
        name="fox_gate_cumsum",
    )(f_logit, b_f)


def _online_softmax_step(s, v, carry, off=None):
    m, l, acc = carry
    smax = jnp.max(s, axis=-1, keepdims=True)
    if off is None:
        m_new = jnp.maximum(m, smax)
        shift = m_new
    else:
        m_new = jnp.maximum(m, smax - off)
        shift = m_new + off
    alpha = jnp.exp(m - m_new)
    p = jnp.exp(s - shift)
    l = alpha * l + jnp.sum(p, axis=-1, keepdims=True)
    acc = alpha * acc + jnp.dot(p.astype(v.dtype), v, preferred_element_type=F32)
    return m_new, l, acc


def _qk(q, k):
    return lax.dot_general(q, k, (((1,), (1,)), ((), ())), preferred_element_type=F32)


def _fox_attn_kernel(q_ref, k_ref, v_ref, fq_ref, fk_ref, o_ref, *, tq, scale):
    h = pl.program_id(1)
    qi = pl.program_id(2)
    q = q_ref[...]
    fq_all = fq_ref[...]
    lane = lax.broadcasted_iota(jnp.int32, fq_all.shape, 1)
    fq = jnp.sum(jnp.where(lane == h, fq_all, 0.0), axis=-1, keepdims=True)
    hd = v_ref.shape[-1]

    def logits(j):
        start = pl.multiple_of(j * tq, tq)
        s = _qk(q, k_ref[pl.ds(start, tq), :]) * scale
        return s + (fq - fk_ref[pl.ds(j, 1), :]), v_ref[pl.ds(start, tq), :]

    def body(j, carry):
        s, v = logits(j)
        return _online_softmax_step(s, v, carry)

    init = (jnp.full((tq, 1), -jnp.inf, F32), jnp.zeros((tq, 1), F32), jnp.zeros((tq, hd), F32))
    carry = lax.fori_loop(0, qi, body, init)
    s, v = logits(qi)
    r = lax.broadcasted_iota(jnp.int32, s.shape, 0)
    c = lax.broadcasted_iota(jnp.int32, s.shape, 1)
    _, l, acc = _online_softmax_step(jnp.where(c <= r, s, -jnp.inf), v, carry)
    o_ref[...] = (acc / l).astype(o_ref.dtype)


def fox_attention(qkv, cum, cum_t, batch, seq, heads, hd):
    m = qkv.shape[0]
    tq = _tile(seq, ATTN_TQ)
    nq = seq // tq
    hp = cum.shape[1]
    return pl.pallas_call(
        functools.partial(_fox_attn_kernel, tq=tq, scale=hd ** -0.5),
        out_shape=jax.ShapeDtypeStruct((m, heads * hd), BF16),
        grid=(batch, heads, nq),
        in_specs=[pl.BlockSpec((tq, hd), lambda b, h, i: (b * nq + i, h)),
                  pl.BlockSpec((seq, hd), lambda b, h, i: (b, heads + h)),
                  pl.BlockSpec((seq, hd), lambda b, h, i: (b, 2 * heads + h)),
                  pl.BlockSpec((tq, hp), lambda b, h, i: (b * nq + i, 0)),
                  pl.BlockSpec((None, nq, tq), lambda b, h, i: (b * heads + h, 0, 0))],
        out_specs=pl.BlockSpec((tq, hd), lambda b, h, i: (b * nq + i, h)),
        compiler_params=_params("parallel", "parallel", "parallel"),
        name="fox_attention",
    )(qkv, qkv, qkv, cum, cum_t)


def _diff_attn_kernel(slope_ref, q_ref, k_ref, v_ref, lam_ref, g_ref, o_ref, *,
                      tq, hd, scale, lambda_init, eps):
    h = pl.program_id(1)
    qi = pl.program_id(2)
    slope = slope_ref[h]
    q = q_ref[...]
    r = lax.broadcasted_iota(jnp.int32, (tq, tq), 0)
    c = lax.broadcasted_iota(jnp.int32, (tq, tq), 1)
    rel = (r - c).astype(F32)
    bias = slope * rel
    ev = v_ref.shape[-1]

    def tile(j):
        start = pl.multiple_of(j * tq, tq)
        k = k_ref[pl.ds(start, tq), :]
        v = v_ref[pl.ds(start, tq), :]
        return [_qk(q[:, a * hd:(a + 1) * hd], k[:, a * hd:(a + 1) * hd]) * scale - bias
                for a in range(2)], v

    def body(j, carry):
        ss, v = tile(j)
        off = slope * ((qi - j) * tq).astype(F32)
        return tuple(_online_softmax_step(ss[a], v, carry[a], off) for a in range(2))

    one = (jnp.full((tq, 1), -jnp.inf, F32), jnp.zeros((tq, 1), F32), jnp.zeros((tq, ev), F32))
    carry = lax.fori_loop(0, qi, body, (one, one))
    ss, v = tile(qi)
    outs = []
    for a in range(2):
        _, l, acc = _online_softmax_step(jnp.where(rel >= 0, ss[a], -jnp.inf), v, carry[a])
        outs.append(acc / l)
    lam = lam_ref[...]
    lam_full = (jnp.exp(jnp.sum(lam[0:1] * lam[1:2], axis=-1, keepdims=True))
                - jnp.exp(jnp.sum(lam[2:3] * lam[3:4], axis=-1, keepdims=True)) + lambda_init)
    out = outs[0] - lam_full * outs[1]
    ms = jnp.mean(out * out, axis=-1, keepdims=True)
    y = (out * lax.rsqrt(ms + eps)) * g_ref[...]
    o_ref[...] = (y * (1.0 - lambda_init)).astype(o_ref.dtype)


def diff_attention(qkv, lam, subln_g, batch, seq, heads, hd, lambda_init):
    m = qkv.shape[0]
    tq = _tile(seq, ATTN_TQ)
    nq = seq // tq
    ev = 2 * hd
    slopes = jnp.exp2(-8.0 * jnp.arange(1, heads + 1, dtype=F32) / heads)
    grid_spec = pltpu.PrefetchScalarGridSpec(
        num_scalar_prefetch=1,
        grid=(batch, heads, nq),
        in_specs=[pl.BlockSpec((tq, ev), lambda b, h, i, s: (b * nq + i, h)),
                  pl.BlockSpec((seq, ev), lambda b, h, i, s: (b, heads + h)),
                  pl.BlockSpec((seq, ev), lambda b, h, i, s: (b, 2 * heads + h)),
                  pl.BlockSpec((4, hd), lambda b, h, i, s: (0, 0)),
                  pl.BlockSpec((1, ev), lambda b, h, i, s: (0, 0))],
        out_specs=pl.BlockSpec((tq, ev), lambda b, h, i, s: (b * nq + i, h)),
    )
    return pl.pallas_call(
        functools.partial(_diff_attn_kernel, tq=tq, hd=hd, scale=hd ** -0.5,
                          lambda_init=lambda_init, eps=DIFF_SUBLN_EPS),
        out_shape=jax.ShapeDtypeStruct((m, heads * ev), BF16),
        grid_spec=grid_spec,
        compiler_params=_params("parallel", "parallel", "parallel"),
        name="diff_attention",
    )(slopes, qkv, qkv, qkv, lam.astype(F32), subln_g.reshape(1, ev).astype(F32))


def _ssm_conv_kernel(x_ref, halo_ref, w_ref, b_ref, o_ref):
    i = pl.program_id(1)
    halo = jnp.where(i == 0, 0.0, halo_ref[...].astype(F32))
    u = _causal_conv_rows(x_ref[...].astype(F32), halo, w_ref[...], b_ref[...])
    o_ref[...] = _silu(u).astype(o_ref.dtype)


def ssm_conv_silu(xbc, w, b, batch, seq):
    m, cdim = xbc.shape
    ts, tc = _tile(seq, CONV_TS), _tile(cdim, CONV_TC)
    ns = seq // ts
    hb = ts // BF16_SUBLANES
    return pl.pallas_call(
        _ssm_conv_kernel,
        out_shape=jax.ShapeDtypeStruct((m, cdim), BF16),
        grid=(batch, ns, cdim // tc),
        in_specs=[pl.BlockSpec((ts, tc), lambda bi, i, j: (bi * ns + i, j)),
                  pl.BlockSpec((BF16_SUBLANES, tc),
                               lambda bi, i, j: (jnp.maximum((bi * ns + i) * hb - 1, 0), j)),
                  pl.BlockSpec((w.shape[0], tc), lambda bi, i, j: (0, j)),
                  pl.BlockSpec((1, tc), lambda bi, i, j: (0, j))],
        out_specs=pl.BlockSpec((ts, tc), lambda bi, i, j: (bi * ns + i, j)),
        compiler_params=_params("parallel", "parallel", "parallel"),
        name="ssm_conv_silu",
    )(xbc, xbc, w.astype(F32), b.reshape(1, cdim).astype(F32))


def _ssm_dt_kernel(dt_ref, bias_ref, alog_ref, dt_out, la_out):
    x = dt_ref[...] + bias_ref[...]
    dt = jnp.maximum(x, 0.0) + jnp.log(1.0 + jnp.exp(-jnp.abs(x)))
    dt_out[...] = dt
    la_out[...] = _cumsum_rows(dt * (-jnp.exp(alog_ref[...])))


def ssm_dt(dt_raw, dt_bias, a_log):
    m, h = dt_raw.shape
    spec = pl.BlockSpec((SSM_CHUNK, h), lambda i: (i, 0))
    pspec = pl.BlockSpec((1, h), lambda i: (0, 0))
    return pl.pallas_call(
        _ssm_dt_kernel,
        out_shape=(jax.ShapeDtypeStruct((m, h), F32), jax.ShapeDtypeStruct((m, h), F32)),
        grid=(m // SSM_CHUNK,),
        in_specs=[spec, pspec, pspec],
        out_specs=(spec, spec),
        compiler_params=_params("parallel"),
        name="ssm_dt",
    )(dt_raw, dt_bias.reshape(1, h).astype(F32), a_log.reshape(1, h).astype(F32))


def _ssd_scan_kernel(x_ref, b_ref, c_ref, dt_ref, la_ref, lat_ref, z_ref, d_ref, g_ref, o_ref,
                     state_ref, *, hpg, pdim, eps):
    ci = pl.program_id(2)

    @pl.when(ci == 0)
    def _():
        state_ref[...] = jnp.zeros_like(state_ref)

    chunk = x_ref.shape[0]
    npair = hpg // 2
    x = x_ref[...].astype(F32)
    bm = b_ref[...]
    cm = c_ref[...]
    dt = dt_ref[...]
    la = la_ref[...]
    lat = lat_ref[...]
    lane = lax.broadcasted_iota(jnp.int32, (chunk, 2 * pdim), 1)
    first = lane < pdim

    def expand(a):
        lo = lax.broadcasted_iota(jnp.int32, (a.shape[0], 2 * pdim), 1) < pdim
        return jnp.concatenate(
            [jnp.where(lo, a[:, 2 * p:2 * p + 1], a[:, 2 * p + 1:2 * p + 2]) for p in range(npair)], axis=1)

    dt_e = expand(dt)
    la_e = expand(la)
    la_end_e = expand(la[chunk - 1:chunk])
    cb = _qk(cm, bm)
    tri = (lax.broadcasted_iota(jnp.int32, (chunk, chunk), 0)
           >= lax.broadcasted_iota(jnp.int32, (chunk, chunk), 1))
    xdt = x * dt_e
    y_parts = []
    for p in range(npair):
        ms = []
        for hh in (2 * p, 2 * p + 1):
            seg = la[:, hh:hh + 1] - lat[hh:hh + 1, :]
            ms.append((cb * jnp.exp(jnp.where(tri, seg, -jnp.inf))).astype(BF16))
        xp = xdt[:, p * 2 * pdim:(p + 1) * 2 * pdim]
        rhs = jnp.concatenate([jnp.where(first, xp, 0.0), jnp.where(first, 0.0, xp)], axis=0).astype(BF16)
        y_parts.append(jnp.dot(jnp.concatenate(ms, axis=1), rhs, preferred_element_type=F32))
    y = jnp.concatenate(y_parts, axis=1)
    state = state_ref[...]
    y = y + jnp.dot(cm, state.astype(BF16), preferred_element_type=F32) * jnp.exp(la_e)
    to_end = jnp.exp(la_end_e - la_e) * dt_e
    upd = lax.dot_general(bm, (x * to_end).astype(BF16), (((0,), (0,)), ((), ())),
                          preferred_element_type=F32)
    state_ref[...] = state * jnp.exp(la_end_e) + upd
    y = y + d_ref[...] * x
    y = y * _silu(z_ref[...].astype(F32))
    y = y * lax.rsqrt(jnp.mean(y * y, axis=-1, keepdims=True) + eps)
    o_ref[...] = (y * g_ref[...]).astype(o_ref.dtype)


def ssd_scan(xbc, z, dt_g, la_g, lat_g, d_e, norm_g, batch, seq, d_inner, heads):
    m = xbc.shape[0]
    groups, nstate, chunk = SSM_GROUPS, SSM_STATE, SSM_CHUNK
    hpg = heads // groups
    pdim = d_inner // heads
    gw = hpg * pdim
    nc = seq // chunk
    xb = d_inner // nstate
    row = lambda b, g, c: b * nc + c
    return pl.pallas_call(
        functools.partial(_ssd_scan_kernel, hpg=hpg, pdim=pdim, eps=SSM_NORM_EPS),
        out_shape=jax.ShapeDtypeStruct((m, d_inner), BF16),
        grid=(batch, groups, nc),
        in_specs=[pl.BlockSpec((chunk, gw), lambda b, g, c: (row(b, g, c), g)),
                  pl.BlockSpec((chunk, nstate), lambda b, g, c: (row(b, g, c), xb + g)),
                  pl.BlockSpec((chunk, nstate), lambda b, g, c: (row(b, g, c), xb + groups + g)),
                  pl.BlockSpec((None, chunk, hpg), lambda b, g, c: (g, row(b, g, c), 0)),
                  pl.BlockSpec((None, chunk, hpg), lambda b, g, c: (g, row(b, g, c), 0)),
                  pl.BlockSpec((None, hpg, chunk), lambda b, g, c: (g, 0, row(b, g, c))),
                  pl.BlockSpec((chunk, gw), lambda b, g, c: (row(b, g, c), g)),
                  pl.BlockSpec((1, gw), lambda b, g, c: (0, g)),
                  pl.BlockSpec((1, gw), lambda b, g, c: (0, g))],
        out_specs=pl.BlockSpec((chunk, gw), lambda b, g, c: (row(b, g, c), g)),
        scratch_shapes=[pltpu.VMEM((nstate, gw), F32)],
        compiler_params=_params("parallel", "parallel", "arbitrary"),
        name="ssd_scan",
    )(xbc, xbc, xbc, dt_g, la_g, lat_g, z, d_e, norm_g)


def _pad_cols(a, n):
    return jnp.pad(a, ((0, 0), (0, n - a.shape[1])))


def fox_mixer(h, hn, w_in, b_f, w_o, batch, seq):
    d = hn.shape[1]
    heads = b_f.shape[0]
    hd = d // heads
    hp = -(-heads // LANES) * LANES
    qkv = matmul(hn, w_in[:, :3 * d].astype(BF16), out_dtype=BF16)
    f_logit = matmul(hn, _pad_cols(w_in[:, 3 * d:], hp).astype(BF16))
    cum = fox_gate_cumsum(f_logit, _pad_cols(b_f.reshape(1, heads).astype(F32), hp), seq)
    tq = _tile(seq, ATTN_TQ)
    cum_t = jnp.transpose(cum.reshape(batch, seq, hp)[:, :, :heads], (0, 2, 1))
    cum_t = cum_t.reshape(batch * heads, seq // tq, tq)
    attn = fox_attention(qkv, cum, cum_t, batch, seq, heads, hd)
    return matmul(attn, w_o.astype(BF16), res=h)


def ssd_mixer(h, hn, w_in, conv_w, conv_b, dt_bias, a_log, d_skip, norm_g, w_out, batch, seq):
    heads = a_log.shape[0]
    d_inner = w_out.shape[0]
    conv_dim = conv_w.shape[1]
    groups = SSM_GROUPS
    hpg = heads // groups
    m = hn.shape[0]
    z = matmul(hn, w_in[:, :d_inner].astype(BF16), out_dtype=BF16)
    xbc = matmul(hn, w_in[:, d_inner:d_inner + conv_dim].astype(BF16), out_dtype=BF16)
    dt_raw = matmul(hn, w_in[:, d_inner + conv_dim:].astype(BF16))
    xbc = ssm_conv_silu(xbc, conv_w, conv_b, batch, seq)
    dt, la = ssm_dt(dt_raw, dt_bias, a_log)
    dt_g = jnp.transpose(dt.reshape(m, groups, hpg), (1, 0, 2))
    la_g = jnp.transpose(la.reshape(m, groups, hpg), (1, 0, 2))
    lat_g = jnp.transpose(la_g, (0, 2, 1))
    pdim = d_inner // heads
    d_e = jnp.repeat(d_skip.astype(F32), pdim).reshape(1, d_inner)
    y = ssd_scan(xbc, z, dt_g, la_g, lat_g, d_e, norm_g.reshape(1, d_inner).astype(F32),
                 batch, seq, d_inner, heads)
    return matmul(y, w_out.astype(BF16), res=h)


def diff_mixer(h, hn, w_in, lam, subln_g, w_o, lambda_init, batch, seq):
    d = hn.shape[1]
    hd = lam.shape[1]
    heads = d // (2 * hd)
    qkv = matmul(hn, w_in.astype(BF16), out_dtype=BF16)
    attn = diff_attention(qkv, lam, subln_g, batch, seq, heads, hd, lambda_init)
    return matmul(attn, w_o.astype(BF16), res=h)


def conv_ffn(h, hn, w_up, conv_w, conv_b, w_down, seq):
    f = w_down.shape[0]
    fp = -(-f // FFN_TN) * FFN_TN
    wg = _pad_cols(w_up[:, :f], fp).astype(BF16)
    wv = _pad_cols(w_up[:, f:], fp).astype(BF16)
    cw = conv_w.astype(F32)
    cb = conv_b.reshape(1, 2 * f).astype(F32)
    act = ffn_up(hn, wg, wv, _pad_cols(cw[:, :f], fp), _pad_cols(cw[:, f:], fp),
                 _pad_cols(cb[:, :f], fp), _pad_cols(cb[:, f:], fp), seq)
    wd = jnp.pad(w_down, ((0, fp - f), (0, 0))).astype(BF16)
    return matmul(act, wd, res=h)


def kernel(x, mix_norm_g, ffn_norm_g, fox_w_in, fox_b_f, fox_w_o, ssm_w_in, ssm_conv_w, ssm_conv_b,
           ssm_dt_bias, ssm_a_log, ssm_d, ssm_norm_g, ssm_w_out, diff_w_in, diff_lambda, diff_subln_g,
           diff_w_o, ffn_w_up, ffn_conv_w, ffn_conv_b, ffn_w_down, final_norm_g):
    batch, seq, d = x.shape
    depth = mix_norm_g.shape[0]
    h = x.reshape(batch * seq, d)
    for i in range(depth):
        kind, j = i % N_MIXERS, i // N_MIXERS
        hn = rmsnorm(h, mix_norm_g[i], NORM_EPS, BF16)
        if kind == 0:
            h = fox_mixer(h, hn, fox_w_in[j], fox_b_f[j], fox_w_o[j], batch, seq)
        elif kind == 1:
            h = ssd_mixer(h, hn, ssm_w_in[j], ssm_conv_w[j], ssm_conv_b[j], ssm_dt_bias[j], ssm_a_log[j],
                          ssm_d[j], ssm_norm_g[j], ssm_w_out[j], batch, seq)
        else:
            lambda_init = 0.8 - 0.6 * math.exp(-0.3 * i)
            h = diff_mixer(h, hn, diff_w_in[j], diff_lambda[j], diff_subln_g[j], diff_w_o[j],
                           lambda_init, batch, seq)
        hn = rmsnorm(h, ffn_norm_g[i], NORM_EPS, BF16)
        h = conv_ffn(h, hn, ffn_w_up[i], ffn_conv_w[i], ffn_conv_b[i], ffn_w_down[i], seq)
    return rmsnorm(h, final_norm_g, NORM_EPS, F32).reshape(batch, seq, d)
```

```python
import functools
import math

import jax
import jax.numpy as jnp
from jax import lax
from jax.experimental import pallas as pl
from jax.experimental.pallas import tpu as pltpu

F32 = jnp.float32
BF16 = jnp.bfloat16
LOG2E = 1.4426950408889634

V7X_VMEM_LIMIT_BYTES = 56 * 1024 * 1024
LANES = 128
BF16_SUBLANES = 16

NORM_EPS = 1e-6
SSM_NORM_EPS = 1e-5
DIFF_SUBLN_EPS = 1e-5
SSM_GROUPS = 8
SSM_STATE = 128
SSM_CHUNK = 128
N_MIXERS = 3

MM_TM = 1024
MM_TN = 1024
MM_TK_MAX = 4096
MM_TK_SPLIT = 3072
FFN_TM = 1024
FFN_TN = 512
ATTN_TQ = 1024
FOX_HEADS_PER_STEP = 2
NORM_TM = 256
CONV_TS = 512
CONV_TC = 1024
CAST_TR = 1024
CAST_TC = 1024


def _tile(dim, pref):
    if dim <= pref:
        return dim
    t = pref
    while dim % t:
        t //= 2
    return t


def _round_up(x, m):
    return -(-x // m) * m


def _params(*sem):
    return pltpu.CompilerParams(dimension_semantics=sem, vmem_limit_bytes=V7X_VMEM_LIMIT_BYTES)


def _cast_kernel(*refs, rows, cols, padded, has_scale):
    if has_scale:
        w_ref, s_ref, o_ref = refs
    else:
        w_ref, o_ref = refs
    tr, tc = o_ref.shape
    w = w_ref[...]
    if has_scale:
        w = w * s_ref[...]
    if padded:
        r = lax.broadcasted_iota(jnp.int32, (tr, tc), 0) + pl.program_id(0) * tr
        c = lax.broadcasted_iota(jnp.int32, (tr, tc), 1) + pl.program_id(1) * tc
        w = jnp.where(jnp.logical_and(r < rows, c < cols), w, 0.0)
    o_ref[...] = w.astype(o_ref.dtype)


def cast_pad(w, layer, col_start=0, ncols=None, out_rows=None, out_cols=None, colscale=None, tc=CAST_TC):
    _, rows, wcols = w.shape
    ncols = wcols - col_start if ncols is None else ncols
    out_rows = rows if out_rows is None else out_rows
    out_cols = ncols if out_cols is None else out_cols
    tr = _tile(out_rows, CAST_TR)
    tc = _tile(out_cols, tc)
    assert col_start % tc == 0
    cb0 = col_start // tc
    last_rb = (rows - 1) // tr
    last_cb = (wcols - 1) // tc
    in_specs = [pl.BlockSpec((None, tr, tc),
                             lambda i, j: (layer, jnp.minimum(i, last_rb), jnp.minimum(j + cb0, last_cb)))]
    args = [w]
    if colscale is not None:
        in_specs.append(pl.BlockSpec((1, tc), lambda i, j: (0, j)))
        args.append(colscale.reshape(1, out_cols).astype(F32))
    padded = out_rows != rows or out_cols != ncols
    return pl.pallas_call(
        functools.partial(_cast_kernel, rows=rows, cols=ncols, padded=padded,
                          has_scale=colscale is not None),
        out_shape=jax.ShapeDtypeStruct((out_rows, out_cols), BF16),
        grid=(out_rows // tr, out_cols // tc),
        in_specs=in_specs,
        out_specs=pl.BlockSpec((tr, tc), lambda i, j: (i, j)),
        compiler_params=_params("parallel", "parallel"),
        name="cast_pad",
    )(*args)


def _rmsnorm_kernel(x_ref, g_ref, o_ref, *, eps):
    x = x_ref[...]
    ms = jnp.mean(x * x, axis=-1, keepdims=True)
    o_ref[...] = ((x * lax.rsqrt(ms + eps)) * g_ref[...]).astype(o_ref.dtype)


def rmsnorm(x, g, eps, out_dtype):
    m, d = x.shape
    tm = _tile(m, NORM_TM)
    return pl.pallas_call(
        functools.partial(_rmsnorm_kernel, eps=eps),
        out_shape=jax.ShapeDtypeStruct((m, d), out_dtype),
        grid=(m // tm,),
        in_specs=[pl.BlockSpec((tm, d), lambda i: (i, 0)),
                  pl.BlockSpec((1, d), lambda i: (0, 0))],
        out_specs=pl.BlockSpec((tm, d), lambda i: (i, 0)),
        compiler_params=_params("parallel"),
        name="rmsnorm",
    )(x, g.reshape(1, d).astype(F32))


def _mm_kernel(*refs, nk, has_res):
    if has_res:
        x_ref, w_ref, r_ref, o_ref = refs[:4]
    else:
        x_ref, w_ref, o_ref = refs[:3]
        r_ref = None
    part = jnp.dot(x_ref[...], w_ref[...], preferred_element_type=F32)

    def finish(acc):
        if has_res:
            acc = acc + r_ref[...]
        o_ref[...] = acc.astype(o_ref.dtype)

    if nk == 1:
        finish(part)
        return
    acc_ref = refs[-1]
    k = pl.program_id(2)

    @pl.when(k == 0)
    def _():
        acc_ref[...] = part

    @pl.when(jnp.logical_and(k > 0, k < nk - 1))
    def _():
        acc_ref[...] += part

    @pl.when(k == nk - 1)
    def _():
        finish(acc_ref[...] + part)


def matmul(x, w, res=None, out_dtype=F32):
    m, kdim = x.shape
    n = w.shape[1]
    tm, tn = _tile(m, MM_TM), _tile(n, MM_TN)
    tk = kdim
    if kdim > MM_TK_MAX:
        tk = MM_TK_SPLIT
        while kdim % tk:
            tk -= 2 * LANES
    nk = kdim // tk
    in_specs = [pl.BlockSpec((tm, tk), lambda i, j, k: (i, k)),
                pl.BlockSpec((tk, tn), lambda i, j, k: (k, j))]
    args = [x, w]
    if res is not None:
        in_specs.append(pl.BlockSpec((tm, tn), lambda i, j, k: (i, j)))
        args.append(res)
    return pl.pallas_call(
        functools.partial(_mm_kernel, nk=nk, has_res=res is not None),
        out_shape=jax.ShapeDtypeStruct((m, n), out_dtype),
        grid=(m // tm, n // tn, nk),
        in_specs=in_specs,
        out_specs=pl.BlockSpec((tm, tn), lambda i, j, k: (i, j)),
        scratch_shapes=[pltpu.VMEM((tm, tn), F32)] if nk > 1 else [],
        compiler_params=_params("parallel", "parallel", "arbitrary"),
        name="matmul",
    )(*args)


def _causal_conv_rows(p, halo, w, b):
    kw = w.shape[0]
    hr = halo.shape[0]
    top = p[0:hr]
    u = b + w[kw - 1:kw] * p
    ut = b + w[kw - 1:kw] * top
    row = lax.broadcasted_iota(jnp.int32, top.shape, 0)
    for k in range(1, kw):
        wk = w[kw - 1 - k:kw - k]
        u = u + wk * pltpu.roll(p, k, 0)
        shifted_top = jnp.where(row < k, pltpu.roll(halo, k, 0), pltpu.roll(top, k, 0))
        ut = ut + wk * shifted_top
    return jnp.concatenate([ut, u[hr:]], axis=0)


def _silu(x):
    return x / (1.0 + jnp.exp(-x))


def _ffn_up_kernel(x_ref, wg_ref, wv_ref, cwg_ref, cwv_ref, cbg_ref, cbv_ref, o_ref,
                   halo_g, halo_v, *, tiles_per_seq):
    i = pl.program_id(1)

    @pl.when(i % tiles_per_seq == 0)
    def _():
        halo_g[...] = jnp.zeros_like(halo_g)
        halo_v[...] = jnp.zeros_like(halo_v)

    x = x_ref[...]
    pg = jnp.dot(x, wg_ref[...], preferred_element_type=F32)
    pv = jnp.dot(x, wv_ref[...], preferred_element_type=F32)
    ug = _causal_conv_rows(pg, halo_g[...], cwg_ref[...], cbg_ref[...])
    uv = _causal_conv_rows(pv, halo_v[...], cwv_ref[...], cbv_ref[...])
    hr = halo_g.shape[0]
    halo_g[...] = pg[pg.shape[0] - hr:]
    halo_v[...] = pv[pv.shape[0] - hr:]
    o_ref[...] = (_silu(ug) * uv).astype(o_ref.dtype)


def ffn_up(x, wg, wv, cwg, cwv, cbg, cbv, seq):
    m, d = x.shape
    f = wg.shape[1]
    tm, tn = _tile(seq, FFN_TM), _tile(f, FFN_TN)
    kw = cwg.shape[0]
    wspec = pl.BlockSpec((d, tn), lambda j, i: (0, j))
    cspec = pl.BlockSpec((kw, tn), lambda j, i: (0, j))
    bspec = pl.BlockSpec((1, tn), lambda j, i: (0, j))
    return pl.pallas_call(
        functools.partial(_ffn_up_kernel, tiles_per_seq=seq // tm),
        out_shape=jax.ShapeDtypeStruct((m, f), BF16),
        grid=(f // tn, m // tm),
        in_specs=[pl.BlockSpec((tm, d), lambda j, i: (i, 0)), wspec, wspec, cspec, cspec, bspec, bspec],
        out_specs=pl.BlockSpec((tm, tn), lambda j, i: (i, j)),
        scratch_shapes=[pltpu.VMEM((BF16_SUBLANES, tn), F32), pltpu.VMEM((BF16_SUBLANES, tn), F32)],
        compiler_params=_params("arbitrary", "arbitrary"),
        name="ffn_up",
    )(x, wg, wv, cwg, cwv, cbg, cbv)


def _cumsum_rows(y):
    n = y.shape[0]
    row = lax.broadcasted_iota(jnp.int32, y.shape, 0)
    shift = 1
    while shift < n:
        y = y + jnp.where(row >= shift, pltpu.roll(y, shift, 0), 0.0)
        shift *= 2
    return y


def _fox_gate_kernel(f_ref, b_ref, o_ref):
    x = f_ref[...] + b_ref[...]
    log_f = jnp.minimum(x, 0.0) - jnp.log(1.0 + jnp.exp(-jnp.abs(x)))
    o_ref[...] = _cumsum_rows(log_f)


def fox_gate_cumsum(f_logit, b_f, seq):
    m, hp = f_logit.shape
    return pl.pallas_call(
        _fox_gate_kernel,
        out_shape=jax.ShapeDtypeStruct((m, hp), F32),
        grid=(m // seq,),
        in_specs=[pl.BlockSpec((seq, hp), lambda b: (b, 0)), pl.BlockSpec((1, hp), lambda b: (0, 0))],
        out_specs=pl.BlockSpec((seq, hp), lambda b: (b, 0)),
        compiler_params=_params("parallel"),
        name="fox_gate_cumsum",
    )(f_logit, b_f)


def _bias_lanes(x, pieces_first):
    hi = x.astype(BF16).astype(F32)
    rem = x - hi
    lo = rem.astype(BF16).astype(F32)
    lo2 = rem - lo
    lane = lax.broadcasted_iota(jnp.int32, (x.shape[0], LANES), 1)
    base = 0 if pieces_first else 3
    ones = 3 if pieces_first else 0
    out = jnp.where(lane == base, hi, jnp.where(lane == base + 1, lo, jnp.where(lane == base + 2, lo2, 0.0)))
    out = jnp.where(jnp.logical_and(lane >= ones, lane < ones + 3), 1.0, out)
    return out.astype(BF16)


def _qk(q, k):
    return lax.dot_general(q, k, (((1,), (1,)), ((), ())), preferred_element_type=F32)


def _online_softmax_step(s, v, carry):
    m, l, acc = carry
    m_new = jnp.maximum(m, jnp.max(s, axis=-1, keepdims=True))
    alpha = jnp.exp2(m - m_new)
    p = jnp.exp2(s - m_new)
    l = alpha * l + jnp.sum(p, axis=-1, keepdims=True)
    acc = alpha * acc + jnp.dot(p.astype(v.dtype), v, preferred_element_type=F32)
    return m_new, l, acc


def _softmax_init(tq, ev):
    return (jnp.full((tq, 1), -jnp.inf, F32), jnp.zeros((tq, 1), F32), jnp.zeros((tq, ev), F32))


def _causal_mask(s):
    r = lax.broadcasted_iota(jnp.int32, s.shape, 0)
    c = lax.broadcasted_iota(jnp.int32, s.shape, 1)
    return jnp.where(c <= r, s, -jnp.inf)


def _fox_attn_kernel(q_ref, k_ref, v_ref, cum_ref, o_ref, kaug_ref, *, tq, hd):
    hg = pl.program_id(1)
    qi = pl.program_id(2)
    nh = k_ref.shape[-1] // hd
    nq = k_ref.shape[0] // tq

    def gate(rows, a):
        lane = lax.broadcasted_iota(jnp.int32, rows.shape, 1)
        return jnp.sum(jnp.where(lane == hg * nh + a, rows, 0.0), axis=-1, keepdims=True) * LOG2E

    @pl.when(qi == 0)
    def _():
        for c in range(nq):
            sl = slice(c * tq, (c + 1) * tq)
            rows = cum_ref[sl, :]
            for a in range(nh):
                kaug_ref[a, sl, :hd] = k_ref[sl, a * hd:(a + 1) * hd]
                kaug_ref[a, sl, hd:] = _bias_lanes(-gate(rows, a), True)

    q_start = pl.multiple_of(qi * tq, tq)
    q_rows = cum_ref[pl.ds(q_start, tq), :]
    q = q_ref[...]
    qs = [jnp.concatenate([q[:, a * hd:(a + 1) * hd], _bias_lanes(gate(q_rows, a), False)], axis=1)
          for a in range(nh)]

    def tile(j, a):
        start = pl.multiple_of(j * tq, tq)
        return _qk(qs[a], kaug_ref[a, pl.ds(start, tq), :]), v_ref[pl.ds(start, tq), a * hd:(a + 1) * hd]

    def body(j, carry):
        return tuple(_online_softmax_step(*tile(j, a), carry[a]) for a in range(nh))

    carry = lax.fori_loop(0, qi, body, (_softmax_init(tq, hd),) * nh)
    for a in range(nh):
        s, v = tile(qi, a)
        _, l, acc = _online_softmax_step(_causal_mask(s), v, carry[a])
        o_ref[:, a * hd:(a + 1) * hd] = (acc / l).astype(o_ref.dtype)


def fox_attention(qkv, cum, batch, seq, heads, hd):
    assert hd == LANES
    m = qkv.shape[0]
    tq = _tile(seq, ATTN_TQ)
    nq = seq // tq
    hp = cum.shape[1]
    nh = FOX_HEADS_PER_STEP
    hg = heads // nh
    w = nh * hd
    return pl.pallas_call(
        functools.partial(_fox_attn_kernel, tq=tq, hd=hd),
        out_shape=jax.ShapeDtypeStruct((m, heads * hd), BF16),
        grid=(batch, hg, nq),
        in_specs=[pl.BlockSpec((tq, w), lambda b, h, i: (b * nq + i, h)),
                  pl.BlockSpec((seq, w), lambda b, h, i: (b, hg + h)),
                  pl.BlockSpec((seq, w), lambda b, h, i: (b, 2 * hg + h)),
                  pl.BlockSpec((seq, hp), lambda b, h, i: (b, 0))],
        out_specs=pl.BlockSpec((tq, w), lambda b, h, i: (b * nq + i, h)),
        scratch_shapes=[pltpu.VMEM((nh, seq, hd + LANES), BF16)],
        compiler_params=_params("parallel", "parallel", "arbitrary"),
        name="fox_attention",
    )(qkv, qkv, qkv, cum)


def _diff_attn_kernel(slope_ref, q_ref, k_ref, v_ref, lam_ref, g_ref, o_ref, kaug_ref, *,
                      tq, hd, lambda_init, eps):
    h = pl.program_id(1)
    qi = pl.program_id(2)
    slope2 = slope_ref[h] * LOG2E
    nq = k_ref.shape[0] // tq

    def alibi(start):
        return slope2 * (lax.broadcasted_iota(jnp.int32, (tq, 1), 0) + start).astype(F32)

    @pl.when(qi == 0)
    def _():
        for c in range(nq):
            sl = slice(c * tq, (c + 1) * tq)
            kx = _bias_lanes(alibi(c * tq), True)
            for a in range(2):
                kaug_ref[a, sl, :hd] = k_ref[sl, a * hd:(a + 1) * hd]
                kaug_ref[a, sl, hd:] = kx

    qx = _bias_lanes(-alibi(qi * tq), False)
    q = q_ref[...]
    qs = [jnp.concatenate([q[:, a * hd:(a + 1) * hd], qx], axis=1) for a in range(2)]

    ev = v_ref.shape[-1]

    def tile(j):
        start = pl.multiple_of(j * tq, tq)
        return [_qk(qs[a], kaug_ref[a, pl.ds(start, tq), :]) for a in range(2)], v_ref[pl.ds(start, tq), :]

    def body(j, carry):
        ss, v = tile(j)
        return tuple(_online_softmax_step(ss[a], v, carry[a]) for a in range(2))

    one = _softmax_init(tq, ev)
    carry = lax.fori_loop(0, qi, body, (one, one))
    ss, v = tile(qi)
    outs = []
    for a in range(2):
        _, l, acc = _online_softmax_step(_causal_mask(ss[a]), v, carry[a])
        outs.append(acc / l)
    lam = lam_ref[...]
    lam_full = (jnp.exp(jnp.sum(lam[0:1] * lam[1:2], axis=-1, keepdims=True))
                - jnp.exp(jnp.sum(lam[2:3] * lam[3:4], axis=-1, keepdims=True)) + lambda_init)
    out = outs[0] - lam_full * outs[1]
    ms = jnp.mean(out * out, axis=-1, keepdims=True)
    y = (out * lax.rsqrt(ms + eps)) * g_ref[...]
    o_ref[...] = (y * (1.0 - lambda_init)).astype(o_ref.dtype)


def diff_attention(qkv, lam, subln_g, batch, seq, heads, hd, lambda_init):
    assert hd == LANES
    m = qkv.shape[0]
    tq = _tile(seq, ATTN_TQ)
    nq = seq // tq
    ev = 2 * hd
    slopes = jnp.exp2(-8.0 * jnp.arange(1, heads + 1, dtype=F32) / heads)
    grid_spec = pltpu.PrefetchScalarGridSpec(
        num_scalar_prefetch=1,
        grid=(batch, heads, nq),
        in_specs=[pl.BlockSpec((tq, ev), lambda b, h, i, s: (b * nq + i, h)),
                  pl.BlockSpec((seq, ev), lambda b, h, i, s: (b, heads + h)),
                  pl.BlockSpec((seq, ev), lambda b, h, i, s: (b, 2 * heads + h)),
                  pl.BlockSpec((4, hd), lambda b, h, i, s: (0, 0)),
                  pl.BlockSpec((1, ev), lambda b, h, i, s: (0, 0))],
        out_specs=pl.BlockSpec((tq, ev), lambda b, h, i, s: (b * nq + i, h)),
        scratch_shapes=[pltpu.VMEM((2, seq, hd + LANES), BF16)],
    )
    return pl.pallas_call(
        functools.partial(_diff_attn_kernel, tq=tq, hd=hd, lambda_init=lambda_init, eps=DIFF_SUBLN_EPS),
        out_shape=jax.ShapeDtypeStruct((m, heads * ev), BF16),
        grid_spec=grid_spec,
        compiler_params=_params("parallel", "parallel", "arbitrary"),
        name="diff_attention",
    )(slopes, qkv, qkv, qkv, lam.astype(F32), subln_g.reshape(1, ev).astype(F32))


def _ssm_conv_kernel(x_ref, halo_ref, w_ref, b_ref, o_ref):
    i = pl.program_id(1)
    halo = jnp.where(i == 0, 0.0, halo_ref[...].astype(F32))
    u = _causal_conv_rows(x_ref[...].astype(F32), halo, w_ref[...], b_ref[...])
    o_ref[...] = _silu(u).astype(o_ref.dtype)


def ssm_conv_silu(zxbc, col_start, w, b, batch, seq):
    m = zxbc.shape[0]
    cdim = w.shape[1]
    ts, tc = _tile(seq, CONV_TS), _tile(cdim, CONV_TC)
    assert col_start % tc == 0
    cb0 = col_start // tc
    ns = seq // ts
    hb = ts // BF16_SUBLANES
    return pl.pallas_call(
        _ssm_conv_kernel,
        out_shape=jax.ShapeDtypeStruct((m, cdim), BF16),
        grid=(batch, ns, cdim // tc),
        in_specs=[pl.BlockSpec((ts, tc), lambda bi, i, j: (bi * ns + i, cb0 + j)),
                  pl.BlockSpec((BF16_SUBLANES, tc),
                               lambda bi, i, j: (jnp.maximum((bi * ns + i) * hb - 1, 0), cb0 + j)),
                  pl.BlockSpec((w.shape[0], tc), lambda bi, i, j: (0, j)),
                  pl.BlockSpec((1, tc), lambda bi, i, j: (0, j))],
        out_specs=pl.BlockSpec((ts, tc), lambda bi, i, j: (bi * ns + i, j)),
        compiler_params=_params("parallel", "parallel", "parallel"),
        name="ssm_conv_silu",
    )(zxbc, zxbc, w.astype(F32), b.reshape(1, cdim).astype(F32))


def _ssm_dt_kernel(dt_ref, bias_ref, alog_ref, dt_out, la_out):
    x = dt_ref[...] + bias_ref[...]
    dt = jnp.maximum(x, 0.0) + jnp.log(1.0 + jnp.exp(-jnp.abs(x)))
    dt_out[...] = dt
    la_out[...] = _cumsum_rows(dt * (-jnp.exp(alog_ref[...])))


def ssm_dt(dt_raw, dt_bias, a_log):
    m, h = dt_raw.shape
    spec = pl.BlockSpec((SSM_CHUNK, h), lambda i: (i, 0))
    pspec = pl.BlockSpec((1, h), lambda i: (0, 0))
    return pl.pallas_call(
        _ssm_dt_kernel,
        out_shape=(jax.ShapeDtypeStruct((m, h), F32), jax.ShapeDtypeStruct((m, h), F32)),
        grid=(m // SSM_CHUNK,),
        in_specs=[spec, pspec, pspec],
        out_specs=(spec, spec),
        compiler_params=_params("parallel"),
        name="ssm_dt",
    )(dt_raw, dt_bias.reshape(1, h).astype(F32), a_log.reshape(1, h).astype(F32))


def _ssd_scan_kernel(x_ref, b_ref, c_ref, dt_ref, la_ref, lat_ref, z_ref, d_ref, g_ref, o_ref,
                     state_ref, *, hpg, pdim, eps):
    ci = pl.program_id(2)

    @pl.when(ci == 0)
    def _():
        state_ref[...] = jnp.zeros_like(state_ref)

    chunk = x_ref.shape[0]
    npair = hpg // 2
    x = x_ref[...].astype(F32)
    bm = b_ref[...]
    cm = c_ref[...]
    dt = dt_ref[...]
    la = la_ref[...]
    lat = lat_ref[...]
    lane = lax.broadcasted_iota(jnp.int32, (chunk, 2 * pdim), 1)
    first = lane < pdim

    def expand(a):
        lo = lax.broadcasted_iota(jnp.int32, (a.shape[0], 2 * pdim), 1) < pdim
        return jnp.concatenate(
            [jnp.where(lo, a[:, 2 * p:2 * p + 1], a[:, 2 * p + 1:2 * p + 2]) for p in range(npair)], axis=1)

    dt_e = expand(dt)
    la_e = expand(la)
    la_end_e = expand(la[chunk - 1:chunk])
    cb = _qk(cm, bm)
    tri = (lax.broadcasted_iota(jnp.int32, (chunk, chunk), 0)
           >= lax.broadcasted_iota(jnp.int32, (chunk, chunk), 1))
    xdt = x * dt_e
    y_parts = []
    for p in range(npair):
        ms = []
        for hh in (2 * p, 2 * p + 1):
            seg = la[:, hh:hh + 1] - lat[hh:hh + 1, :]
            ms.append((cb * jnp.exp(jnp.where(tri, seg, -jnp.inf))).astype(BF16))
        xp = xdt[:, p * 2 * pdim:(p + 1) * 2 * pdim]
        rhs = jnp.concatenate([jnp.where(first, xp, 0.0), jnp.where(first, 0.0, xp)], axis=0).astype(BF16)
        y_parts.append(jnp.dot(jnp.concatenate(ms, axis=1), rhs, preferred_element_type=F32))
    y = jnp.concatenate(y_parts, axis=1)
    state = state_ref[...]
    y = y + jnp.dot(cm, state.astype(BF16), preferred_element_type=F32) * jnp.exp(la_e)
    to_end = jnp.exp(la_end_e - la_e) * dt_e
    upd = lax.dot_general(bm, (x * to_end).astype(BF16), (((0,), (0,)), ((), ())),
                          preferred_element_type=F32)
    state_ref[...] = state * jnp.exp(la_end_e) + upd
    y = y + d_ref[...] * x
    y = y * _silu(z_ref[...].astype(F32))
    y = y * lax.rsqrt(jnp.mean(y * y, axis=-1, keepdims=True) + eps)
    o_ref[...] = (y * g_ref[...]).astype(o_ref.dtype)


def ssd_scan(xbc, z, dt_g, la_g, lat_g, d_e, norm_g, batch, seq, d_inner, heads):
    m = xbc.shape[0]
    groups, nstate, chunk = SSM_GROUPS, SSM_STATE, SSM_CHUNK
    hpg = heads // groups
    pdim = d_inner // heads
    gw = hpg * pdim
    nc = seq // chunk
    xb = d_inner // nstate
    row = lambda b, g, c: b * nc + c
    return pl.pallas_call(
        functools.partial(_ssd_scan_kernel, hpg=hpg, pdim=pdim, eps=SSM_NORM_EPS),
        out_shape=jax.ShapeDtypeStruct((m, d_inner), BF16),
        grid=(batch, groups, nc),
        in_specs=[pl.BlockSpec((chunk, gw), lambda b, g, c: (row(b, g, c), g)),
                  pl.BlockSpec((chunk, nstate), lambda b, g, c: (row(b, g, c), xb + g)),
                  pl.BlockSpec((chunk, nstate), lambda b, g, c: (row(b, g, c), xb + groups + g)),
                  pl.BlockSpec((None, chunk, hpg), lambda b, g, c: (g, row(b, g, c), 0)),
                  pl.BlockSpec((None, chunk, hpg), lambda b, g, c: (g, row(b, g, c), 0)),
                  pl.BlockSpec((None, hpg, chunk), lambda b, g, c: (g, 0, row(b, g, c))),
                  pl.BlockSpec((chunk, gw), lambda b, g, c: (row(b, g, c), g)),
                  pl.BlockSpec((1, gw), lambda b, g, c: (0, g)),
                  pl.BlockSpec((1, gw), lambda b, g, c: (0, g))],
        out_specs=pl.BlockSpec((chunk, gw), lambda b, g, c: (row(b, g, c), g)),
        scratch_shapes=[pltpu.VMEM((nstate, gw), F32)],
        compiler_params=_params("parallel", "parallel", "arbitrary"),
        name="ssd_scan",
    )(xbc, xbc, xbc, dt_g, la_g, lat_g, z, d_e, norm_g)


def _q_colscale(ncols, nq, hd):
    return jnp.where(jnp.arange(ncols) < nq, (hd ** -0.5) * LOG2E, 1.0).astype(F32)


def fox_mixer(h, hn, layer, w_in, b_f, w_o, batch, seq):
    d = hn.shape[1]
    heads = b_f.shape[0]
    hd = d // heads
    hp = _round_up(heads, LANES)
    qkv = matmul(hn, cast_pad(w_in, layer, 0, 3 * d, colscale=_q_colscale(3 * d, d, hd)), out_dtype=BF16)
    f_logit = matmul(hn, cast_pad(w_in, layer, 3 * d, heads, out_cols=hp, tc=LANES))
    b_pad = jnp.pad(b_f.reshape(1, heads).astype(F32), ((0, 0), (0, hp - heads)))
    cum = fox_gate_cumsum(f_logit, b_pad, seq)
    attn = fox_attention(qkv, cum, batch, seq, heads, hd)
    return matmul(attn, cast_pad(w_o, layer), res=h)


def ssd_mixer(h, hn, layer, w_in, conv_w, conv_b, dt_bias, a_log, d_skip, norm_g, w_out, batch, seq):
    heads = a_log.shape[0]
    d_inner = w_out.shape[1]
    conv_dim = conv_w.shape[1]
    groups = SSM_GROUPS
    hpg = heads // groups
    m = hn.shape[0]
    zxbc = matmul(hn, cast_pad(w_in, layer, 0, d_inner + conv_dim), out_dtype=BF16)
    dt_raw = matmul(hn, cast_pad(w_in, layer, d_inner + conv_dim, heads, tc=LANES))
    xbc = ssm_conv_silu(zxbc, d_inner, conv_w, conv_b, batch, seq)
    dt, la = ssm_dt(dt_raw, dt_bias, a_log)
    dt_g = jnp.transpose(dt.reshape(m, groups, hpg), (1, 0, 2))
    la_g = jnp.transpose(la.reshape(m, groups, hpg), (1, 0, 2))
    lat_g = jnp.transpose(la_g, (0, 2, 1))
    pdim = d_inner // heads
    d_e = jnp.repeat(d_skip.astype(F32), pdim).reshape(1, d_inner)
    y = ssd_scan(xbc, zxbc, dt_g, la_g, lat_g, d_e, norm_g.reshape(1, d_inner).astype(F32),
                 batch, seq, d_inner, heads)
    return matmul(y, cast_pad(w_out, layer), res=h)


def diff_mixer(h, hn, layer, w_in, lam, subln_g, w_o, lambda_init, batch, seq):
    d = hn.shape[1]
    hd = lam.shape[1]
    heads = d // (2 * hd)
    qkv = matmul(hn, cast_pad(w_in, layer, colscale=_q_colscale(3 * d, d, hd)), out_dtype=BF16)
    attn = diff_attention(qkv, lam, subln_g, batch, seq, heads, hd, lambda_init)
    return matmul(attn, cast_pad(w_o, layer), res=h)


def conv_ffn(h, hn, layer, w_up, conv_w, conv_b, w_down, seq):
    f = w_down.shape[1]
    fp = _round_up(f, FFN_TN)
    tc = math.gcd(f, CAST_TC)
    wg = cast_pad(w_up, layer, 0, f, out_cols=fp, tc=tc)
    wv = cast_pad(w_up, layer, f, f, out_cols=fp, tc=tc)
    wd = cast_pad(w_down, layer, out_rows=fp)
    cw = conv_w.astype(F32)
    cb = conv_b.reshape(1, 2 * f).astype(F32)
    pad = lambda a: jnp.pad(a, ((0, 0), (0, fp - f)))
    act = ffn_up(hn, wg, wv, pad(cw[:, :f]), pad(cw[:, f:]), pad(cb[:, :f]), pad(cb[:, f:]), seq)
    return matmul(act, wd, res=h)


def kernel(x, mix_norm_g, ffn_norm_g, fox_w_in, fox_b_f, fox_w_o, ssm_w_in, ssm_conv_w, ssm_conv_b,
           ssm_dt_bias, ssm_a_log, ssm_d, ssm_norm_g, ssm_w_out, diff_w_in, diff_lambda, diff_subln_g,
           diff_w_o, ffn_w_up, ffn_conv_w, ffn_conv_b, ffn_w_down, final_norm_g):
    batch, seq, d = x.shape
    depth = mix_norm_g.shape[0]
    h = x.reshape(batch * seq, d)
    for i in range(depth):
        kind, j = i % N_MIXERS, i // N_MIXERS
        hn = rmsnorm(h, mix_norm_g[i], NORM_EPS, BF16)
        if kind == 0:
            h = fox_mixer(h, hn, j, fox_w_in, fox_b_f[j], fox_w_o, batch, seq)
        elif kind == 1:
            h = ssd_mixer(h, hn, j, ssm_w_in, ssm_conv_w[j], ssm_conv_b[j], ssm_dt_bias[j], ssm_a_log[j],
                          ssm_d[j], ssm_norm_g[j], ssm_w_out, batch, seq)
        else:
            lambda_init = 0.8 - 0.6 * math.exp(-0.3 * i)
            h = diff_mixer(h, hn, j, diff_w_in, diff_lambda[j], diff_subln_g[j], diff_w_o,
                           lambda_init, batch, seq)
        hn = rmsnorm(h, ffn_norm_g[i], NORM_EPS, BF16)
        h = conv_ffn(h, hn, i, ffn_w_up, ffn_conv_w[i], ffn_conv_b[i], ffn_w_down, seq)
    return rmsnorm(h, final_norm_g, NORM_EPS, F32).reshape(batch, seq, d)
```

```python
import functools
import math

import jax
import jax.numpy as jnp
from jax import lax
from jax.experimental import pallas as pl
from jax.experimental.pallas import tpu as pltpu

F32 = jnp.float32
BF16 = jnp.bfloat16
LOG2E = 1.4426950408889634

V7X_VMEM_LIMIT_BYTES = 56 * 1024 * 1024
LANES = 128
BF16_SUBLANES = 16

NORM_EPS = 1e-6
SSM_NORM_EPS = 1e-5
DIFF_SUBLN_EPS = 1e-5
SSM_GROUPS = 8
SSM_STATE = 128
SSM_CHUNK = 128
N_MIXERS = 3

MM_TM = 1024
MM_TN = 1024
MM_TK_MAX = 4096
MM_TK_SPLIT = 3072
FFN_TM = 1024
FFN_TN = 512
ATTN_TQ = 1024
FOX_HEADS_PER_STEP = 2
NORM_TM = 256
CONV_TS = 512
CONV_TC = 1024
CAST_TR = 1024
CAST_TC = 1024


def _tile(dim, pref):
    if dim <= pref:
        return dim
    t = pref
    while dim % t:
        t //= 2
    return t


def _round_up(x, m):
    return -(-x // m) * m


def _params(*sem):
    return pltpu.CompilerParams(dimension_semantics=sem, vmem_limit_bytes=V7X_VMEM_LIMIT_BYTES)


def _cast_kernel(w_ref, *refs, rows, cols, padded, has_colscale, has_rowscale):
    o_ref = refs[-1]
    tr, tc = o_ref.shape
    w = w_ref[...]
    if has_colscale:
        w = w * refs[0][...]
    if has_rowscale:
        w = w * refs[int(has_colscale)][...]
    if padded:
        r = lax.broadcasted_iota(jnp.int32, (tr, tc), 0) + pl.program_id(0) * tr
        c = lax.broadcasted_iota(jnp.int32, (tr, tc), 1) + pl.program_id(1) * tc
        w = jnp.where(jnp.logical_and(r < rows, c < cols), w, 0.0)
    o_ref[...] = w.astype(o_ref.dtype)


def cast_pad(w, layer, col_start=0, ncols=None, out_rows=None, out_cols=None, colscale=None,
             rowscale=None, tc=CAST_TC):
    _, rows, wcols = w.shape
    ncols = wcols - col_start if ncols is None else ncols
    out_rows = rows if out_rows is None else out_rows
    out_cols = ncols if out_cols is None else out_cols
    tr = _tile(out_rows, CAST_TR)
    tc = _tile(out_cols, tc)
    assert col_start % tc == 0
    cb0 = col_start // tc
    last_rb = (rows - 1) // tr
    last_cb = (wcols - 1) // tc
    in_specs = [pl.BlockSpec((None, tr, tc),
                             lambda i, j: (layer, jnp.minimum(i, last_rb), jnp.minimum(j + cb0, last_cb)))]
    args = [w]
    if colscale is not None:
        in_specs.append(pl.BlockSpec((1, tc), lambda i, j: (0, j)))
        args.append(colscale.reshape(1, out_cols).astype(F32))
    if rowscale is not None:
        in_specs.append(pl.BlockSpec((tr, 1), lambda i, j: (jnp.minimum(i, last_rb), 0)))
        args.append(rowscale.reshape(rows, 1).astype(F32))
    padded = out_rows != rows or out_cols != ncols
    return pl.pallas_call(
        functools.partial(_cast_kernel, rows=rows, cols=ncols, padded=padded,
                          has_colscale=colscale is not None, has_rowscale=rowscale is not None),
        out_shape=jax.ShapeDtypeStruct((out_rows, out_cols), BF16),
        grid=(out_rows // tr, out_cols // tc),
        in_specs=in_specs,
        out_specs=pl.BlockSpec((tr, tc), lambda i, j: (i, j)),
        compiler_params=_params("parallel", "parallel"),
        name="cast_pad",
    )(*args)


def _rmsnorm_kernel(x_ref, g_ref, o_ref, *, eps):
    x = x_ref[...]
    ms = jnp.mean(x * x, axis=-1, keepdims=True)
    o_ref[...] = ((x * lax.rsqrt(ms + eps)) * g_ref[...]).astype(o_ref.dtype)


def rmsnorm(x, g, eps, out_dtype):
    m, d = x.shape
    tm = _tile(m, NORM_TM)
    return pl.pallas_call(
        functools.partial(_rmsnorm_kernel, eps=eps),
        out_shape=jax.ShapeDtypeStruct((m, d), out_dtype),
        grid=(m // tm,),
        in_specs=[pl.BlockSpec((tm, d), lambda i: (i, 0)),
                  pl.BlockSpec((1, d), lambda i: (0, 0))],
        out_specs=pl.BlockSpec((tm, d), lambda i: (i, 0)),
        compiler_params=_params("parallel"),
        name="rmsnorm",
    )(x, g.reshape(1, d).astype(F32))


def _row_sumsq(h):
    return jnp.broadcast_to(jnp.sum(h * h, axis=-1, keepdims=True), (h.shape[0], LANES))


def _norm_scale(ss, dim, eps):
    tot = ss[:, :LANES]
    for c in range(LANES, ss.shape[1], LANES):
        tot = tot + ss[:, c:c + LANES]
    return lax.rsqrt(tot * (1.0 / dim) + eps)


def _scale_rows(acc, r):
    if acc.shape[1] < LANES:
        return acc * r[:, :acc.shape[1]]
    return jnp.concatenate([acc[:, c:c + LANES] * r for c in range(0, acc.shape[1], LANES)], axis=1)


def _norm_prep_kernel(x_ref, xb_ref, ss_ref):
    x = x_ref[...]
    xb_ref[...] = x.astype(xb_ref.dtype)
    ss_ref[...] = _row_sumsq(x)


def norm_prep(x):
    m, d = x.shape
    tm = _tile(m, NORM_TM)
    return pl.pallas_call(
        _norm_prep_kernel,
        out_shape=(jax.ShapeDtypeStruct((m, d), BF16), jax.ShapeDtypeStruct((m, LANES), F32)),
        grid=(m // tm,),
        in_specs=[pl.BlockSpec((tm, d), lambda i: (i, 0))],
        out_specs=(pl.BlockSpec((tm, d), lambda i: (i, 0)), pl.BlockSpec((tm, LANES), lambda i: (i, 0))),
        compiler_params=_params("parallel"),
        name="norm_prep",
    )(x)


def _mm_kernel(*refs, nk, has_res, has_stat, emit_stat, eps):
    refs = list(refs)
    x_ref, w_ref = refs[:2]
    pos = 2
    ss_in = r_ref = None
    if has_stat:
        ss_in = refs[pos]
        pos += 1
    if has_res:
        r_ref = refs[pos]
        pos += 1
    o_ref = refs[pos]
    part = jnp.dot(x_ref[...], w_ref[...], preferred_element_type=F32)

    def finish(acc):
        if has_stat:
            acc = _scale_rows(acc, _norm_scale(ss_in[...], x_ref.shape[1] * nk, eps))
        if has_res:
            acc = acc + r_ref[...]
        o_ref[...] = acc.astype(o_ref.dtype)
        if emit_stat:
            refs[pos + 1][...] = acc.astype(BF16)
            refs[pos + 2][...] = _row_sumsq(acc)

    if nk == 1:
        finish(part)
        return
    acc_ref = refs[-1]
    k = pl.program_id(2)

    @pl.when(k == 0)
    def _():
        acc_ref[...] = part

    @pl.when(jnp.logical_and(k > 0, k < nk - 1))
    def _():
        acc_ref[...] += part

    @pl.when(k == nk - 1)
    def _():
        finish(acc_ref[...] + part)


def matmul(x, w, res=None, out_dtype=F32, rowstat=None, emit_stat=False):
    m, kdim = x.shape
    n = w.shape[1]
    tm, tn = _tile(m, MM_TM), _tile(n, MM_TN)
    tk = kdim
    if kdim > (MM_TK_SPLIT if emit_stat else MM_TK_MAX):
        tk = MM_TK_SPLIT
        while kdim % tk:
            tk -= 2 * LANES
    nk = kdim // tk
    in_specs = [pl.BlockSpec((tm, tk), lambda i, j, k: (i, k)),
                pl.BlockSpec((tk, tn), lambda i, j, k: (k, j))]
    args = [x, w]
    if rowstat is not None:
        in_specs.append(pl.BlockSpec((tm, rowstat.shape[1]), lambda i, j, k: (i, 0)))
        args.append(rowstat)
    if res is not None:
        in_specs.append(pl.BlockSpec((tm, tn), lambda i, j, k: (i, j)))
        args.append(res)
    out_block = pl.BlockSpec((tm, tn), lambda i, j, k: (i, j))
    out_shape = jax.ShapeDtypeStruct((m, n), out_dtype)
    out_specs = out_block
    if emit_stat:
        out_shape = (out_shape, jax.ShapeDtypeStruct((m, n), BF16),
                     jax.ShapeDtypeStruct((m, (n // tn) * LANES), F32))
        out_specs = (out_block, out_block, pl.BlockSpec((tm, LANES), lambda i, j, k: (i, j)))
    return pl.pallas_call(
        functools.partial(_mm_kernel, nk=nk, has_res=res is not None, has_stat=rowstat is not None,
                          emit_stat=emit_stat, eps=NORM_EPS),
        out_shape=out_shape,
        grid=(m // tm, n // tn, nk),
        in_specs=in_specs,
        out_specs=out_specs,
        scratch_shapes=[pltpu.VMEM((tm, tn), F32)] if nk > 1 else [],
        compiler_params=_params("parallel", "parallel", "arbitrary"),
        name="matmul",
    )(*args)


def _causal_conv_rows(p, halo, w, b):
    kw = w.shape[0]
    hr = halo.shape[0]
    top = p[0:hr]
    u = b + w[kw - 1:kw] * p
    ut = b + w[kw - 1:kw] * top
    row = lax.broadcasted_iota(jnp.int32, top.shape, 0)
    for k in range(1, kw):
        wk = w[kw - 1 - k:kw - k]
        u = u + wk * pltpu.roll(p, k, 0)
        shifted_top = jnp.where(row < k, pltpu.roll(halo, k, 0), pltpu.roll(top, k, 0))
        ut = ut + wk * shifted_top
    return jnp.concatenate([ut, u[hr:]], axis=0)


def _silu(x):
    return x / (1.0 + jnp.exp(-x))


def _ffn_up_kernel(x_ref, ss_ref, wgl_ref, wgh_ref, wvl_ref, wvh_ref, cwg_ref, cwv_ref, cbg_ref, cbv_ref,
                   o_ref, halo_g, halo_v, *, tiles_per_seq, eps):
    i = pl.program_id(1)

    @pl.when(i % tiles_per_seq == 0)
    def _():
        halo_g[...] = jnp.zeros_like(halo_g)
        halo_v[...] = jnp.zeros_like(halo_v)

    x = x_ref[...]
    r = _norm_scale(ss_ref[...], x.shape[1], eps)
    wg = jnp.concatenate([wgl_ref[...], wgh_ref[...]], axis=1)
    wv = jnp.concatenate([wvl_ref[...], wvh_ref[...]], axis=1)
    pg = _scale_rows(jnp.dot(x, wg, preferred_element_type=F32), r)
    pv = _scale_rows(jnp.dot(x, wv, preferred_element_type=F32), r)
    ug = _causal_conv_rows(pg, halo_g[...], cwg_ref[...], cbg_ref[...])
    uv = _causal_conv_rows(pv, halo_v[...], cwv_ref[...], cbv_ref[...])
    hr = halo_g.shape[0]
    halo_g[...] = pg[pg.shape[0] - hr:]
    halo_v[...] = pv[pv.shape[0] - hr:]
    o_ref[...] = (_silu(ug) * uv).astype(o_ref.dtype)


def ffn_up(x, rowstat, w_up, cwg, cwv, cbg, cbv, seq):
    m, d = x.shape
    f = w_up.shape[1] // 2
    fp = cwg.shape[1]
    tm, tn = _tile(seq, FFN_TM), _tile(fp, FFN_TN)
    half = tn // 2
    assert f % half == 0
    nb, last = f // half, 2 * f // half - 1
    kw = cwg.shape[0]
    wspec = lambda off: pl.BlockSpec((d, half), lambda j, i: (0, jnp.minimum(2 * j + off, last)))
    cspec = pl.BlockSpec((kw, tn), lambda j, i: (0, j))
    bspec = pl.BlockSpec((1, tn), lambda j, i: (0, j))
    return pl.pallas_call(
        functools.partial(_ffn_up_kernel, tiles_per_seq=seq // tm, eps=NORM_EPS),
        out_shape=jax.ShapeDtypeStruct((m, fp), BF16),
        grid=(fp // tn, m // tm),
        in_specs=[pl.BlockSpec((tm, d), lambda j, i: (i, 0)),
                  pl.BlockSpec((tm, rowstat.shape[1]), lambda j, i: (i, 0)),
                  wspec(0), wspec(1), wspec(nb), wspec(nb + 1), cspec, cspec, bspec, bspec],
        out_specs=pl.BlockSpec((tm, tn), lambda j, i: (i, j)),
        scratch_shapes=[pltpu.VMEM((BF16_SUBLANES, tn), F32), pltpu.VMEM((BF16_SUBLANES, tn), F32)],
        compiler_params=_params("arbitrary", "arbitrary"),
        name="ffn_up",
    )(x, rowstat, w_up, w_up, w_up, w_up, cwg, cwv, cbg, cbv)


def _cumsum_rows(y):
    n = y.shape[0]
    row = lax.broadcasted_iota(jnp.int32, y.shape, 0)
    shift = 1
    while shift < n:
        y = y + jnp.where(row >= shift, pltpu.roll(y, shift, 0), 0.0)
        shift *= 2
    return y


def _fox_gate_kernel(f_ref, b_ref, o_ref):
    x = f_ref[...] + b_ref[...]
    log_f = jnp.minimum(x, 0.0) - jnp.log(1.0 + jnp.exp(-jnp.abs(x)))
    o_ref[...] = _cumsum_rows(log_f)


def fox_gate_cumsum(f_logit, b_f, seq):
    m, hp = f_logit.shape
    return pl.pallas_call(
        _fox_gate_kernel,
        out_shape=jax.ShapeDtypeStruct((m, hp), F32),
        grid=(m // seq,),
        in_specs=[pl.BlockSpec((seq, hp), lambda b: (b, 0)), pl.BlockSpec((1, hp), lambda b: (0, 0))],
        out_specs=pl.BlockSpec((seq, hp), lambda b: (b, 0)),
        compiler_params=_params("parallel"),
        name="fox_gate_cumsum",
    )(f_logit, b_f)


def _bias_lanes(x, pieces_first):
    hi = x.astype(BF16).astype(F32)
    rem = x - hi
    lo = rem.astype(BF16).astype(F32)
    lo2 = rem - lo
    lane = lax.broadcasted_iota(jnp.int32, (x.shape[0], LANES), 1)
    base = 0 if pieces_first else 3
    ones = 3 if pieces_first else 0
    out = jnp.where(lane == base, hi, jnp.where(lane == base + 1, lo, jnp.where(lane == base + 2, lo2, 0.0)))
    out = jnp.where(jnp.logical_and(lane >= ones, lane < ones + 3), 1.0, out)
    return out.astype(BF16)


def _qk(q, k):
    return lax.dot_general(q, k, (((1,), (1,)), ((), ())), preferred_element_type=F32)


def _online_softmax_step(s, v, carry):
    m, l, acc = carry
    m_new = jnp.maximum(m, jnp.max(s, axis=-1, keepdims=True))
    alpha = jnp.exp2(m - m_new)
    p = jnp.exp2(s - m_new)
    l = alpha * l + jnp.sum(p, axis=-1, keepdims=True)
    acc = alpha * acc + jnp.dot(p.astype(v.dtype), v, preferred_element_type=F32)
    return m_new, l, acc


def _softmax_init(tq, ev):
    return (jnp.full((tq, 1), -jnp.inf, F32), jnp.zeros((tq, 1), F32), jnp.zeros((tq, ev), F32))


def _causal_mask(s):
    r = lax.broadcasted_iota(jnp.int32, s.shape, 0)
    c = lax.broadcasted_iota(jnp.int32, s.shape, 1)
    return jnp.where(c <= r, s, -jnp.inf)


def _fox_attn_kernel(q_ref, k_ref, v_ref, cum_ref, o_ref, kaug_ref, *, tq, hd):
    hg = pl.program_id(1)
    qi = pl.program_id(2)
    nh = k_ref.shape[-1] // hd
    nq = k_ref.shape[0] // tq

    def gate(rows, a):
        lane = lax.broadcasted_iota(jnp.int32, rows.shape, 1)
        return jnp.sum(jnp.where(lane == hg * nh + a, rows, 0.0), axis=-1, keepdims=True) * LOG2E

    @pl.when(qi == 0)
    def _():
        for c in range(nq):
            sl = slice(c * tq, (c + 1) * tq)
            rows = cum_ref[sl, :]
            for a in range(nh):
                kaug_ref[a, sl, :hd] = k_ref[sl, a * hd:(a + 1) * hd]
                kaug_ref[a, sl, hd:] = _bias_lanes(-gate(rows, a), True)

    q_start = pl.multiple_of(qi * tq, tq)
    q_rows = cum_ref[pl.ds(q_start, tq), :]
    q = q_ref[...]
    qs = [jnp.concatenate([q[:, a * hd:(a + 1) * hd], _bias_lanes(gate(q_rows, a), False)], axis=1)
          for a in range(nh)]

    def tile(j, a):
        start = pl.multiple_of(j * tq, tq)
        return _qk(qs[a], kaug_ref[a, pl.ds(start, tq), :]), v_ref[pl.ds(start, tq), a * hd:(a + 1) * hd]

    def body(j, carry):
        return tuple(_online_softmax_step(*tile(j, a), carry[a]) for a in range(nh))

    carry = lax.fori_loop(0, qi, body, (_softmax_init(tq, hd),) * nh)
    for a in range(nh):
        s, v = tile(qi, a)
        _, l, acc = _online_softmax_step(_causal_mask(s), v, carry[a])
        o_ref[:, a * hd:(a + 1) * hd] = (acc / l).astype(o_ref.dtype)


def fox_attention(qkv, cum, batch, seq, heads, hd):
    assert hd == LANES
    m = qkv.shape[0]
    tq = _tile(seq, ATTN_TQ)
    nq = seq // tq
    hp = cum.shape[1]
    nh = FOX_HEADS_PER_STEP
    hg = heads // nh
    w = nh * hd
    return pl.pallas_call(
        functools.partial(_fox_attn_kernel, tq=tq, hd=hd),
        out_shape=jax.ShapeDtypeStruct((m, heads * hd), BF16),
        grid=(batch, hg, nq),
        in_specs=[pl.BlockSpec((tq, w), lambda b, h, i: (b * nq + i, h)),
                  pl.BlockSpec((seq, w), lambda b, h, i: (b, hg + h)),
                  pl.BlockSpec((seq, w), lambda b, h, i: (b, 2 * hg + h)),
                  pl.BlockSpec((seq, hp), lambda b, h, i: (b, 0))],
        out_specs=pl.BlockSpec((tq, w), lambda b, h, i: (b * nq + i, h)),
        scratch_shapes=[pltpu.VMEM((nh, seq, hd + LANES), BF16)],
        compiler_params=_params("parallel", "parallel", "arbitrary"),
        name="fox_attention",
    )(qkv, qkv, qkv, cum)


def _diff_attn_kernel(slope_ref, q_ref, k_ref, v_ref, lam_ref, g_ref, o_ref, kaug_ref, *,
                      tq, hd, lambda_init, eps):
    h = pl.program_id(1)
    qi = pl.program_id(2)
    slope2 = slope_ref[h] * LOG2E
    nq = k_ref.shape[0] // tq

    def alibi(start):
        return slope2 * (lax.broadcasted_iota(jnp.int32, (tq, 1), 0) + start).astype(F32)

    @pl.when(qi == 0)
    def _():
        for c in range(nq):
            sl = slice(c * tq, (c + 1) * tq)
            kx = _bias_lanes(alibi(c * tq), True)
            for a in range(2):
                kaug_ref[a, sl, :hd] = k_ref[sl, a * hd:(a + 1) * hd]
                kaug_ref[a, sl, hd:] = kx

    qx = _bias_lanes(-alibi(qi * tq), False)
    q = q_ref[...]
    qs = [jnp.concatenate([q[:, a * hd:(a + 1) * hd], qx], axis=1) for a in range(2)]

    ev = v_ref.shape[-1]

    def tile(j):
        start = pl.multiple_of(j * tq, tq)
        return [_qk(qs[a], kaug_ref[a, pl.ds(start, tq), :]) for a in range(2)], v_ref[pl.ds(start, tq), :]

    def body(j, carry):
        ss, v = tile(j)
        return tuple(_online_softmax_step(ss[a], v, carry[a]) for a in range(2))

    one = _softmax_init(tq, ev)
    carry = lax.fori_loop(0, qi, body, (one, one))
    ss, v = tile(qi)
    outs = []
    for a in range(2):
        _, l, acc = _online_softmax_step(_causal_mask(ss[a]), v, carry[a])
        outs.append(acc / l)
    lam = lam_ref[...]
    lam_full = (jnp.exp(jnp.sum(lam[0:1] * lam[1:2], axis=-1, keepdims=True))
                - jnp.exp(jnp.sum(lam[2:3] * lam[3:4], axis=-1, keepdims=True)) + lambda_init)
    out = outs[0] - lam_full * outs[1]
    ms = jnp.mean(out * out, axis=-1, keepdims=True)
    y = (out * lax.rsqrt(ms + eps)) * g_ref[...]
    o_ref[...] = (y * (1.0 - lambda_init)).astype(o_ref.dtype)


def diff_attention(qkv, lam, subln_g, batch, seq, heads, hd, lambda_init):
    assert hd == LANES
    m = qkv.shape[0]
    tq = _tile(seq, ATTN_TQ)
    nq = seq // tq
    ev = 2 * hd
    slopes = jnp.exp2(-8.0 * jnp.arange(1, heads + 1, dtype=F32) / heads)
    grid_spec = pltpu.PrefetchScalarGridSpec(
        num_scalar_prefetch=1,
        grid=(batch, heads, nq),
        in_specs=[pl.BlockSpec((tq, ev), lambda b, h, i, s: (b * nq + i, h)),
                  pl.BlockSpec((seq, ev), lambda b, h, i, s: (b, heads + h)),
                  pl.BlockSpec((seq, ev), lambda b, h, i, s: (b, 2 * heads + h)),
                  pl.BlockSpec((4, hd), lambda b, h, i, s: (0, 0)),
                  pl.BlockSpec((1, ev), lambda b, h, i, s: (0, 0))],
        out_specs=pl.BlockSpec((tq, ev), lambda b, h, i, s: (b * nq + i, h)),
        scratch_shapes=[pltpu.VMEM((2, seq, hd + LANES), BF16)],
    )
    return pl.pallas_call(
        functools.partial(_diff_attn_kernel, tq=tq, hd=hd, lambda_init=lambda_init, eps=DIFF_SUBLN_EPS),
        out_shape=jax.ShapeDtypeStruct((m, heads * ev), BF16),
        grid_spec=grid_spec,
        compiler_params=_params("parallel", "parallel", "arbitrary"),
        name="diff_attention",
    )(slopes, qkv, qkv, qkv, lam.astype(F32), subln_g.reshape(1, ev).astype(F32))


def _ssm_conv_kernel(x_ref, halo_ref, w_ref, b_ref, o_ref):
    i = pl.program_id(1)
    halo = jnp.where(i == 0, 0.0, halo_ref[...].astype(F32))
    u = _causal_conv_rows(x_ref[...].astype(F32), halo, w_ref[...], b_ref[...])
    o_ref[...] = _silu(u).astype(o_ref.dtype)


def ssm_conv_silu(zxbc, col_start, w, b, batch, seq):
    m = zxbc.shape[0]
    cdim = w.shape[1]
    ts, tc = _tile(seq, CONV_TS), _tile(cdim, CONV_TC)
    assert col_start % tc == 0
    cb0 = col_start // tc
    ns = seq // ts
    hb = ts // BF16_SUBLANES
    return pl.pallas_call(
        _ssm_conv_kernel,
        out_shape=jax.ShapeDtypeStruct((m, cdim), BF16),
        grid=(batch, ns, cdim // tc),
        in_specs=[pl.BlockSpec((ts, tc), lambda bi, i, j: (bi * ns + i, cb0 + j)),
                  pl.BlockSpec((BF16_SUBLANES, tc),
                               lambda bi, i, j: (jnp.maximum((bi * ns + i) * hb - 1, 0), cb0 + j)),
                  pl.BlockSpec((w.shape[0], tc), lambda bi, i, j: (0, j)),
                  pl.BlockSpec((1, tc), lambda bi, i, j: (0, j))],
        out_specs=pl.BlockSpec((ts, tc), lambda bi, i, j: (bi * ns + i, j)),
        compiler_params=_params("parallel", "parallel", "parallel"),
        name="ssm_conv_silu",
    )(zxbc, zxbc, w.astype(F32), b.reshape(1, cdim).astype(F32))


def _ssm_dt_kernel(dt_ref, bias_ref, alog_ref, dt_out, la_out):
    x = dt_ref[...] + bias_ref[...]
    dt = jnp.maximum(x, 0.0) + jnp.log(1.0 + jnp.exp(-jnp.abs(x)))
    dt_out[...] = dt
    la_out[...] = _cumsum_rows(dt * (-jnp.exp(alog_ref[...])))


def ssm_dt(dt_raw, dt_bias, a_log):
    m, h = dt_raw.shape
    spec = pl.BlockSpec((SSM_CHUNK, h), lambda i: (i, 0))
    pspec = pl.BlockSpec((1, h), lambda i: (0, 0))
    return pl.pallas_call(
        _ssm_dt_kernel,
        out_shape=(jax.ShapeDtypeStruct((m, h), F32), jax.ShapeDtypeStruct((m, h), F32)),
        grid=(m // SSM_CHUNK,),
        in_specs=[spec, pspec, pspec],
        out_specs=(spec, spec),
        compiler_params=_params("parallel"),
        name="ssm_dt",
    )(dt_raw, dt_bias.reshape(1, h).astype(F32), a_log.reshape(1, h).astype(F32))


def _ssd_scan_kernel(x_ref, b_ref, c_ref, dt_ref, la_ref, lat_ref, z_ref, d_ref, g_ref, o_ref,
                     state_ref, *, hpg, pdim, eps):
    ci = pl.program_id(2)

    @pl.when(ci == 0)
    def _():
        state_ref[...] = jnp.zeros_like(state_ref)

    chunk = x_ref.shape[0]
    npair = hpg // 2
    x = x_ref[...].astype(F32)
    bm = b_ref[...]
    cm = c_ref[...]
    dt = dt_ref[...]
    la = la_ref[...]
    lat = lat_ref[...]
    lane = lax.broadcasted_iota(jnp.int32, (chunk, 2 * pdim), 1)
    first = lane < pdim

    def expand(a):
        lo = lax.broadcasted_iota(jnp.int32, (a.shape[0], 2 * pdim), 1) < pdim
        return jnp.concatenate(
            [jnp.where(lo, a[:, 2 * p:2 * p + 1], a[:, 2 * p + 1:2 * p + 2]) for p in range(npair)], axis=1)

    dt_e = expand(dt)
    la_e = expand(la)
    la_end_e = expand(la[chunk - 1:chunk])
    cb = _qk(cm, bm)
    tri = (lax.broadcasted_iota(jnp.int32, (chunk, chunk), 0)
           >= lax.broadcasted_iota(jnp.int32, (chunk, chunk), 1))
    xdt = x * dt_e
    y_parts = []
    for p in range(npair):
        ms = []
        for hh in (2 * p, 2 * p + 1):
            seg = la[:, hh:hh + 1] - lat[hh:hh + 1, :]
            ms.append((cb * jnp.exp(jnp.where(tri, seg, -jnp.inf))).astype(BF16))
        xp = xdt[:, p * 2 * pdim:(p + 1) * 2 * pdim]
        rhs = jnp.concatenate([jnp.where(first, xp, 0.0), jnp.where(first, 0.0, xp)], axis=0).astype(BF16)
        y_parts.append(jnp.dot(jnp.concatenate(ms, axis=1), rhs, preferred_element_type=F32))
    y = jnp.concatenate(y_parts, axis=1)
    state = state_ref[...]
    y = y + jnp.dot(cm, state.astype(BF16), preferred_element_type=F32) * jnp.exp(la_e)
    to_end = jnp.exp(la_end_e - la_e) * dt_e
    upd = lax.dot_general(bm, (x * to_end).astype(BF16), (((0,), (0,)), ((), ())),
                          preferred_element_type=F32)
    state_ref[...] = state * jnp.exp(la_end_e) + upd
    y = y + d_ref[...] * x
    y = y * _silu(z_ref[...].astype(F32))
    y = y * lax.rsqrt(jnp.mean(y * y, axis=-1, keepdims=True) + eps)
    o_ref[...] = (y * g_ref[...]).astype(o_ref.dtype)


def ssd_scan(xbc, z, dt_g, la_g, lat_g, d_e, norm_g, batch, seq, d_inner, heads):
    m = xbc.shape[0]
    groups, nstate, chunk = SSM_GROUPS, SSM_STATE, SSM_CHUNK
    hpg = heads // groups
    pdim = d_inner // heads
    gw = hpg * pdim
    nc = seq // chunk
    xb = d_inner // nstate
    row = lambda b, g, c: b * nc + c
    return pl.pallas_call(
        functools.partial(_ssd_scan_kernel, hpg=hpg, pdim=pdim, eps=SSM_NORM_EPS),
        out_shape=jax.ShapeDtypeStruct((m, d_inner), BF16),
        grid=(batch, groups, nc),
        in_specs=[pl.BlockSpec((chunk, gw), lambda b, g, c: (row(b, g, c), g)),
                  pl.BlockSpec((chunk, nstate), lambda b, g, c: (row(b, g, c), xb + g)),
                  pl.BlockSpec((chunk, nstate), lambda b, g, c: (row(b, g, c), xb + groups + g)),
                  pl.BlockSpec((None, chunk, hpg), lambda b, g, c: (g, row(b, g, c), 0)),
                  pl.BlockSpec((None, chunk, hpg), lambda b, g, c: (g, row(b, g, c), 0)),
                  pl.BlockSpec((None, hpg, chunk), lambda b, g, c: (g, 0, row(b, g, c))),
                  pl.BlockSpec((chunk, gw), lambda b, g, c: (row(b, g, c), g)),
                  pl.BlockSpec((1, gw), lambda b, g, c: (0, g)),
                  pl.BlockSpec((1, gw), lambda b, g, c: (0, g))],
        out_specs=pl.BlockSpec((chunk, gw), lambda b, g, c: (row(b, g, c), g)),
        scratch_shapes=[pltpu.VMEM((nstate, gw), F32)],
        compiler_params=_params("parallel", "parallel", "arbitrary"),
        name="ssd_scan",
    )(xbc, xbc, xbc, dt_g, la_g, lat_g, z, d_e, norm_g)


def _q_colscale(ncols, nq, hd):
    return jnp.where(jnp.arange(ncols) < nq, (hd ** -0.5) * LOG2E, 1.0).astype(F32)


def fox_mixer(h, hb, ss, g, layer, w_in, b_f, w_o, batch, seq):
    d = hb.shape[1]
    heads = b_f.shape[0]
    hd = d // heads
    hp = _round_up(heads, LANES)
    gcol = g.astype(F32)[:, None]
    w_qkv = (gcol * w_in[layer, :, :3 * d] * _q_colscale(3 * d, d, hd)).astype(BF16)
    w_gate = jnp.pad(gcol * w_in[layer, :, 3 * d:], ((0, 0), (0, hp - heads))).astype(BF16)
    qkv = matmul(hb, w_qkv, out_dtype=BF16, rowstat=ss)
    f_logit = matmul(hb, w_gate, rowstat=ss)
    b_pad = jnp.pad(b_f.reshape(1, heads).astype(F32), ((0, 0), (0, hp - heads)))
    cum = fox_gate_cumsum(f_logit, b_pad, seq)
    attn = fox_attention(qkv, cum, batch, seq, heads, hd)
    return matmul(attn, cast_pad(w_o, layer), res=h, emit_stat=True)


def ssd_mixer(h, hb, ss, g, layer, w_in, conv_w, conv_b, dt_bias, a_log, d_skip, norm_g, w_out, batch, seq):
    heads = a_log.shape[0]
    d_inner = w_out.shape[1]
    conv_dim = conv_w.shape[1]
    groups = SSM_GROUPS
    hpg = heads // groups
    m = hb.shape[0]
    zxbc = matmul(hb, cast_pad(w_in, layer, 0, d_inner + conv_dim, rowscale=g), out_dtype=BF16,
                  rowstat=ss)
    dt_raw = matmul(hb, cast_pad(w_in, layer, d_inner + conv_dim, heads, rowscale=g, tc=LANES), rowstat=ss)
    xbc = ssm_conv_silu(zxbc, d_inner, conv_w, conv_b, batch, seq)
    dt, la = ssm_dt(dt_raw, dt_bias, a_log)
    dt_g = jnp.transpose(dt.reshape(m, groups, hpg), (1, 0, 2))
    la_g = jnp.transpose(la.reshape(m, groups, hpg), (1, 0, 2))
    lat_g = jnp.transpose(la_g, (0, 2, 1))
    pdim = d_inner // heads
    d_e = jnp.repeat(d_skip.astype(F32), pdim).reshape(1, d_inner)
    y = ssd_scan(xbc, zxbc, dt_g, la_g, lat_g, d_e, norm_g.reshape(1, d_inner).astype(F32),
                 batch, seq, d_inner, heads)
    return matmul(y, cast_pad(w_out, layer), res=h, emit_stat=True)


def diff_mixer(h, hb, ss, g, layer, w_in, lam, subln_g, w_o, lambda_init, batch, seq):
    d = hb.shape[1]
    hd = lam.shape[1]
    heads = d // (2 * hd)
    qkv = matmul(hb, cast_pad(w_in, layer, colscale=_q_colscale(3 * d, d, hd), rowscale=g),
                 out_dtype=BF16, rowstat=ss)
    attn = diff_attention(qkv, lam, subln_g, batch, seq, heads, hd, lambda_init)
    return matmul(attn, cast_pad(w_o, layer), res=h, emit_stat=True)


def conv_ffn(h, hb, ss, g, layer, w_up, conv_w, conv_b, w_down, seq):
    f = w_down.shape[1]
    fp = _round_up(f, FFN_TN)
    wu = cast_pad(w_up, layer, rowscale=g)
    wd = cast_pad(w_down, layer, out_rows=fp)
    cw = conv_w.astype(F32)
    cb = conv_b.reshape(1, 2 * f).astype(F32)
    pad = lambda a: jnp.pad(a, ((0, 0), (0, fp - f)))
    act = ffn_up(hb, ss, wu, pad(cw[:, :f]), pad(cw[:, f:]), pad(cb[:, :f]), pad(cb[:, f:]), seq)
    return matmul(act, wd, res=h, emit_stat=True)


def kernel(x, mix_norm_g, ffn_norm_g, fox_w_in, fox_b_f, fox_w_o, ssm_w_in, ssm_conv_w, ssm_conv_b,
           ssm_dt_bias, ssm_a_log, ssm_d, ssm_norm_g, ssm_w_out, diff_w_in, diff_lambda, diff_subln_g,
           diff_w_o, ffn_w_up, ffn_conv_w, ffn_conv_b, ffn_w_down, final_norm_g):
    batch, seq, d = x.shape
    depth = mix_norm_g.shape[0]
    h = x.reshape(batch * seq, d)
    hb, ss = norm_prep(h)
    for i in range(depth):
        kind, j = i % N_MIXERS, i // N_MIXERS
        g = mix_norm_g[i]
        if kind == 0:
            h, hb, ss = fox_mixer(h, hb, ss, g, j, fox_w_in, fox_b_f[j], fox_w_o, batch, seq)
        elif kind == 1:
            h, hb, ss = ssd_mixer(h, hb, ss, g, j, ssm_w_in, ssm_conv_w[j], ssm_conv_b[j], ssm_dt_bias[j],
                                  ssm_a_log[j], ssm_d[j], ssm_norm_g[j], ssm_w_out, batch, seq)
        else:
            lambda_init = 0.8 - 0.6 * math.exp(-0.3 * i)
            h, hb, ss = diff_mixer(h, hb, ss, g, j, diff_w_in, diff_lambda[j], diff_subln_g[j], diff_w_o,
                                   lambda_init, batch, seq)
        h, hb, ss = conv_ffn(h, hb, ss, ffn_norm_g[i], i, ffn_w_up, ffn_conv_w[i], ffn_conv_b[i],
                             ffn_w_down, seq)
    return rmsnorm(h, final_norm_g, NORM_EPS, F32).reshape(batch, seq, d)
```

```python
import collections
import functools
import math

import jax
import jax.numpy as jnp
from jax import lax
from jax.experimental import pallas as pl
from jax.experimental.pallas import tpu as pltpu

F32 = jnp.float32
BF16 = jnp.bfloat16
LOG2E = 1.4426950408889634

V7X_VMEM_LIMIT_BYTES = 56 * 1024 * 1024
LANES = 128
BF16_SUBLANES = 16

NORM_EPS = 1e-6
SSM_NORM_EPS = 1e-5
DIFF_SUBLN_EPS = 1e-5
SSM_GROUPS = 8
SSM_STATE = 128
SSM_CHUNK = 128
N_MIXERS = 3

MM_TM = 1024
MM_TN = 1024
MM_TK_MAX = 4096
MM_TK_SPLIT = 3072
FFN_TM = 1024
FFN_TN = 512
ATTN_TQ = 1024
FOX_HEADS_PER_STEP = 2
NORM_TM = 256
CONV_TS = 512
CONV_TC = 1024
CAST_TR = 1024
CAST_TC = 1024


def _tile(dim, pref):
    if dim <= pref:
        return dim
    t = pref
    while dim % t:
        t //= 2
    return t


def _round_up(x, m):
    return -(-x // m) * m


def _params(*sem):
    return pltpu.CompilerParams(dimension_semantics=sem, vmem_limit_bytes=V7X_VMEM_LIMIT_BYTES)


def _cast_kernel(w_ref, *refs, rows, cols, padded, has_colscale):
    o_ref = refs[-1]
    tr, tc = o_ref.shape
    w = w_ref[...]
    if has_colscale:
        w = w * refs[0][...]
    if padded:
        r = lax.broadcasted_iota(jnp.int32, (tr, tc), 0) + pl.program_id(0) * tr
        c = lax.broadcasted_iota(jnp.int32, (tr, tc), 1) + pl.program_id(1) * tc
        w = jnp.where(jnp.logical_and(r < rows, c < cols), w, 0.0)
    o_ref[...] = w.astype(o_ref.dtype)


def cast_pad(w, layer, col_start=0, ncols=None, out_rows=None, out_cols=None, colscale=None, tc=CAST_TC):
    _, rows, wcols = w.shape
    ncols = wcols - col_start if ncols is None else ncols
    out_rows = rows if out_rows is None else out_rows
    out_cols = ncols if out_cols is None else out_cols
    tr = _tile(out_rows, CAST_TR)
    tc = _tile(out_cols, tc)
    assert col_start % tc == 0
    cb0 = col_start // tc
    last_rb = (rows - 1) // tr
    last_cb = (wcols - 1) // tc
    in_specs = [pl.BlockSpec((None, tr, tc),
                             lambda i, j: (layer, jnp.minimum(i, last_rb), jnp.minimum(j + cb0, last_cb)))]
    args = [w]
    if colscale is not None:
        in_specs.append(pl.BlockSpec((1, tc), lambda i, j: (0, j)))
        args.append(colscale.reshape(1, out_cols).astype(F32))
    padded = out_rows != rows or out_cols != ncols
    return pl.pallas_call(
        functools.partial(_cast_kernel, rows=rows, cols=ncols, padded=padded,
                          has_colscale=colscale is not None),
        out_shape=jax.ShapeDtypeStruct((out_rows, out_cols), BF16),
        grid=(out_rows // tr, out_cols // tc),
        in_specs=in_specs,
        out_specs=pl.BlockSpec((tr, tc), lambda i, j: (i, j)),
        compiler_params=_params("parallel", "parallel"),
        name="cast_pad",
    )(*args)


def _rmsnorm_kernel(x_ref, g_ref, o_ref, *, eps):
    x = x_ref[...]
    ms = jnp.mean(x * x, axis=-1, keepdims=True)
    o_ref[...] = ((x * lax.rsqrt(ms + eps)) * g_ref[...]).astype(o_ref.dtype)


def rmsnorm(x, g, eps, out_dtype):
    m, d = x.shape
    tm = _tile(m, NORM_TM)
    return pl.pallas_call(
        functools.partial(_rmsnorm_kernel, eps=eps),
        out_shape=jax.ShapeDtypeStruct((m, d), out_dtype),
        grid=(m // tm,),
        in_specs=[pl.BlockSpec((tm, d), lambda i: (i, 0)),
                  pl.BlockSpec((1, d), lambda i: (0, 0))],
        out_specs=pl.BlockSpec((tm, d), lambda i: (i, 0)),
        compiler_params=_params("parallel"),
        name="rmsnorm",
    )(x, g.reshape(1, d).astype(F32))


SideCast = collections.namedtuple("SideCast", "src layer out_rows")


def _side_plan(side, nsteps):
    r = BF16_SUBLANES
    while side.out_rows % r or side.out_rows // r > nsteps:
        r += BF16_SUBLANES
        assert r <= side.out_rows
    return r, side.out_rows // r


def _side_specs(side, nsteps, step_of):
    _, rows, cols = side.src.shape
    r, nblocks = _side_plan(side, nsteps)
    last_src = (rows - 1) // r
    blk = lambda *idx: jnp.minimum(step_of(*idx), nblocks - 1)
    in_spec = pl.BlockSpec((None, r, cols), lambda *idx: (side.layer, jnp.minimum(blk(*idx), last_src), 0))
    out_spec = pl.BlockSpec((r, cols), lambda *idx: (blk(*idx), 0))
    return in_spec, out_spec, jax.ShapeDtypeStruct((side.out_rows, cols), BF16), nblocks


def _side_cast(src_ref, dst_ref, step, nblocks, rows):
    w = src_ref[...]
    r = dst_ref.shape[0]
    if nblocks * r != rows:
        row = lax.broadcasted_iota(jnp.int32, w.shape, 0) + jnp.minimum(step, nblocks - 1) * r
        w = jnp.where(row < rows, w, 0.0)
    dst_ref[...] = w.astype(dst_ref.dtype)


def _mm_kernel(*refs, nk, has_res, side):
    if side is not None:
        n_in = 3 if has_res else 2
        nj = pl.num_programs(1)
        step = (pl.program_id(0) * nj + pl.program_id(1)) * nk + pl.program_id(2)
        _side_cast(refs[n_in], refs[n_in + 2], step, *side)
        refs = refs[:n_in] + (refs[n_in + 1],) + refs[n_in + 3:]
    if has_res:
        x_ref, w_ref, r_ref, o_ref = refs[:4]
    else:
        x_ref, w_ref, o_ref = refs[:3]
        r_ref = None
    part = jnp.dot(x_ref[...], w_ref[...], preferred_element_type=F32)

    def finish(acc):
        if has_res:
            acc = acc + r_ref[...]
        o_ref[...] = acc.astype(o_ref.dtype)

    if nk == 1:
        finish(part)
        return
    acc_ref = refs[-1]
    k = pl.program_id(2)

    @pl.when(jnp.logical_and(pl.program_id(0) == 0, jnp.logical_and(pl.program_id(1) == 0, k == 0)))
    def _():
        acc_ref[...] = jnp.zeros_like(acc_ref)

    acc_ref[...] = jnp.where(k == 0, 0.0, acc_ref[...]) + part

    @pl.when(k == nk - 1)
    def _():
        finish(acc_ref[...])


def matmul(x, w, res=None, out_dtype=F32, side=None):
    m, kdim = x.shape
    n = w.shape[1]
    tm, tn = _tile(m, MM_TM), _tile(n, MM_TN)
    tk = kdim
    if kdim > MM_TK_MAX:
        tk = MM_TK_SPLIT
        while kdim % tk:
            tk -= 2 * LANES
    nk = kdim // tk
    in_specs = [pl.BlockSpec((tm, tk), lambda i, j, k: (i, k)),
                pl.BlockSpec((tk, tn), lambda i, j, k: (k, j))]
    args = [x, w]
    if res is not None:
        in_specs.append(pl.BlockSpec((tm, tn), lambda i, j, k: (i, j)))
        args.append(res)
    grid = (m // tm, n // tn, nk)
    out_shape = jax.ShapeDtypeStruct((m, n), out_dtype)
    out_specs = pl.BlockSpec((tm, tn), lambda i, j, k: (i, j))
    side_static = None
    if side is not None:
        nj = grid[1]
        s_in, s_out, s_shape, nblocks = _side_specs(side, math.prod(grid), lambda i, j, k: (i * nj + j) * nk + k)
        in_specs.append(s_in)
        args.append(side.src)
        out_shape, out_specs = (out_shape, s_shape), (out_specs, s_out)
        side_static = (nblocks, side.src.shape[1])
    return pl.pallas_call(
        functools.partial(_mm_kernel, nk=nk, has_res=res is not None, side=side_static),
        out_shape=out_shape,
        grid=grid,
        in_specs=in_specs,
        out_specs=out_specs,
        scratch_shapes=[pltpu.VMEM((tm, tn), F32)] if nk > 1 else [],
        compiler_params=_params("arbitrary", "arbitrary", "arbitrary"),
        name="matmul",
    )(*args)


def _causal_conv_rows(p, halo, w, b):
    kw = w.shape[0]
    hr = halo.shape[0]
    top = p[0:hr]
    u = b + w[kw - 1:kw] * p
    ut = b + w[kw - 1:kw] * top
    row = lax.broadcasted_iota(jnp.int32, top.shape, 0)
    for k in range(1, kw):
        wk = w[kw - 1 - k:kw - k]
        u = u + wk * pltpu.roll(p, k, 0)
        shifted_top = jnp.where(row < k, pltpu.roll(halo, k, 0), pltpu.roll(top, k, 0))
        ut = ut + wk * shifted_top
    return jnp.concatenate([ut, u[hr:]], axis=0)


def _silu(x):
    return x / (1.0 + jnp.exp(-x))


def _ffn_up_kernel(x_ref, wgl_ref, wgh_ref, wvl_ref, wvh_ref, cwg_ref, cwv_ref, cbg_ref, cbv_ref,
                   side_src, o_ref, side_dst, halo_g, halo_v, *, tiles_per_seq, side):
    i = pl.program_id(1)
    _side_cast(side_src, side_dst, pl.program_id(0) * pl.num_programs(1) + i, *side)

    @pl.when(i % tiles_per_seq == 0)
    def _():
        halo_g[...] = jnp.zeros_like(halo_g)
        halo_v[...] = jnp.zeros_like(halo_v)

    x = x_ref[...]
    wg = jnp.concatenate([wgl_ref[...], wgh_ref[...]], axis=1)
    wv = jnp.concatenate([wvl_ref[...], wvh_ref[...]], axis=1)
    pg = jnp.dot(x, wg, preferred_element_type=F32)
    pv = jnp.dot(x, wv, preferred_element_type=F32)
    ug = _causal_conv_rows(pg, halo_g[...], cwg_ref[...], cbg_ref[...])
    uv = _causal_conv_rows(pv, halo_v[...], cwv_ref[...], cbv_ref[...])
    hr = halo_g.shape[0]
    halo_g[...] = pg[pg.shape[0] - hr:]
    halo_v[...] = pv[pv.shape[0] - hr:]
    o_ref[...] = (_silu(ug) * uv).astype(o_ref.dtype)


def ffn_up(x, w_up, cwg, cwv, cbg, cbv, seq, side):
    m, d = x.shape
    f = w_up.shape[1] // 2
    fp = cwg.shape[1]
    tm, tn = _tile(seq, FFN_TM), _tile(fp, FFN_TN)
    half = tn // 2
    assert f % half == 0
    nb, last = f // half, 2 * f // half - 1
    kw = cwg.shape[0]
    wspec = lambda off: pl.BlockSpec((d, half), lambda j, i: (0, jnp.minimum(2 * j + off, last)))
    cspec = pl.BlockSpec((kw, tn), lambda j, i: (0, j))
    bspec = pl.BlockSpec((1, tn), lambda j, i: (0, j))
    nj, ni = fp // tn, m // tm
    s_in, s_out, s_shape, nblocks = _side_specs(side, nj * ni, lambda j, i: j * ni + i)
    return pl.pallas_call(
        functools.partial(_ffn_up_kernel, tiles_per_seq=seq // tm, side=(nblocks, side.src.shape[1])),
        out_shape=(jax.ShapeDtypeStruct((m, fp), BF16), s_shape),
        grid=(nj, ni),
        in_specs=[pl.BlockSpec((tm, d), lambda j, i: (i, 0)),
                  wspec(0), wspec(1), wspec(nb), wspec(nb + 1), cspec, cspec, bspec, bspec, s_in],
        out_specs=(pl.BlockSpec((tm, tn), lambda j, i: (i, j)), s_out),
        scratch_shapes=[pltpu.VMEM((BF16_SUBLANES, tn), F32), pltpu.VMEM((BF16_SUBLANES, tn), F32)],
        compiler_params=_params("arbitrary", "arbitrary"),
        name="ffn_up",
    )(x, w_up, w_up, w_up, w_up, cwg, cwv, cbg, cbv, side.src)


def _cumsum_rows(y):
    n = y.shape[0]
    row = lax.broadcasted_iota(jnp.int32, y.shape, 0)
    shift = 1
    while shift < n:
        y = y + jnp.where(row >= shift, pltpu.roll(y, shift, 0), 0.0)
        shift *= 2
    return y


def _fox_gate_kernel(f_ref, b_ref, o_ref):
    x = f_ref[...] + b_ref[...]
    log_f = jnp.minimum(x, 0.0) - jnp.log(1.0 + jnp.exp(-jnp.abs(x)))
    o_ref[...] = _cumsum_rows(log_f)


def fox_gate_cumsum(f_logit, b_f, seq):
    m, hp = f_logit.shape
    return pl.pallas_call(
        _fox_gate_kernel,
        out_shape=jax.ShapeDtypeStruct((m, hp), F32),
        grid=(m // seq,),
        in_specs=[pl.BlockSpec((seq, hp), lambda b: (b, 0)), pl.BlockSpec((1, hp), lambda b: (0, 0))],
        out_specs=pl.BlockSpec((seq, hp), lambda b: (b, 0)),
        compiler_params=_params("parallel"),
        name="fox_gate_cumsum",
    )(f_logit, b_f)


def _bias_lanes(x, pieces_first):
    hi = x.astype(BF16).astype(F32)
    rem = x - hi
    lo = rem.astype(BF16).astype(F32)
    lo2 = rem - lo
    lane = lax.broadcasted_iota(jnp.int32, (x.shape[0], LANES), 1)
    base = 0 if pieces_first else 3
    ones = 3 if pieces_first else 0
    out = jnp.where(lane == base, hi, jnp.where(lane == base + 1, lo, jnp.where(lane == base + 2, lo2, 0.0)))
    out = jnp.where(jnp.logical_and(lane >= ones, lane < ones + 3), 1.0, out)
    return out.astype(BF16)


def _qk(q, k):
    return lax.dot_general(q, k, (((1,), (1,)), ((), ())), preferred_element_type=F32)


def _online_softmax_step(s, v, carry):
    m, l, acc = carry
    m_new = jnp.maximum(m, jnp.max(s, axis=-1, keepdims=True))
    alpha = jnp.exp2(m - m_new)
    p = jnp.exp2(s - m_new)
    l = alpha * l + jnp.sum(p, axis=-1, keepdims=True)
    acc = alpha * acc + jnp.dot(p.astype(v.dtype), v, preferred_element_type=F32)
    return m_new, l, acc


def _softmax_init(tq, ev):
    return (jnp.full((tq, 1), -jnp.inf, F32), jnp.zeros((tq, 1), F32), jnp.zeros((tq, ev), F32))


def _causal_mask(s):
    r = lax.broadcasted_iota(jnp.int32, s.shape, 0)
    c = lax.broadcasted_iota(jnp.int32, s.shape, 1)
    return jnp.where(c <= r, s, -jnp.inf)


def _fox_attn_kernel(q_ref, k_ref, v_ref, cum_ref, o_ref, kaug_ref, *, tq, hd):
    hg = pl.program_id(1)
    qi = pl.program_id(2)
    nh = k_ref.shape[-1] // hd
    nq = k_ref.shape[0] // tq

    def gate(rows, a):
        lane = lax.broadcasted_iota(jnp.int32, rows.shape, 1)
        return jnp.sum(jnp.where(lane == hg * nh + a, rows, 0.0), axis=-1, keepdims=True) * LOG2E

    @pl.when(qi == 0)
    def _():
        for c in range(nq):
            sl = slice(c * tq, (c + 1) * tq)
            rows = cum_ref[sl, :]
            for a in range(nh):
                kaug_ref[a, sl, :hd] = k_ref[sl, a * hd:(a + 1) * hd]
                kaug_ref[a, sl, hd:] = _bias_lanes(-gate(rows, a), True)

    q_start = pl.multiple_of(qi * tq, tq)
    q_rows = cum_ref[pl.ds(q_start, tq), :]
    q = q_ref[...]
    qs = [jnp.concatenate([q[:, a * hd:(a + 1) * hd], _bias_lanes(gate(q_rows, a), False)], axis=1)
          for a in range(nh)]

    def tile(j, a):
        start = pl.multiple_of(j * tq, tq)
        return _qk(qs[a], kaug_ref[a, pl.ds(start, tq), :]), v_ref[pl.ds(start, tq), a * hd:(a + 1) * hd]

    def body(j, carry):
        return tuple(_online_softmax_step(*tile(j, a), carry[a]) for a in range(nh))

    carry = lax.fori_loop(0, qi, body, (_softmax_init(tq, hd),) * nh)
    for a in range(nh):
        s, v = tile(qi, a)
        _, l, acc = _online_softmax_step(_causal_mask(s), v, carry[a])
        o_ref[:, a * hd:(a + 1) * hd] = (acc / l).astype(o_ref.dtype)


def fox_attention(qkv, cum, batch, seq, heads, hd):
    assert hd == LANES
    m = qkv.shape[0]
    tq = _tile(seq, ATTN_TQ)
    nq = seq // tq
    hp = cum.shape[1]
    nh = FOX_HEADS_PER_STEP
    hg = heads // nh
    w = nh * hd
    return pl.pallas_call(
        functools.partial(_fox_attn_kernel, tq=tq, hd=hd),
        out_shape=jax.ShapeDtypeStruct((m, heads * hd), BF16),
        grid=(batch, hg, nq),
        in_specs=[pl.BlockSpec((tq, w), lambda b, h, i: (b * nq + i, h)),
                  pl.BlockSpec((seq, w), lambda b, h, i: (b, hg + h)),
                  pl.BlockSpec((seq, w), lambda b, h, i: (b, 2 * hg + h)),
                  pl.BlockSpec((seq, hp), lambda b, h, i: (b, 0))],
        out_specs=pl.BlockSpec((tq, w), lambda b, h, i: (b * nq + i, h)),
        scratch_shapes=[pltpu.VMEM((nh, seq, hd + LANES), BF16)],
        compiler_params=_params("parallel", "parallel", "arbitrary"),
        name="fox_attention",
    )(qkv, qkv, qkv, cum)


def _diff_attn_kernel(slope_ref, q_ref, k_ref, v_ref, lam_ref, g_ref, o_ref, kaug_ref, *,
                      tq, hd, lambda_init, eps):
    h = pl.program_id(1)
    qi = pl.program_id(2)
    slope2 = slope_ref[h] * LOG2E
    nq = k_ref.shape[0] // tq

    def alibi(start):
        return slope2 * (lax.broadcasted_iota(jnp.int32, (tq, 1), 0) + start).astype(F32)

    @pl.when(qi == 0)
    def _():
        for c in range(nq):
            sl = slice(c * tq, (c + 1) * tq)
            kx = _bias_lanes(alibi(c * tq), True)
            for a in range(2):
                kaug_ref[a, sl, :hd] = k_ref[sl, a * hd:(a + 1) * hd]
                kaug_ref[a, sl, hd:] = kx

    qx = _bias_lanes(-alibi(qi * tq), False)
    q = q_ref[...]
    qs = [jnp.concatenate([q[:, a * hd:(a + 1) * hd], qx], axis=1) for a in range(2)]

    ev = v_ref.shape[-1]

    def tile(j):
        start = pl.multiple_of(j * tq, tq)
        return [_qk(qs[a], kaug_ref[a, pl.ds(start, tq), :]) for a in range(2)], v_ref[pl.ds(start, tq), :]

    def body(j, carry):
        ss, v = tile(j)
        return tuple(_online_softmax_step(ss[a], v, carry[a]) for a in range(2))

    one = _softmax_init(tq, ev)
    carry = lax.fori_loop(0, qi, body, (one, one))
    ss, v = tile(qi)
    outs = []
    for a in range(2):
        _, l, acc = _online_softmax_step(_causal_mask(ss[a]), v, carry[a])
        outs.append(acc / l)
    lam = lam_ref[...]
    lam_full = (jnp.exp(jnp.sum(lam[0:1] * lam[1:2], axis=-1, keepdims=True))
                - jnp.exp(jnp.sum(lam[2:3] * lam[3:4], axis=-1, keepdims=True)) + lambda_init)
    out = outs[0] - lam_full * outs[1]
    ms = jnp.mean(out * out, axis=-1, keepdims=True)
    y = (out * lax.rsqrt(ms + eps)) * g_ref[...]
    o_ref[...] = (y * (1.0 - lambda_init)).astype(o_ref.dtype)


def diff_attention(qkv, lam, subln_g, batch, seq, heads, hd, lambda_init):
    assert hd == LANES
    m = qkv.shape[0]
    tq = _tile(seq, ATTN_TQ)
    nq = seq // tq
    ev = 2 * hd
    slopes = jnp.exp2(-8.0 * jnp.arange(1, heads + 1, dtype=F32) / heads)
    grid_spec = pltpu.PrefetchScalarGridSpec(
        num_scalar_prefetch=1,
        grid=(batch, heads, nq),
        in_specs=[pl.BlockSpec((tq, ev), lambda b, h, i, s: (b * nq + i, h)),
                  pl.BlockSpec((seq, ev), lambda b, h, i, s: (b, heads + h)),
                  pl.BlockSpec((seq, ev), lambda b, h, i, s: (b, 2 * heads + h)),
                  pl.BlockSpec((4, hd), lambda b, h, i, s: (0, 0)),
                  pl.BlockSpec((1, ev), lambda b, h, i, s: (0, 0))],
        out_specs=pl.BlockSpec((tq, ev), lambda b, h, i, s: (b * nq + i, h)),
        scratch_shapes=[pltpu.VMEM((2, seq, hd + LANES), BF16)],
    )
    return pl.pallas_call(
        functools.partial(_diff_attn_kernel, tq=tq, hd=hd, lambda_init=lambda_init, eps=DIFF_SUBLN_EPS),
        out_shape=jax.ShapeDtypeStruct((m, heads * ev), BF16),
        grid_spec=grid_spec,
        compiler_params=_params("parallel", "parallel", "arbitrary"),
        name="diff_attention",
    )(slopes, qkv, qkv, qkv, lam.astype(F32), subln_g.reshape(1, ev).astype(F32))


def _ssm_conv_kernel(x_ref, halo_ref, w_ref, b_ref, o_ref):
    i = pl.program_id(1)
    halo = jnp.where(i == 0, 0.0, halo_ref[...].astype(F32))
    u = _causal_conv_rows(x_ref[...].astype(F32), halo, w_ref[...], b_ref[...])
    o_ref[...] = _silu(u).astype(o_ref.dtype)


def ssm_conv_silu(zxbc, col_start, w, b, batch, seq):
    m = zxbc.shape[0]
    cdim = w.shape[1]
    ts, tc = _tile(seq, CONV_TS), _tile(cdim, CONV_TC)
    assert col_start % tc == 0
    cb0 = col_start // tc
    ns = seq // ts
    hb = ts // BF16_SUBLANES
    return pl.pallas_call(
        _ssm_conv_kernel,
        out_shape=jax.ShapeDtypeStruct((m, cdim), BF16),
        grid=(batch, ns, cdim // tc),
        in_specs=[pl.BlockSpec((ts, tc), lambda bi, i, j: (bi * ns + i, cb0 + j)),
                  pl.BlockSpec((BF16_SUBLANES, tc),
                               lambda bi, i, j: (jnp.maximum((bi * ns + i) * hb - 1, 0), cb0 + j)),
                  pl.BlockSpec((w.shape[0], tc), lambda bi, i, j: (0, j)),
                  pl.BlockSpec((1, tc), lambda bi, i, j: (0, j))],
        out_specs=pl.BlockSpec((ts, tc), lambda bi, i, j: (bi * ns + i, j)),
        compiler_params=_params("parallel", "parallel", "parallel"),
        name="ssm_conv_silu",
    )(zxbc, zxbc, w.astype(F32), b.reshape(1, cdim).astype(F32))


def _ssm_dt_kernel(dt_ref, bias_ref, alog_ref, dt_out, la_out):
    x = dt_ref[...] + bias_ref[...]
    dt = jnp.maximum(x, 0.0) + jnp.log(1.0 + jnp.exp(-jnp.abs(x)))
    dt_out[...] = dt
    la_out[...] = _cumsum_rows(dt * (-jnp.exp(alog_ref[...])))


def ssm_dt(dt_raw, dt_bias, a_log):
    m, h = dt_raw.shape
    spec = pl.BlockSpec((SSM_CHUNK, h), lambda i: (i, 0))
    pspec = pl.BlockSpec((1, h), lambda i: (0, 0))
    return pl.pallas_call(
        _ssm_dt_kernel,
        out_shape=(jax.ShapeDtypeStruct((m, h), F32), jax.ShapeDtypeStruct((m, h), F32)),
        grid=(m // SSM_CHUNK,),
        in_specs=[spec, pspec, pspec],
        out_specs=(spec, spec),
        compiler_params=_params("parallel"),
        name="ssm_dt",
    )(dt_raw, dt_bias.reshape(1, h).astype(F32), a_log.reshape(1, h).astype(F32))


def _ssd_scan_kernel(x_ref, b_ref, c_ref, dt_ref, la_ref, lat_ref, z_ref, d_ref, g_ref, o_ref,
                     state_ref, *, hpg, pdim, eps):
    ci = pl.program_id(2)

    @pl.when(ci == 0)
    def _():
        state_ref[...] = jnp.zeros_like(state_ref)

    chunk = x_ref.shape[0]
    npair = hpg // 2
    x = x_ref[...].astype(F32)
    bm = b_ref[...]
    cm = c_ref[...]
    dt = dt_ref[...]
    la = la_ref[...]
    lat = lat_ref[...]
    lane = lax.broadcasted_iota(jnp.int32, (chunk, 2 * pdim), 1)
    first = lane < pdim

    def expand(a):
        lo = lax.broadcasted_iota(jnp.int32, (a.shape[0], 2 * pdim), 1) < pdim
        return jnp.concatenate(
            [jnp.where(lo, a[:, 2 * p:2 * p + 1], a[:, 2 * p + 1:2 * p + 2]) for p in range(npair)], axis=1)

    dt_e = expand(dt)
    la_e = expand(la)
    la_end_e = expand(la[chunk - 1:chunk])
    cb = _qk(cm, bm)
    tri = (lax.broadcasted_iota(jnp.int32, (chunk, chunk), 0)
           >= lax.broadcasted_iota(jnp.int32, (chunk, chunk), 1))
    xdt = x * dt_e
    y_parts = []
    for p in range(npair):
        ms = []
        for hh in (2 * p, 2 * p + 1):
            seg = la[:, hh:hh + 1] - lat[hh:hh + 1, :]
            ms.append((cb * jnp.exp(jnp.where(tri, seg, -jnp.inf))).astype(BF16))
        xp = xdt[:, p * 2 * pdim:(p + 1) * 2 * pdim]
        rhs = jnp.concatenate([jnp.where(first, xp, 0.0), jnp.where(first, 0.0, xp)], axis=0).astype(BF16)
        y_parts.append(jnp.dot(jnp.concatenate(ms, axis=1), rhs, preferred_element_type=F32))
    y = jnp.concatenate(y_parts, axis=1)
    state = state_ref[...]
    y = y + jnp.dot(cm, state.astype(BF16), preferred_element_type=F32) * jnp.exp(la_e)
    to_end = jnp.exp(la_end_e - la_e) * dt_e
    upd = lax.dot_general(bm, (x * to_end).astype(BF16), (((0,), (0,)), ((), ())),
                          preferred_element_type=F32)
    state_ref[...] = state * jnp.exp(la_end_e) + upd
    y = y + d_ref[...] * x
    y = y * _silu(z_ref[...].astype(F32))
    y = y * lax.rsqrt(jnp.mean(y * y, axis=-1, keepdims=True) + eps)
    o_ref[...] = (y * g_ref[...]).astype(o_ref.dtype)


def ssd_scan(xbc, z, dt_g, la_g, lat_g, d_e, norm_g, batch, seq, d_inner, heads):
    m = xbc.shape[0]
    groups, nstate, chunk = SSM_GROUPS, SSM_STATE, SSM_CHUNK
    hpg = heads // groups
    pdim = d_inner // heads
    gw = hpg * pdim
    nc = seq // chunk
    xb = d_inner // nstate
    row = lambda b, g, c: b * nc + c
    return pl.pallas_call(
        functools.partial(_ssd_scan_kernel, hpg=hpg, pdim=pdim, eps=SSM_NORM_EPS),
        out_shape=jax.ShapeDtypeStruct((m, d_inner), BF16),
        grid=(batch, groups, nc),
        in_specs=[pl.BlockSpec((chunk, gw), lambda b, g, c: (row(b, g, c), g)),
                  pl.BlockSpec((chunk, nstate), lambda b, g, c: (row(b, g, c), xb + g)),
                  pl.BlockSpec((chunk, nstate), lambda b, g, c: (row(b, g, c), xb + groups + g)),
                  pl.BlockSpec((None, chunk, hpg), lambda b, g, c: (g, row(b, g, c), 0)),
                  pl.BlockSpec((None, chunk, hpg), lambda b, g, c: (g, row(b, g, c), 0)),
                  pl.BlockSpec((None, hpg, chunk), lambda b, g, c: (g, 0, row(b, g, c))),
                  pl.BlockSpec((chunk, gw), lambda b, g, c: (row(b, g, c), g)),
                  pl.BlockSpec((1, gw), lambda b, g, c: (0, g)),
                  pl.BlockSpec((1, gw), lambda b, g, c: (0, g))],
        out_specs=pl.BlockSpec((chunk, gw), lambda b, g, c: (row(b, g, c), g)),
        scratch_shapes=[pltpu.VMEM((nstate, gw), F32)],
        compiler_params=_params("parallel", "parallel", "arbitrary"),
        name="ssd_scan",
    )(xbc, xbc, xbc, dt_g, la_g, lat_g, z, d_e, norm_g)


def _q_colscale(ncols, nq, hd):
    return jnp.where(jnp.arange(ncols) < nq, (hd ** -0.5) * LOG2E, 1.0).astype(F32)


def fox_mixer(h, hn, layer, w_in, b_f, w_o, batch, seq, side):
    d = hn.shape[1]
    heads = b_f.shape[0]
    hd = d // heads
    hp = _round_up(heads, LANES)
    w_qkv = (w_in[layer, :, :3 * d] * _q_colscale(3 * d, d, hd)).astype(BF16)
    w_gate = jnp.pad(w_in[layer, :, 3 * d:], ((0, 0), (0, hp - heads))).astype(BF16)
    qkv, side_out = matmul(hn, w_qkv, out_dtype=BF16, side=side)
    f_logit = matmul(hn, w_gate)
    b_pad = jnp.pad(b_f.reshape(1, heads).astype(F32), ((0, 0), (0, hp - heads)))
    cum = fox_gate_cumsum(f_logit, b_pad, seq)
    attn = fox_attention(qkv, cum, batch, seq, heads, hd)
    return matmul(attn, cast_pad(w_o, layer), res=h), side_out


def ssd_mixer(h, hn, layer, w_in, conv_w, conv_b, dt_bias, a_log, d_skip, norm_g, w_out, batch, seq, side):
    heads = a_log.shape[0]
    d_inner = w_out.shape[1]
    conv_dim = conv_w.shape[1]
    groups = SSM_GROUPS
    hpg = heads // groups
    m = hn.shape[0]
    zxbc, side_out = matmul(hn, cast_pad(w_in, layer, 0, d_inner + conv_dim), out_dtype=BF16,
                            side=side)
    dt_raw = matmul(hn, cast_pad(w_in, layer, d_inner + conv_dim, heads, tc=LANES))
    xbc = ssm_conv_silu(zxbc, d_inner, conv_w, conv_b, batch, seq)
    dt, la = ssm_dt(dt_raw, dt_bias, a_log)
    dt_g = jnp.transpose(dt.reshape(m, groups, hpg), (1, 0, 2))
    la_g = jnp.transpose(la.reshape(m, groups, hpg), (1, 0, 2))
    lat_g = jnp.transpose(la_g, (0, 2, 1))
    pdim = d_inner // heads
    d_e = jnp.repeat(d_skip.astype(F32), pdim).reshape(1, d_inner)
    y = ssd_scan(xbc, zxbc, dt_g, la_g, lat_g, d_e, norm_g.reshape(1, d_inner).astype(F32),
                 batch, seq, d_inner, heads)
    return matmul(y, cast_pad(w_out, layer), res=h), side_out


def diff_mixer(h, hn, layer, w_in, lam, subln_g, w_o, lambda_init, batch, seq, side):
    d = hn.shape[1]
    hd = lam.shape[1]
    heads = d // (2 * hd)
    qkv, side_out = matmul(hn, cast_pad(w_in, layer, colscale=_q_colscale(3 * d, d, hd)), out_dtype=BF16,
                           side=side)
    attn = diff_attention(qkv, lam, subln_g, batch, seq, heads, hd, lambda_init)
    return matmul(attn, cast_pad(w_o, layer), res=h), side_out


def _ffn_pad(w_down):
    return _round_up(w_down.shape[1], FFN_TN)


def conv_ffn(h, hn, layer, wu, conv_w, conv_b, w_down, seq):
    f = w_down.shape[1]
    fp = _ffn_pad(w_down)
    cw = conv_w.astype(F32)
    cb = conv_b.reshape(1, 2 * f).astype(F32)
    pad = lambda a: jnp.pad(a, ((0, 0), (0, fp - f)))
    act, wd = ffn_up(hn, wu, pad(cw[:, :f]), pad(cw[:, f:]), pad(cb[:, :f]), pad(cb[:, f:]), seq,
                     SideCast(w_down, layer, fp))
    return matmul(act, wd, res=h)


def kernel(x, mix_norm_g, ffn_norm_g, fox_w_in, fox_b_f, fox_w_o, ssm_w_in, ssm_conv_w, ssm_conv_b,
           ssm_dt_bias, ssm_a_log, ssm_d, ssm_norm_g, ssm_w_out, diff_w_in, diff_lambda, diff_subln_g,
           diff_w_o, ffn_w_up, ffn_conv_w, ffn_conv_b, ffn_w_down, final_norm_g):
    batch, seq, d = x.shape
    depth = mix_norm_g.shape[0]
    h = x.reshape(batch * seq, d)
    for i in range(depth):
        kind, j = i % N_MIXERS, i // N_MIXERS
        hn = rmsnorm(h, mix_norm_g[i], NORM_EPS, BF16)
        up = SideCast(ffn_w_up, i, ffn_w_up.shape[1])
        if kind == 0:
            h, wu = fox_mixer(h, hn, j, fox_w_in, fox_b_f[j], fox_w_o, batch, seq, up)
        elif kind == 1:
            h, wu = ssd_mixer(h, hn, j, ssm_w_in, ssm_conv_w[j], ssm_conv_b[j], ssm_dt_bias[j],
                              ssm_a_log[j], ssm_d[j], ssm_norm_g[j], ssm_w_out, batch, seq, up)
        else:
            lambda_init = 0.8 - 0.6 * math.exp(-0.3 * i)
            h, wu = diff_mixer(h, hn, j, diff_w_in, diff_lambda[j], diff_subln_g[j], diff_w_o,
                               lambda_init, batch, seq, up)
        hn = rmsnorm(h, ffn_norm_g[i], NORM_EPS, BF16)
        h = conv_ffn(h, hn, i, wu, ffn_conv_w[i], ffn_conv_b[i], ffn_w_down, seq)
    return rmsnorm(h, final_norm_g, NORM_EPS, F32).reshape(batch, seq, d)
```

```python
import collections
import functools
import math

import jax
import jax.numpy as jnp
from jax import lax
from jax.experimental import pallas as pl
from jax.experimental.pallas import tpu as pltpu

F32 = jnp.float32
BF16 = jnp.bfloat16
LOG2E = 1.4426950408889634

V7X_VMEM_LIMIT_BYTES = 56 * 1024 * 1024
LANES = 128
BF16_SUBLANES = 16

NORM_EPS = 1e-6
SSM_NORM_EPS = 1e-5
DIFF_SUBLN_EPS = 1e-5
SSM_GROUPS = 8
SSM_STATE = 128
SSM_CHUNK = 128
N_MIXERS = 3

MM_TM = 1024
MM_TN = 1024
MM_TK_MAX = 4096
MM_TK_SPLIT = 3072
FFN_TM = 1024
FFN_TN = 512
ATTN_TQ = 1024
FOX_HEADS_PER_STEP = 2
NORM_TM = 256
CONV_TS = 512
CONV_TC = 1024
CAST_STEPS = 64


def _tile(dim, pref):
    if dim <= pref:
        return dim
    t = pref
    while dim % t:
        t //= 2
    return t


def _round_up(x, m):
    return -(-x // m) * m


def _params(*sem):
    return pltpu.CompilerParams(dimension_semantics=sem, vmem_limit_bytes=V7X_VMEM_LIMIT_BYTES)


SideCast = collections.namedtuple("SideCast", "src layer out_rows colscale")


def _side_setup(side, nsteps, step_of):
    _, rows, cols = side.src.shape
    r = BF16_SUBLANES
    while side.out_rows % r or side.out_rows // r > nsteps:
        r += BF16_SUBLANES
        assert r <= side.out_rows
    nblocks = side.out_rows // r
    last_src = (rows - 1) // r
    blk = lambda *idx: jnp.minimum(step_of(*idx), nblocks - 1)
    in_specs = [pl.BlockSpec((None, r, cols), lambda *idx: (side.layer, jnp.minimum(blk(*idx), last_src), 0))]
    args = [side.src]
    if side.colscale is not None:
        in_specs.append(pl.BlockSpec((1, cols), lambda *idx: (0, 0)))
        args.append(side.colscale.reshape(1, cols).astype(F32))
    return dict(in_specs=in_specs, args=args, out_spec=pl.BlockSpec((r, cols), lambda *idx: (blk(*idx), 0)),
                out_shape=jax.ShapeDtypeStruct((side.out_rows, cols), BF16),
                static=(nblocks, rows, side.colscale is not None))


def _side_cast(in_refs, dst_ref, step, static):
    nblocks, rows, has_scale = static
    w = in_refs[0][...]
    if has_scale:
        w = w * in_refs[1][...]
    r = dst_ref.shape[0]
    if nblocks * r != rows:
        row = lax.broadcasted_iota(jnp.int32, w.shape, 0) + jnp.minimum(step, nblocks - 1) * r
        w = jnp.where(row < rows, w, 0.0)
    dst_ref[...] = w.astype(dst_ref.dtype)


def _cast_rows_kernel(*refs, static):
    _side_cast(refs[:-1], refs[-1], pl.program_id(0), static)


def cast_rows(side):
    setup = _side_setup(side, CAST_STEPS, lambda i: i)
    return pl.pallas_call(
        functools.partial(_cast_rows_kernel, static=setup["static"]),
        out_shape=setup["out_shape"],
        grid=(setup["static"][0],),
        in_specs=setup["in_specs"],
        out_specs=setup["out_spec"],
        compiler_params=_params("parallel"),
        name="cast_rows",
    )(*setup["args"])


def cast_layer(w, layer):
    return cast_rows(SideCast(w, layer, w.shape[1], None))


def _rmsnorm_kernel(x_ref, g_ref, o_ref, *, eps):
    x = x_ref[...]
    ms = jnp.mean(x * x, axis=-1, keepdims=True)
    o_ref[...] = ((x * lax.rsqrt(ms + eps)) * g_ref[...]).astype(o_ref.dtype)


def rmsnorm(x, g, eps, out_dtype):
    m, d = x.shape
    tm = _tile(m, NORM_TM)
    return pl.pallas_call(
        functools.partial(_rmsnorm_kernel, eps=eps),
        out_shape=jax.ShapeDtypeStruct((m, d), out_dtype),
        grid=(m // tm,),
        in_specs=[pl.BlockSpec((tm, d), lambda i: (i, 0)),
                  pl.BlockSpec((1, d), lambda i: (0, 0))],
        out_specs=pl.BlockSpec((tm, d), lambda i: (i, 0)),
        compiler_params=_params("parallel"),
        name="rmsnorm",
    )(x, g.reshape(1, d).astype(F32))


def _mm_kernel(*refs, nk, has_res, side, ncols):
    n_reg = 3 if has_res else 2
    n_side = 0
    if side is not None:
        n_side = 2 if side[2] else 1
        step = (pl.program_id(0) * pl.num_programs(1) + pl.program_id(1)) * nk + pl.program_id(2)
        _side_cast(refs[n_reg:n_reg + n_side], refs[n_reg + n_side + 1], step, side)
    x_ref, w_ref = refs[:2]
    r_ref = refs[2] if has_res else None
    o_ref = refs[n_reg + n_side]
    w = w_ref[...]
    tn = w.shape[1]
    if ncols % tn:
        col = lax.broadcasted_iota(jnp.int32, w.shape, 1) + pl.program_id(1) * tn
        w = jnp.where(col < ncols, w, jnp.zeros_like(w))
    part = jnp.dot(x_ref[...], w, preferred_element_type=F32)

    def finish(acc):
        if has_res:
            acc = acc + r_ref[...]
        o_ref[...] = acc.astype(o_ref.dtype)

    if nk == 1:
        finish(part)
        return
    acc_ref = refs[-1]
    k = pl.program_id(2)

    @pl.when(k == 0)
    def _():
        acc_ref[...] = part

    @pl.when(jnp.logical_and(k > 0, k < nk - 1))
    def _():
        acc_ref[...] += part

    @pl.when(k == nk - 1)
    def _():
        finish(acc_ref[...] + part)


def matmul(x, w, res=None, out_dtype=F32, side=None, col_start=0, ncols=None):
    m, kdim = x.shape
    ncols = w.shape[1] - col_start if ncols is None else ncols
    n = _round_up(ncols, LANES)
    tm, tn = _tile(m, MM_TM), _tile(n, MM_TN)
    assert col_start % tn == 0
    cb0 = col_start // tn
    tk = kdim
    if kdim > MM_TK_MAX:
        tk = MM_TK_SPLIT
        while kdim % tk:
            tk -= 2 * LANES
    nk = kdim // tk
    in_specs = [pl.BlockSpec((tm, tk), lambda i, j, k: (i, k)),
                pl.BlockSpec((tk, tn), lambda i, j, k: (k, cb0 + j))]
    args = [x, w]
    if res is not None:
        in_specs.append(pl.BlockSpec((tm, tn), lambda i, j, k: (i, j)))
        args.append(res)
    grid = (m // tm, n // tn, nk)
    out_shape = jax.ShapeDtypeStruct((m, n), out_dtype)
    out_specs = pl.BlockSpec((tm, tn), lambda i, j, k: (i, j))
    side_static = None
    if side is not None:
        nj = grid[1]
        setup = _side_setup(side, math.prod(grid), lambda i, j, k: (i * nj + j) * nk + k)
        in_specs += setup["in_specs"]
        args += setup["args"]
        out_shape, out_specs = (out_shape, setup["out_shape"]), (out_specs, setup["out_spec"])
        side_static = setup["static"]
    return pl.pallas_call(
        functools.partial(_mm_kernel, nk=nk, has_res=res is not None, side=side_static, ncols=ncols),
        out_shape=out_shape,
        grid=grid,
        in_specs=in_specs,
        out_specs=out_specs,
        scratch_shapes=[pltpu.VMEM((tm, tn), F32)] if nk > 1 else [],
        compiler_params=_params("arbitrary", "arbitrary", "arbitrary"),
        name="matmul",
    )(*args)


def _causal_conv_rows(p, halo, w, b):
    kw = w.shape[0]
    hr = halo.shape[0]
    top = p[0:hr]
    u = b + w[kw - 1:kw] * p
    ut = b + w[kw - 1:kw] * top
    row = lax.broadcasted_iota(jnp.int32, top.shape, 0)
    for k in range(1, kw):
        wk = w[kw - 1 - k:kw - k]
        u = u + wk * pltpu.roll(p, k, 0)
        shifted_top = jnp.where(row < k, pltpu.roll(halo, k, 0), pltpu.roll(top, k, 0))
        ut = ut + wk * shifted_top
    return jnp.concatenate([ut, u[hr:]], axis=0)


def _silu(x):
    return x / (1.0 + jnp.exp(-x))


def _ffn_up_kernel(x_ref, wgl_ref, wgh_ref, wvl_ref, wvh_ref, cwg_ref, cwv_ref, cbg_ref, cbv_ref,
                   side_src, o_ref, side_dst, halo_g, halo_v, *, tiles_per_seq, side):
    i = pl.program_id(1)
    _side_cast((side_src,), side_dst, pl.program_id(0) * pl.num_programs(1) + i, side)

    @pl.when(i % tiles_per_seq == 0)
    def _():
        halo_g[...] = jnp.zeros_like(halo_g)
        halo_v[...] = jnp.zeros_like(halo_v)

    x = x_ref[...]
    wg = jnp.concatenate([wgl_ref[...], wgh_ref[...]], axis=1)
    wv = jnp.concatenate([wvl_ref[...], wvh_ref[...]], axis=1)
    pg = jnp.dot(x, wg, preferred_element_type=F32)
    pv = jnp.dot(x, wv, preferred_element_type=F32)
    ug = _causal_conv_rows(pg, halo_g[...], cwg_ref[...], cbg_ref[...])
    uv = _causal_conv_rows(pv, halo_v[...], cwv_ref[...], cbv_ref[...])
    hr = halo_g.shape[0]
    halo_g[...] = pg[pg.shape[0] - hr:]
    halo_v[...] = pv[pv.shape[0] - hr:]
    o_ref[...] = (_silu(ug) * uv).astype(o_ref.dtype)


def ffn_up(x, w_up, cwg, cwv, cbg, cbv, seq, side):
    m, d = x.shape
    f = w_up.shape[1] // 2
    fp = cwg.shape[1]
    tm, tn = _tile(seq, FFN_TM), _tile(fp, FFN_TN)
    half = tn // 2
    assert f % half == 0
    nb, last = f // half, 2 * f // half - 1
    kw = cwg.shape[0]
    wspec = lambda off: pl.BlockSpec((d, half), lambda j, i: (0, jnp.minimum(2 * j + off, last)))
    cspec = pl.BlockSpec((kw, tn), lambda j, i: (0, j))
    bspec = pl.BlockSpec((1, tn), lambda j, i: (0, j))
    nj, ni = fp // tn, m // tm
    assert side.colscale is None
    setup = _side_setup(side, nj * ni, lambda j, i: j * ni + i)
    return pl.pallas_call(
        functools.partial(_ffn_up_kernel, tiles_per_seq=seq // tm, side=setup["static"]),
        out_shape=(jax.ShapeDtypeStruct((m, fp), BF16), setup["out_shape"]),
        grid=(nj, ni),
        in_specs=[pl.BlockSpec((tm, d), lambda j, i: (i, 0)),
                  wspec(0), wspec(1), wspec(nb), wspec(nb + 1), cspec, cspec, bspec, bspec] + setup["in_specs"],
        out_specs=(pl.BlockSpec((tm, tn), lambda j, i: (i, j)), setup["out_spec"]),
        scratch_shapes=[pltpu.VMEM((BF16_SUBLANES, tn), F32), pltpu.VMEM((BF16_SUBLANES, tn), F32)],
        compiler_params=_params("arbitrary", "arbitrary"),
        name="ffn_up",
    )(x, w_up, w_up, w_up, w_up, cwg, cwv, cbg, cbv, side.src)


def _cumsum_rows(y):
    n = y.shape[0]
    row = lax.broadcasted_iota(jnp.int32, y.shape, 0)
    shift = 1
    while shift < n:
        y = y + jnp.where(row >= shift, pltpu.roll(y, shift, 0), 0.0)
        shift *= 2
    return y


def _fox_gate_kernel(f_ref, b_ref, o_ref):
    x = f_ref[...] + b_ref[...]
    log_f = jnp.minimum(x, 0.0) - jnp.log(1.0 + jnp.exp(-jnp.abs(x)))
    o_ref[...] = _cumsum_rows(log_f)


def fox_gate_cumsum(f_logit, b_f, seq):
    m, hp = f_logit.shape
    return pl.pallas_call(
        _fox_gate_kernel,
        out_shape=jax.ShapeDtypeStruct((m, hp), F32),
        grid=(m // seq,),
        in_specs=[pl.BlockSpec((seq, hp), lambda b: (b, 0)), pl.BlockSpec((1, hp), lambda b: (0, 0))],
        out_specs=pl.BlockSpec((seq, hp), lambda b: (b, 0)),
        compiler_params=_params("parallel"),
        name="fox_gate_cumsum",
    )(f_logit, b_f)


def _bias_lanes(x, pieces_first):
    hi = x.astype(BF16).astype(F32)
    rem = x - hi
    lo = rem.astype(BF16).astype(F32)
    lo2 = rem - lo
    lane = lax.broadcasted_iota(jnp.int32, (x.shape[0], LANES), 1)
    base = 0 if pieces_first else 3
    ones = 3 if pieces_first else 0
    out = jnp.where(lane == base, hi, jnp.where(lane == base + 1, lo, jnp.where(lane == base + 2, lo2, 0.0)))
    out = jnp.where(jnp.logical_and(lane >= ones, lane < ones + 3), 1.0, out)
    return out.astype(BF16)


def _qk(q, k):
    return lax.dot_general(q, k, (((1,), (1,)), ((), ())), preferred_element_type=F32)


def _online_softmax_step(s, v, carry):
    m, l, acc = carry
    m_new = jnp.maximum(m, jnp.max(s, axis=-1, keepdims=True))
    alpha = jnp.exp2(m - m_new)
    p = jnp.exp2(s - m_new)
    l = alpha * l + jnp.sum(p, axis=-1, keepdims=True)
    acc = alpha * acc + jnp.dot(p.astype(v.dtype), v, preferred_element_type=F32)
    return m_new, l, acc


def _softmax_init(tq, ev):
    return (jnp.full((tq, 1), -jnp.inf, F32), jnp.zeros((tq, 1), F32), jnp.zeros((tq, ev), F32))


def _causal_mask(s):
    r = lax.broadcasted_iota(jnp.int32, s.shape, 0)
    c = lax.broadcasted_iota(jnp.int32, s.shape, 1)
    return jnp.where(c <= r, s, -jnp.inf)


def _fox_attn_kernel(q_ref, k_ref, v_ref, cum_ref, o_ref, kaug_ref, *, tq, hd):
    hg = pl.program_id(1)
    qi = pl.program_id(2)
    nh = k_ref.shape[-1] // hd
    nq = k_ref.shape[0] // tq

    def gate(rows, a):
        lane = lax.broadcasted_iota(jnp.int32, rows.shape, 1)
        return jnp.sum(jnp.where(lane == hg * nh + a, rows, 0.0), axis=-1, keepdims=True) * LOG2E

    @pl.when(qi == 0)
    def _():
        for c in range(nq):
            sl = slice(c * tq, (c + 1) * tq)
            rows = cum_ref[sl, :]
            for a in range(nh):
                kaug_ref[a, sl, :hd] = k_ref[sl, a * hd:(a + 1) * hd]
                kaug_ref[a, sl, hd:] = _bias_lanes(-gate(rows, a), True)

    q_start = pl.multiple_of(qi * tq, tq)
    q_rows = cum_ref[pl.ds(q_start, tq), :]
    q = q_ref[...]
    qs = [jnp.concatenate([q[:, a * hd:(a + 1) * hd], _bias_lanes(gate(q_rows, a), False)], axis=1)
          for a in range(nh)]

    def tile(j, a):
        start = pl.multiple_of(j * tq, tq)
        return _qk(qs[a], kaug_ref[a, pl.ds(start, tq), :]), v_ref[pl.ds(start, tq), a * hd:(a + 1) * hd]

    def body(j, carry):
        return tuple(_online_softmax_step(*tile(j, a), carry[a]) for a in range(nh))

    carry = lax.fori_loop(0, qi, body, (_softmax_init(tq, hd),) * nh)
    for a in range(nh):
        s, v = tile(qi, a)
        _, l, acc = _online_softmax_step(_causal_mask(s), v, carry[a])
        o_ref[:, a * hd:(a + 1) * hd] = (acc / l).astype(o_ref.dtype)


def fox_attention(qkv, cum, batch, seq, heads, hd):
    assert hd == LANES
    m = qkv.shape[0]
    tq = _tile(seq, ATTN_TQ)
    nq = seq // tq
    hp = cum.shape[1]
    nh = FOX_HEADS_PER_STEP
    hg = heads // nh
    w = nh * hd
    return pl.pallas_call(
        functools.partial(_fox_attn_kernel, tq=tq, hd=hd),
        out_shape=jax.ShapeDtypeStruct((m, heads * hd), BF16),
        grid=(batch, hg, nq),
        in_specs=[pl.BlockSpec((tq, w), lambda b, h, i: (b * nq + i, h)),
                  pl.BlockSpec((seq, w), lambda b, h, i: (b, hg + h)),
                  pl.BlockSpec((seq, w), lambda b, h, i: (b, 2 * hg + h)),
                  pl.BlockSpec((seq, hp), lambda b, h, i: (b, 0))],
        out_specs=pl.BlockSpec((tq, w), lambda b, h, i: (b * nq + i, h)),
        scratch_shapes=[pltpu.VMEM((nh, seq, hd + LANES), BF16)],
        compiler_params=_params("parallel", "parallel", "arbitrary"),
        name="fox_attention",
    )(qkv, qkv, qkv, cum)


def _diff_attn_kernel(slope_ref, q_ref, k_ref, v_ref, lam_ref, g_ref, o_ref, kaug_ref, *,
                      tq, hd, lambda_init, eps):
    h = pl.program_id(1)
    qi = pl.program_id(2)
    slope2 = slope_ref[h] * LOG2E
    nq = k_ref.shape[0] // tq

    def alibi(start):
        return slope2 * (lax.broadcasted_iota(jnp.int32, (tq, 1), 0) + start).astype(F32)

    @pl.when(qi == 0)
    def _():
        for c in range(nq):
            sl = slice(c * tq, (c + 1) * tq)
            kx = _bias_lanes(alibi(c * tq), True)
            for a in range(2):
                kaug_ref[a, sl, :hd] = k_ref[sl, a * hd:(a + 1) * hd]
                kaug_ref[a, sl, hd:] = kx

    qx = _bias_lanes(-alibi(qi * tq), False)
    q = q_ref[...]
    qs = [jnp.concatenate([q[:, a * hd:(a + 1) * hd], qx], axis=1) for a in range(2)]

    ev = v_ref.shape[-1]

    def tile(j):
        start = pl.multiple_of(j * tq, tq)
        return [_qk(qs[a], kaug_ref[a, pl.ds(start, tq), :]) for a in range(2)], v_ref[pl.ds(start, tq), :]

    def body(j, carry):
        ss, v = tile(j)
        return tuple(_online_softmax_step(ss[a], v, carry[a]) for a in range(2))

    one = _softmax_init(tq, ev)
    carry = lax.fori_loop(0, qi, body, (one, one))
    ss, v = tile(qi)
    outs = []
    for a in range(2):
        _, l, acc = _online_softmax_step(_causal_mask(ss[a]), v, carry[a])
        outs.append(acc / l)
    lam = lam_ref[...]
    lam_full = (jnp.exp(jnp.sum(lam[0:1] * lam[1:2], axis=-1, keepdims=True))
                - jnp.exp(jnp.sum(lam[2:3] * lam[3:4], axis=-1, keepdims=True)) + lambda_init)
    out = outs[0] - lam_full * outs[1]
    ms = jnp.mean(out * out, axis=-1, keepdims=True)
    y = (out * lax.rsqrt(ms + eps)) * g_ref[...]
    o_ref[...] = (y * (1.0 - lambda_init)).astype(o_ref.dtype)


def diff_attention(qkv, lam, subln_g, batch, seq, heads, hd, lambda_init):
    assert hd == LANES
    m = qkv.shape[0]
    tq = _tile(seq, ATTN_TQ)
    nq = seq // tq
    ev = 2 * hd
    slopes = jnp.exp2(-8.0 * jnp.arange(1, heads + 1, dtype=F32) / heads)
    grid_spec = pltpu.PrefetchScalarGridSpec(
        num_scalar_prefetch=1,
        grid=(batch, heads, nq),
        in_specs=[pl.BlockSpec((tq, ev), lambda b, h, i, s: (b * nq + i, h)),
                  pl.BlockSpec((seq, ev), lambda b, h, i, s: (b, heads + h)),
                  pl.BlockSpec((seq, ev), lambda b, h, i, s: (b, 2 * heads + h)),
                  pl.BlockSpec((4, hd), lambda b, h, i, s: (0, 0)),
                  pl.BlockSpec((1, ev), lambda b, h, i, s: (0, 0))],
        out_specs=pl.BlockSpec((tq, ev), lambda b, h, i, s: (b * nq + i, h)),
        scratch_shapes=[pltpu.VMEM((2, seq, hd + LANES), BF16)],
    )
    return pl.pallas_call(
        functools.partial(_diff_attn_kernel, tq=tq, hd=hd, lambda_init=lambda_init, eps=DIFF_SUBLN_EPS),
        out_shape=jax.ShapeDtypeStruct((m, heads * ev), BF16),
        grid_spec=grid_spec,
        compiler_params=_params("parallel", "parallel", "arbitrary"),
        name="diff_attention",
    )(slopes, qkv, qkv, qkv, lam.astype(F32), subln_g.reshape(1, ev).astype(F32))


def _ssm_conv_kernel(x_ref, halo_ref, w_ref, b_ref, o_ref):
    i = pl.program_id(1)
    halo = jnp.where(i == 0, 0.0, halo_ref[...].astype(F32))
    u = _causal_conv_rows(x_ref[...].astype(F32), halo, w_ref[...], b_ref[...])
    o_ref[...] = _silu(u).astype(o_ref.dtype)


def ssm_conv_silu(zxbc, col_start, w, b, batch, seq):
    m = zxbc.shape[0]
    cdim = w.shape[1]
    ts, tc = _tile(seq, CONV_TS), _tile(cdim, CONV_TC)
    assert col_start % tc == 0
    cb0 = col_start // tc
    ns = seq // ts
    hb = ts // BF16_SUBLANES
    return pl.pallas_call(
        _ssm_conv_kernel,
        out_shape=jax.ShapeDtypeStruct((m, cdim), BF16),
        grid=(batch, ns, cdim // tc),
        in_specs=[pl.BlockSpec((ts, tc), lambda bi, i, j: (bi * ns + i, cb0 + j)),
                  pl.BlockSpec((BF16_SUBLANES, tc),
                               lambda bi, i, j: (jnp.maximum((bi * ns + i) * hb - 1, 0), cb0 + j)),
                  pl.BlockSpec((w.shape[0], tc), lambda bi, i, j: (0, j)),
                  pl.BlockSpec((1, tc), lambda bi, i, j: (0, j))],
        out_specs=pl.BlockSpec((ts, tc), lambda bi, i, j: (bi * ns + i, j)),
        compiler_params=_params("parallel", "parallel", "parallel"),
        name="ssm_conv_silu",
    )(zxbc, zxbc, w.astype(F32), b.reshape(1, cdim).astype(F32))


def _ssm_dt_kernel(dt_ref, bias_ref, alog_ref, dt_out, la_out):
    x = dt_ref[...] + bias_ref[...]
    dt = jnp.maximum(x, 0.0) + jnp.log(1.0 + jnp.exp(-jnp.abs(x)))
    dt_out[...] = dt
    la_out[...] = _cumsum_rows(dt * (-jnp.exp(alog_ref[...])))


def ssm_dt(dt_raw, dt_bias, a_log):
    m, h = dt_raw.shape
    spec = pl.BlockSpec((SSM_CHUNK, h), lambda i: (i, 0))
    pspec = pl.BlockSpec((1, h), lambda i: (0, 0))
    return pl.pallas_call(
        _ssm_dt_kernel,
        out_shape=(jax.ShapeDtypeStruct((m, h), F32), jax.ShapeDtypeStruct((m, h), F32)),
        grid=(m // SSM_CHUNK,),
        in_specs=[spec, pspec, pspec],
        out_specs=(spec, spec),
        compiler_params=_params("parallel"),
        name="ssm_dt",
    )(dt_raw, dt_bias.reshape(1, h).astype(F32), a_log.reshape(1, h).astype(F32))


def _ssd_scan_kernel(x_ref, b_ref, c_ref, dt_ref, la_ref, lat_ref, z_ref, d_ref, g_ref, o_ref,
                     state_ref, *, hpg, pdim, eps):
    ci = pl.program_id(2)

    @pl.when(ci == 0)
    def _():
        state_ref[...] = jnp.zeros_like(state_ref)

    chunk = x_ref.shape[0]
    npair = hpg // 2
    x = x_ref[...].astype(F32)
    bm = b_ref[...]
    cm = c_ref[...]
    dt = dt_ref[...]
    la = la_ref[...]
    lat = lat_ref[...]
    lane = lax.broadcasted_iota(jnp.int32, (chunk, 2 * pdim), 1)
    first = lane < pdim

    def expand(a):
        lo = lax.broadcasted_iota(jnp.int32, (a.shape[0], 2 * pdim), 1) < pdim
        return jnp.concatenate(
            [jnp.where(lo, a[:, 2 * p:2 * p + 1], a[:, 2 * p + 1:2 * p + 2]) for p in range(npair)], axis=1)

    dt_e = expand(dt)
    la_e = expand(la)
    la_end_e = expand(la[chunk - 1:chunk])
    cb = _qk(cm, bm)
    tri = (lax.broadcasted_iota(jnp.int32, (chunk, chunk), 0)
           >= lax.broadcasted_iota(jnp.int32, (chunk, chunk), 1))
    xdt = x * dt_e
    y_parts = []
    for p in range(npair):
        ms = []
        for hh in (2 * p, 2 * p + 1):
            seg = la[:, hh:hh + 1] - lat[hh:hh + 1, :]
            ms.append((cb * jnp.exp(jnp.where(tri, seg, -jnp.inf))).astype(BF16))
        xp = xdt[:, p * 2 * pdim:(p + 1) * 2 * pdim]
        rhs = jnp.concatenate([jnp.where(first, xp, 0.0), jnp.where(first, 0.0, xp)], axis=0).astype(BF16)
        y_parts.append(jnp.dot(jnp.concatenate(ms, axis=1), rhs, preferred_element_type=F32))
    y = jnp.concatenate(y_parts, axis=1)
    state = state_ref[...]
    y = y + jnp.dot(cm, state.astype(BF16), preferred_element_type=F32) * jnp.exp(la_e)
    to_end = jnp.exp(la_end_e - la_e) * dt_e
    upd = lax.dot_general(bm, (x * to_end).astype(BF16), (((0,), (0,)), ((), ())),
                          preferred_element_type=F32)
    state_ref[...] = state * jnp.exp(la_end_e) + upd
    y = y + d_ref[...] * x
    y = y * _silu(z_ref[...].astype(F32))
    y = y * lax.rsqrt(jnp.mean(y * y, axis=-1, keepdims=True) + eps)
    o_ref[...] = (y * g_ref[...]).astype(o_ref.dtype)


def ssd_scan(xbc, z, dt_g, la_g, lat_g, d_e, norm_g, batch, seq, d_inner, heads):
    m = xbc.shape[0]
    groups, nstate, chunk = SSM_GROUPS, SSM_STATE, SSM_CHUNK
    hpg = heads // groups
    pdim = d_inner // heads
    gw = hpg * pdim
    nc = seq // chunk
    xb = d_inner // nstate
    row = lambda b, g, c: b * nc + c
    return pl.pallas_call(
        functools.partial(_ssd_scan_kernel, hpg=hpg, pdim=pdim, eps=SSM_NORM_EPS),
        out_shape=jax.ShapeDtypeStruct((m, d_inner), BF16),
        grid=(batch, groups, nc),
        in_specs=[pl.BlockSpec((chunk, gw), lambda b, g, c: (row(b, g, c), g)),
                  pl.BlockSpec((chunk, nstate), lambda b, g, c: (row(b, g, c), xb + g)),
                  pl.BlockSpec((chunk, nstate), lambda b, g, c: (row(b, g, c), xb + groups + g)),
                  pl.BlockSpec((None, chunk, hpg), lambda b, g, c: (g, row(b, g, c), 0)),
                  pl.BlockSpec((None, chunk, hpg), lambda b, g, c: (g, row(b, g, c), 0)),
                  pl.BlockSpec((None, hpg, chunk), lambda b, g, c: (g, 0, row(b, g, c))),
                  pl.BlockSpec((chunk, gw), lambda b, g, c: (row(b, g, c), g)),
                  pl.BlockSpec((1, gw), lambda b, g, c: (0, g)),
                  pl.BlockSpec((1, gw), lambda b, g, c: (0, g))],
        out_specs=pl.BlockSpec((chunk, gw), lambda b, g, c: (row(b, g, c), g)),
        scratch_shapes=[pltpu.VMEM((nstate, gw), F32)],
        compiler_params=_params("parallel", "parallel", "arbitrary"),
        name="ssd_scan",
    )(xbc, xbc, xbc, dt_g, la_g, lat_g, z, d_e, norm_g)


def _q_colscale(ncols, nq, hd):
    return jnp.where(jnp.arange(ncols) < nq, (hd ** -0.5) * LOG2E, 1.0).astype(F32)


def in_proj_cast(kind, layer, d, fox_w_in, fox_heads, ssm_w_in, diff_w_in, diff_hd):
    if kind == 0:
        return SideCast(fox_w_in, layer, d, _q_colscale(fox_w_in.shape[2], d, d // fox_heads))
    if kind == 1:
        return SideCast(ssm_w_in, layer, d, None)
    return SideCast(diff_w_in, layer, d, _q_colscale(diff_w_in.shape[2], d, diff_hd))


def fox_mixer(h, hn, layer, w_in, b_f, w_o, batch, seq, side):
    d = hn.shape[1]
    heads = b_f.shape[0]
    hd = d // heads
    qkv, side_out = matmul(hn, w_in, out_dtype=BF16, side=side, ncols=3 * d)
    f_logit = matmul(hn, w_in, col_start=3 * d, ncols=heads)
    hp = f_logit.shape[1]
    b_pad = jnp.pad(b_f.reshape(1, heads).astype(F32), ((0, 0), (0, hp - heads)))
    cum = fox_gate_cumsum(f_logit, b_pad, seq)
    attn = fox_attention(qkv, cum, batch, seq, heads, hd)
    return matmul(attn, cast_layer(w_o, layer), res=h), side_out


def ssd_mixer(h, hn, layer, w_in, conv_w, conv_b, dt_bias, a_log, d_skip, norm_g, w_out, batch, seq, side):
    heads = a_log.shape[0]
    d_inner = w_out.shape[1]
    conv_dim = conv_w.shape[1]
    groups = SSM_GROUPS
    hpg = heads // groups
    m = hn.shape[0]
    zxbc, side_out = matmul(hn, w_in, out_dtype=BF16, side=side, ncols=d_inner + conv_dim)
    dt_raw = matmul(hn, w_in, col_start=d_inner + conv_dim, ncols=heads)[:, :heads]
    xbc = ssm_conv_silu(zxbc, d_inner, conv_w, conv_b, batch, seq)
    dt, la = ssm_dt(dt_raw, dt_bias, a_log)
    dt_g = jnp.transpose(dt.reshape(m, groups, hpg), (1, 0, 2))
    la_g = jnp.transpose(la.reshape(m, groups, hpg), (1, 0, 2))
    lat_g = jnp.transpose(la_g, (0, 2, 1))
    pdim = d_inner // heads
    d_e = jnp.repeat(d_skip.astype(F32), pdim).reshape(1, d_inner)
    y = ssd_scan(xbc, zxbc, dt_g, la_g, lat_g, d_e, norm_g.reshape(1, d_inner).astype(F32),
                 batch, seq, d_inner, heads)
    return matmul(y, cast_layer(w_out, layer), res=h), side_out


def diff_mixer(h, hn, layer, w_in, lam, subln_g, w_o, lambda_init, batch, seq, side):
    d = hn.shape[1]
    hd = lam.shape[1]
    heads = d // (2 * hd)
    qkv, side_out = matmul(hn, w_in, out_dtype=BF16, side=side)
    attn = diff_attention(qkv, lam, subln_g, batch, seq, heads, hd, lambda_init)
    return matmul(attn, cast_layer(w_o, layer), res=h), side_out


def conv_ffn(h, hn, layer, wu, conv_w, conv_b, w_down, seq, side):
    f = w_down.shape[1]
    fp = _round_up(f, FFN_TN)
    cw = conv_w.astype(F32)
    cb = conv_b.reshape(1, 2 * f).astype(F32)
    pad = lambda a: jnp.pad(a, ((0, 0), (0, fp - f)))
    act, wd = ffn_up(hn, wu, pad(cw[:, :f]), pad(cw[:, f:]), pad(cb[:, :f]), pad(cb[:, f:]), seq,
                     SideCast(w_down, layer, fp, None))
    if side is None:
        return matmul(act, wd, res=h), None
    return matmul(act, wd, res=h, side=side)


def kernel(x, mix_norm_g, ffn_norm_g, fox_w_in, fox_b_f, fox_w_o, ssm_w_in, ssm_conv_w, ssm_conv_b,
           ssm_dt_bias, ssm_a_log, ssm_d, ssm_norm_g, ssm_w_out, diff_w_in, diff_lambda, diff_subln_g,
           diff_w_o, ffn_w_up, ffn_conv_w, ffn_conv_b, ffn_w_down, final_norm_g):
    batch, seq, d = x.shape
    depth = mix_norm_g.shape[0]
    h = x.reshape(batch * seq, d)

    def in_cast(i):
        return in_proj_cast(i % N_MIXERS, i // N_MIXERS, d, fox_w_in, fox_b_f.shape[1], ssm_w_in,
                            diff_w_in, diff_lambda.shape[2])

    w_in = cast_rows(in_cast(0))
    for i in range(depth):
        kind, j = i % N_MIXERS, i // N_MIXERS
        hn = rmsnorm(h, mix_norm_g[i], NORM_EPS, BF16)
        up = SideCast(ffn_w_up, i, ffn_w_up.shape[1], None)
        if kind == 0:
            h, wu = fox_mixer(h, hn, j, w_in, fox_b_f[j], fox_w_o, batch, seq, up)
        elif kind == 1:
            h, wu = ssd_mixer(h, hn, j, w_in, ssm_conv_w[j], ssm_conv_b[j], ssm_dt_bias[j],
                              ssm_a_log[j], ssm_d[j], ssm_norm_g[j], ssm_w_out, batch, seq, up)
        else:
            lambda_init = 0.8 - 0.6 * math.exp(-0.3 * i)
            h, wu = diff_mixer(h, hn, j, w_in, diff_lambda[j], diff_subln_g[j], diff_w_o,
                               lambda_init, batch, seq, up)
        hn = rmsnorm(h, ffn_norm_g[i], NORM_EPS, BF16)
        h, w_in = conv_ffn(h, hn, i, wu, ffn_conv_w[i], ffn_conv_b[i], ffn_w_down, seq,
                           in_cast(i + 1) if i + 1 < depth else None)
    return rmsnorm(h, final_norm_g, NORM_EPS, F32).reshape(batch, seq, d)
```

```python
import collections
import functools
import math

import jax
import jax.numpy as jnp
from jax import lax
from jax.experimental import pallas as pl
from jax.experimental.pallas import tpu as pltpu

F32 = jnp.float32
BF16 = jnp.bfloat16
LOG2E = 1.4426950408889634

V7X_VMEM_LIMIT_BYTES = 56 * 1024 * 1024
LANES = 128
BF16_SUBLANES = 16

NORM_EPS = 1e-6
SSM_NORM_EPS = 1e-5
DIFF_SUBLN_EPS = 1e-5
SSM_GROUPS = 8
SSM_STATE = 128
SSM_CHUNK = 128
N_MIXERS = 3

MM_TM = 1024
MM_TN = 1024
MM_TK_MAX = 4096
MM_TK_SPLIT = 3072
FFN_TM = 1024
FFN_TN = 512
ATTN_TQ = 1024
FOX_HEADS_PER_STEP = 2
NORM_TM = 256
CONV_TS = 512
CONV_TC = 1024
CAST_STEPS = 64


def _tile(dim, pref):
    if dim <= pref:
        return dim
    t = pref
    while dim % t:
        t //= 2
    return t


def _round_up(x, m):
    return -(-x // m) * m


def _params(*sem):
    return pltpu.CompilerParams(dimension_semantics=sem, vmem_limit_bytes=V7X_VMEM_LIMIT_BYTES)


SideCast = collections.namedtuple("SideCast", "src layer out_rows colscale")


def _side_setup(side, nsteps, step_of):
    _, rows, cols = side.src.shape
    r = BF16_SUBLANES
    while side.out_rows % r or side.out_rows // r > nsteps:
        r += BF16_SUBLANES
        assert r <= side.out_rows
    nblocks = side.out_rows // r
    last_src = (rows - 1) // r
    blk = lambda *idx: jnp.minimum(step_of(*idx), nblocks - 1)
    in_specs = [pl.BlockSpec((None, r, cols), lambda *idx: (side.layer, jnp.minimum(blk(*idx), last_src), 0))]
    args = [side.src]
    if side.colscale is not None:
        in_specs.append(pl.BlockSpec((1, cols), lambda *idx: (0, 0)))
        args.append(side.colscale.reshape(1, cols).astype(F32))
    return dict(in_specs=in_specs, args=args, out_spec=pl.BlockSpec((r, cols), lambda *idx: (blk(*idx), 0)),
                out_shape=jax.ShapeDtypeStruct((side.out_rows, cols), BF16),
                static=(nblocks, rows, side.colscale is not None))


def _side_cast(in_refs, dst_ref, step, static):
    nblocks, rows, has_scale = static
    w = in_refs[0][...]
    if has_scale:
        w = w * in_refs[1][...]
    r = dst_ref.shape[0]
    if nblocks * r != rows:
        row = lax.broadcasted_iota(jnp.int32, w.shape, 0) + jnp.minimum(step, nblocks - 1) * r
        w = jnp.where(row < rows, w, 0.0)
    dst_ref[...] = w.astype(dst_ref.dtype)


def _cast_rows_kernel(*refs, static):
    _side_cast(refs[:-1], refs[-1], pl.program_id(0), static)


def cast_rows(side):
    setup = _side_setup(side, CAST_STEPS, lambda i: i)
    return pl.pallas_call(
        functools.partial(_cast_rows_kernel, static=setup["static"]),
        out_shape=setup["out_shape"],
        grid=(setup["static"][0],),
        in_specs=setup["in_specs"],
        out_specs=setup["out_spec"],
        compiler_params=_params("parallel"),
        name="cast_rows",
    )(*setup["args"])


def cast_layer(w, layer):
    return cast_rows(SideCast(w, layer, w.shape[1], None))


def _rmsnorm_kernel(x_ref, g_ref, o_ref, *, eps):
    x = x_ref[...]
    ms = jnp.mean(x * x, axis=-1, keepdims=True)
    o_ref[...] = ((x * lax.rsqrt(ms + eps)) * g_ref[...]).astype(o_ref.dtype)


def rmsnorm(x, g, eps, out_dtype):
    m, d = x.shape
    tm = _tile(m, NORM_TM)
    return pl.pallas_call(
        functools.partial(_rmsnorm_kernel, eps=eps),
        out_shape=jax.ShapeDtypeStruct((m, d), out_dtype),
        grid=(m // tm,),
        in_specs=[pl.BlockSpec((tm, d), lambda i: (i, 0)),
                  pl.BlockSpec((1, d), lambda i: (0, 0))],
        out_specs=pl.BlockSpec((tm, d), lambda i: (i, 0)),
        compiler_params=_params("parallel"),
        name="rmsnorm",
    )(x, g.reshape(1, d).astype(F32))


def _mm_kernel(*refs, nk, has_res, side, ncols):
    n_reg = 3 if has_res else 2
    n_side = 0
    if side is not None:
        n_side = 2 if side[2] else 1
        step = (pl.program_id(0) * pl.num_programs(1) + pl.program_id(1)) * nk + pl.program_id(2)
        _side_cast(refs[n_reg:n_reg + n_side], refs[n_reg + n_side + 1], step, side)
    x_ref, w_ref = refs[:2]
    r_ref = refs[2] if has_res else None
    o_ref = refs[n_reg + n_side]
    w = w_ref[...]
    tn = w.shape[1]
    if ncols % tn:
        col = lax.broadcasted_iota(jnp.int32, w.shape, 1) + pl.program_id(1) * tn
        w = jnp.where(col < ncols, w, jnp.zeros_like(w))
    part = jnp.dot(x_ref[...], w, preferred_element_type=F32)

    def finish(acc):
        if has_res:
            acc = acc + r_ref[...]
        o_ref[...] = acc.astype(o_ref.dtype)

    if nk == 1:
        finish(part)
        return
    acc_ref = refs[-1]
    k = pl.program_id(2)

    @pl.when(k == 0)
    def _():
        acc_ref[...] = part

    @pl.when(jnp.logical_and(k > 0, k < nk - 1))
    def _():
        acc_ref[...] += part

    @pl.when(k == nk - 1)
    def _():
        finish(acc_ref[...] + part)


def matmul(x, w, res=None, out_dtype=F32, side=None, col_start=0, ncols=None):
    m, kdim = x.shape
    ncols = w.shape[1] - col_start if ncols is None else ncols
    n = _round_up(ncols, LANES)
    tm, tn = _tile(m, MM_TM), _tile(n, MM_TN)
    assert col_start % tn == 0
    cb0 = col_start // tn
    tk = kdim
    if kdim > MM_TK_MAX:
        tk = MM_TK_SPLIT
        while kdim % tk:
            tk -= 2 * LANES
    nk = kdim // tk
    in_specs = [pl.BlockSpec((tm, tk), lambda i, j, k: (i, k)),
                pl.BlockSpec((tk, tn), lambda i, j, k: (k, cb0 + j))]
    args = [x, w]
    if res is not None:
        in_specs.append(pl.BlockSpec((tm, tn), lambda i, j, k: (i, j)))
        args.append(res)
    grid = (m // tm, n // tn, nk)
    out_shape = jax.ShapeDtypeStruct((m, n), out_dtype)
    out_specs = pl.BlockSpec((tm, tn), lambda i, j, k: (i, j))
    side_static = None
    if side is not None:
        nj = grid[1]
        setup = _side_setup(side, math.prod(grid), lambda i, j, k: (i * nj + j) * nk + k)
        in_specs += setup["in_specs"]
        args += setup["args"]
        out_shape, out_specs = (out_shape, setup["out_shape"]), (out_specs, setup["out_spec"])
        side_static = setup["static"]
    return pl.pallas_call(
        functools.partial(_mm_kernel, nk=nk, has_res=res is not None, side=side_static, ncols=ncols),
        out_shape=out_shape,
        grid=grid,
        in_specs=in_specs,
        out_specs=out_specs,
        scratch_shapes=[pltpu.VMEM((tm, tn), F32)] if nk > 1 else [],
        compiler_params=_params("arbitrary", "arbitrary", "arbitrary"),
        name="matmul",
    )(*args)


def _causal_conv_rows(p, halo, w, b):
    kw = w.shape[0]
    hr = halo.shape[0]
    top = p[0:hr]
    u = b + w[kw - 1:kw] * p
    ut = b + w[kw - 1:kw] * top
    row = lax.broadcasted_iota(jnp.int32, top.shape, 0)
    for k in range(1, kw):
        wk = w[kw - 1 - k:kw - k]
        u = u + wk * pltpu.roll(p, k, 0)
        shifted_top = jnp.where(row < k, pltpu.roll(halo, k, 0), pltpu.roll(top, k, 0))
        ut = ut + wk * shifted_top
    return jnp.concatenate([ut, u[hr:]], axis=0)


def _silu(x):
    return x / (1.0 + jnp.exp(-x))


def _ffn_up_kernel(x_ref, wgl_ref, wgh_ref, wvl_ref, wvh_ref, cwg_ref, cwv_ref, cbg_ref, cbv_ref,
                   side_src, o_ref, side_dst, halo_g, halo_v, *, tiles_per_seq, side, last_is_half):
    j = pl.program_id(0)
    i = pl.program_id(1)
    _side_cast((side_src,), side_dst, j * pl.num_programs(1) + i, side)

    @pl.when(i % tiles_per_seq == 0)
    def _():
        halo_g[...] = jnp.zeros_like(halo_g)
        halo_v[...] = jnp.zeros_like(halo_v)

    tn = o_ref.shape[1]

    def tile(cols):
        x = x_ref[...]
        if cols == tn:
            wg = jnp.concatenate([wgl_ref[...], wgh_ref[...]], axis=1)
            wv = jnp.concatenate([wvl_ref[...], wvh_ref[...]], axis=1)
        else:
            wg, wv = wgl_ref[...], wvl_ref[...]
            o_ref[:, cols:] = jnp.zeros((o_ref.shape[0], tn - cols), o_ref.dtype)
        pg = jnp.dot(x, wg, preferred_element_type=F32)
        pv = jnp.dot(x, wv, preferred_element_type=F32)
        ug = _causal_conv_rows(pg, halo_g[:, :cols], cwg_ref[:, :cols], cbg_ref[:, :cols])
        uv = _causal_conv_rows(pv, halo_v[:, :cols], cwv_ref[:, :cols], cbv_ref[:, :cols])
        hr = halo_g.shape[0]
        halo_g[:, :cols] = pg[pg.shape[0] - hr:]
        halo_v[:, :cols] = pv[pv.shape[0] - hr:]
        o_ref[:, :cols] = (_silu(ug) * uv).astype(o_ref.dtype)

    if not last_is_half:
        tile(tn)
        return
    last = pl.num_programs(0) - 1

    @pl.when(j < last)
    def _():
        tile(tn)

    @pl.when(j == last)
    def _():
        tile(tn // 2)


def ffn_up(x, w_up, cwg, cwv, cbg, cbv, seq, side):
    m, d = x.shape
    f = w_up.shape[1] // 2
    fp = cwg.shape[1]
    tm, tn = _tile(seq, FFN_TM), _tile(fp, FFN_TN)
    half = tn // 2
    assert f % half == 0
    nb, last = f // half, 2 * f // half - 1
    kw = cwg.shape[0]
    wspec = lambda off: pl.BlockSpec((d, half), lambda j, i: (0, jnp.minimum(2 * j + off, last)))
    cspec = pl.BlockSpec((kw, tn), lambda j, i: (0, j))
    bspec = pl.BlockSpec((1, tn), lambda j, i: (0, j))
    nj, ni = fp // tn, m // tm
    assert side.colscale is None
    setup = _side_setup(side, nj * ni, lambda j, i: j * ni + i)
    return pl.pallas_call(
        functools.partial(_ffn_up_kernel, tiles_per_seq=seq // tm, side=setup["static"],
                          last_is_half=fp - f == half),
        out_shape=(jax.ShapeDtypeStruct((m, fp), BF16), setup["out_shape"]),
        grid=(nj, ni),
        in_specs=[pl.BlockSpec((tm, d), lambda j, i: (i, 0)),
                  wspec(0), wspec(1), wspec(nb), wspec(nb + 1), cspec, cspec, bspec, bspec] + setup["in_specs"],
        out_specs=(pl.BlockSpec((tm, tn), lambda j, i: (i, j)), setup["out_spec"]),
        scratch_shapes=[pltpu.VMEM((BF16_SUBLANES, tn), F32), pltpu.VMEM((BF16_SUBLANES, tn), F32)],
        compiler_params=_params("arbitrary", "arbitrary"),
        name="ffn_up",
    )(x, w_up, w_up, w_up, w_up, cwg, cwv, cbg, cbv, side.src)


def _cumsum_rows(y):
    n = y.shape[0]
    row = lax.broadcasted_iota(jnp.int32, y.shape, 0)
    shift = 1
    while shift < n:
        y = y + jnp.where(row >= shift, pltpu.roll(y, shift, 0), 0.0)
        shift *= 2
    return y


def _fox_gate_kernel(f_ref, b_ref, o_ref):
    x = f_ref[...] + b_ref[...]
    log_f = jnp.minimum(x, 0.0) - jnp.log(1.0 + jnp.exp(-jnp.abs(x)))
    o_ref[...] = _cumsum_rows(log_f)


def fox_gate_cumsum(f_logit, b_f, seq):
    m, hp = f_logit.shape
    return pl.pallas_call(
        _fox_gate_kernel,
        out_shape=jax.ShapeDtypeStruct((m, hp), F32),
        grid=(m // seq,),
        in_specs=[pl.BlockSpec((seq, hp), lambda b: (b, 0)), pl.BlockSpec((1, hp), lambda b: (0, 0))],
        out_specs=pl.BlockSpec((seq, hp), lambda b: (b, 0)),
        compiler_params=_params("parallel"),
        name="fox_gate_cumsum",
    )(f_logit, b_f)


def _bias_lanes(x, pieces_first):
    hi = x.astype(BF16).astype(F32)
    rem = x - hi
    lo = rem.astype(BF16).astype(F32)
    lo2 = rem - lo
    lane = lax.broadcasted_iota(jnp.int32, (x.shape[0], LANES), 1)
    base = 0 if pieces_first else 3
    ones = 3 if pieces_first else 0
    out = jnp.where(lane == base, hi, jnp.where(lane == base + 1, lo, jnp.where(lane == base + 2, lo2, 0.0)))
    out = jnp.where(jnp.logical_and(lane >= ones, lane < ones + 3), 1.0, out)
    return out.astype(BF16)


def _qk(q, k):
    return lax.dot_general(q, k, (((1,), (1,)), ((), ())), preferred_element_type=F32)


def _online_softmax_step(s, v, carry):
    m, l, acc = carry
    m_new = jnp.maximum(m, jnp.max(s, axis=-1, keepdims=True))
    alpha = jnp.exp2(m - m_new)
    p = jnp.exp2(s - m_new)
    l = alpha * l + jnp.sum(p, axis=-1, keepdims=True)
    acc = alpha * acc + jnp.dot(p.astype(v.dtype), v, preferred_element_type=F32)
    return m_new, l, acc


def _softmax_init(tq, ev):
    return (jnp.full((tq, 1), -jnp.inf, F32), jnp.zeros((tq, 1), F32), jnp.zeros((tq, ev), F32))


def _causal_mask(s):
    r = lax.broadcasted_iota(jnp.int32, s.shape, 0)
    c = lax.broadcasted_iota(jnp.int32, s.shape, 1)
    return jnp.where(c <= r, s, -jnp.inf)


def _fox_attn_kernel(q_ref, k_ref, v_ref, cum_ref, o_ref, kaug_ref, *, tq, hd):
    hg = pl.program_id(1)
    qi = pl.program_id(2)
    nh = k_ref.shape[-1] // hd
    nq = k_ref.shape[0] // tq

    def gate(rows, a):
        lane = lax.broadcasted_iota(jnp.int32, rows.shape, 1)
        return jnp.sum(jnp.where(lane == hg * nh + a, rows, 0.0), axis=-1, keepdims=True) * LOG2E

    @pl.when(qi == 0)
    def _():
        for c in range(nq):
            sl = slice(c * tq, (c + 1) * tq)
            rows = cum_ref[sl, :]
            for a in range(nh):
                kaug_ref[a, sl, :hd] = k_ref[sl, a * hd:(a + 1) * hd]
                kaug_ref[a, sl, hd:] = _bias_lanes(-gate(rows, a), True)

    q_start = pl.multiple_of(qi * tq, tq)
    q_rows = cum_ref[pl.ds(q_start, tq), :]
    q = q_ref[...]
    qs = [jnp.concatenate([q[:, a * hd:(a + 1) * hd], _bias_lanes(gate(q_rows, a), False)], axis=1)
          for a in range(nh)]

    def tile(j, a):
        start = pl.multiple_of(j * tq, tq)
        return _qk(qs[a], kaug_ref[a, pl.ds(start, tq), :]), v_ref[pl.ds(start, tq), a * hd:(a + 1) * hd]

    def body(j, carry):
        return tuple(_online_softmax_step(*tile(j, a), carry[a]) for a in range(nh))

    carry = lax.fori_loop(0, qi, body, (_softmax_init(tq, hd),) * nh)
    for a in range(nh):
        s, v = tile(qi, a)
        _, l, acc = _online_softmax_step(_causal_mask(s), v, carry[a])
        o_ref[:, a * hd:(a + 1) * hd] = (acc / l).astype(o_ref.dtype)


def fox_attention(qkv, cum, batch, seq, heads, hd):
    assert hd == LANES
    m = qkv.shape[0]
    tq = _tile(seq, ATTN_TQ)
    nq = seq // tq
    hp = cum.shape[1]
    nh = FOX_HEADS_PER_STEP
    hg = heads // nh
    w = nh * hd
    return pl.pallas_call(
        functools.partial(_fox_attn_kernel, tq=tq, hd=hd),
        out_shape=jax.ShapeDtypeStruct((m, heads * hd), BF16),
        grid=(batch, hg, nq),
        in_specs=[pl.BlockSpec((tq, w), lambda b, h, i: (b * nq + i, h)),
                  pl.BlockSpec((seq, w), lambda b, h, i: (b, hg + h)),
                  pl.BlockSpec((seq, w), lambda b, h, i: (b, 2 * hg + h)),
                  pl.BlockSpec((seq, hp), lambda b, h, i: (b, 0))],
        out_specs=pl.BlockSpec((tq, w), lambda b, h, i: (b * nq + i, h)),
        scratch_shapes=[pltpu.VMEM((nh, seq, hd + LANES), BF16)],
        compiler_params=_params("parallel", "parallel", "arbitrary"),
        name="fox_attention",
    )(qkv, qkv, qkv, cum)


def _diff_attn_kernel(slope_ref, q_ref, k_ref, v_ref, lam_ref, g_ref, o_ref, kaug_ref, *,
                      tq, hd, lambda_init, eps):
    h = pl.program_id(1)
    qi = pl.program_id(2)
    slope2 = slope_ref[h] * LOG2E
    nq = k_ref.shape[0] // tq

    def alibi(start):
        return slope2 * (lax.broadcasted_iota(jnp.int32, (tq, 1), 0) + start).astype(F32)

    @pl.when(qi == 0)
    def _():
        for c in range(nq):
            sl = slice(c * tq, (c + 1) * tq)
            kx = _bias_lanes(alibi(c * tq), True)
            for a in range(2):
                kaug_ref[a, sl, :hd] = k_ref[sl, a * hd:(a + 1) * hd]
                kaug_ref[a, sl, hd:] = kx

    qx = _bias_lanes(-alibi(qi * tq), False)
    q = q_ref[...]
    qs = [jnp.concatenate([q[:, a * hd:(a + 1) * hd], qx], axis=1) for a in range(2)]

    ev = v_ref.shape[-1]

    def tile(j):
        start = pl.multiple_of(j * tq, tq)
        return [_qk(qs[a], kaug_ref[a, pl.ds(start, tq), :]) for a in range(2)], v_ref[pl.ds(start, tq), :]

    def body(j, carry):
        ss, v = tile(j)
        return tuple(_online_softmax_step(ss[a], v, carry[a]) for a in range(2))

    one = _softmax_init(tq, ev)
    carry = lax.fori_loop(0, qi, body, (one, one))
    ss, v = tile(qi)
    outs = []
    for a in range(2):
        _, l, acc = _online_softmax_step(_causal_mask(ss[a]), v, carry[a])
        outs.append(acc / l)
    lam = lam_ref[...]
    lam_full = (jnp.exp(jnp.sum(lam[0:1] * lam[1:2], axis=-1, keepdims=True))
                - jnp.exp(jnp.sum(lam[2:3] * lam[3:4], axis=-1, keepdims=True)) + lambda_init)
    out = outs[0] - lam_full * outs[1]
    ms = jnp.mean(out * out, axis=-1, keepdims=True)
    y = (out * lax.rsqrt(ms + eps)) * g_ref[...]
    o_ref[...] = (y * (1.0 - lambda_init)).astype(o_ref.dtype)


def diff_attention(qkv, lam, subln_g, batch, seq, heads, hd, lambda_init):
    assert hd == LANES
    m = qkv.shape[0]
    tq = _tile(seq, ATTN_TQ)
    nq = seq // tq
    ev = 2 * hd
    slopes = jnp.exp2(-8.0 * jnp.arange(1, heads + 1, dtype=F32) / heads)
    grid_spec = pltpu.PrefetchScalarGridSpec(
        num_scalar_prefetch=1,
        grid=(batch, heads, nq),
        in_specs=[pl.BlockSpec((tq, ev), lambda b, h, i, s: (b * nq + i, h)),
                  pl.BlockSpec((seq, ev), lambda b, h, i, s: (b, heads + h)),
                  pl.BlockSpec((seq, ev), lambda b, h, i, s: (b, 2 * heads + h)),
                  pl.BlockSpec((4, hd), lambda b, h, i, s: (0, 0)),
                  pl.BlockSpec((1, ev), lambda b, h, i, s: (0, 0))],
        out_specs=pl.BlockSpec((tq, ev), lambda b, h, i, s: (b * nq + i, h)),
        scratch_shapes=[pltpu.VMEM((2, seq, hd + LANES), BF16)],
    )
    return pl.pallas_call(
        functools.partial(_diff_attn_kernel, tq=tq, hd=hd, lambda_init=lambda_init, eps=DIFF_SUBLN_EPS),
        out_shape=jax.ShapeDtypeStruct((m, heads * ev), BF16),
        grid_spec=grid_spec,
        compiler_params=_params("parallel", "parallel", "arbitrary"),
        name="diff_attention",
    )(slopes, qkv, qkv, qkv, lam.astype(F32), subln_g.reshape(1, ev).astype(F32))


def _ssm_conv_kernel(x_ref, halo_ref, w_ref, b_ref, o_ref):
    i = pl.program_id(1)
    halo = jnp.where(i == 0, 0.0, halo_ref[...].astype(F32))
    u = _causal_conv_rows(x_ref[...].astype(F32), halo, w_ref[...], b_ref[...])
    o_ref[...] = _silu(u).astype(o_ref.dtype)


def ssm_conv_silu(zxbc, col_start, w, b, batch, seq):
    m = zxbc.shape[0]
    cdim = w.shape[1]
    ts, tc = _tile(seq, CONV_TS), _tile(cdim, CONV_TC)
    assert col_start % tc == 0
    cb0 = col_start // tc
    ns = seq // ts
    hb = ts // BF16_SUBLANES
    return pl.pallas_call(
        _ssm_conv_kernel,
        out_shape=jax.ShapeDtypeStruct((m, cdim), BF16),
        grid=(batch, ns, cdim // tc),
        in_specs=[pl.BlockSpec((ts, tc), lambda bi, i, j: (bi * ns + i, cb0 + j)),
                  pl.BlockSpec((BF16_SUBLANES, tc),
                               lambda bi, i, j: (jnp.maximum((bi * ns + i) * hb - 1, 0), cb0 + j)),
                  pl.BlockSpec((w.shape[0], tc), lambda bi, i, j: (0, j)),
                  pl.BlockSpec((1, tc), lambda bi, i, j: (0, j))],
        out_specs=pl.BlockSpec((ts, tc), lambda bi, i, j: (bi * ns + i, j)),
        compiler_params=_params("parallel", "parallel", "parallel"),
        name="ssm_conv_silu",
    )(zxbc, zxbc, w.astype(F32), b.reshape(1, cdim).astype(F32))


def _ssm_dt_kernel(dt_ref, bias_ref, alog_ref, dt_out, la_out):
    x = dt_ref[...] + bias_ref[...]
    dt = jnp.maximum(x, 0.0) + jnp.log(1.0 + jnp.exp(-jnp.abs(x)))
    dt_out[...] = dt
    la_out[...] = _cumsum_rows(dt * (-jnp.exp(alog_ref[...])))


def ssm_dt(dt_raw, dt_bias, a_log):
    m, h = dt_raw.shape
    spec = pl.BlockSpec((SSM_CHUNK, h), lambda i: (i, 0))
    pspec = pl.BlockSpec((1, h), lambda i: (0, 0))
    return pl.pallas_call(
        _ssm_dt_kernel,
        out_shape=(jax.ShapeDtypeStruct((m, h), F32), jax.ShapeDtypeStruct((m, h), F32)),
        grid=(m // SSM_CHUNK,),
        in_specs=[spec, pspec, pspec],
        out_specs=(spec, spec),
        compiler_params=_params("parallel"),
        name="ssm_dt",
    )(dt_raw, dt_bias.reshape(1, h).astype(F32), a_log.reshape(1, h).astype(F32))


def _ssd_scan_kernel(x_ref, b_ref, c_ref, dt_ref, la_ref, lat_ref, z_ref, d_ref, g_ref, o_ref,
                     state_ref, *, hpg, pdim, eps):
    ci = pl.program_id(2)

    @pl.when(ci == 0)
    def _():
        state_ref[...] = jnp.zeros_like(state_ref)

    chunk = x_ref.shape[0]
    npair = hpg // 2
    x = x_ref[...].astype(F32)
    bm = b_ref[...]
    cm = c_ref[...]
    dt = dt_ref[...]
    la = la_ref[...]
    lat = lat_ref[...]
    lane = lax.broadcasted_iota(jnp.int32, (chunk, 2 * pdim), 1)
    first = lane < pdim

    def expand(a):
        lo = lax.broadcasted_iota(jnp.int32, (a.shape[0], 2 * pdim), 1) < pdim
        return jnp.concatenate(
            [jnp.where(lo, a[:, 2 * p:2 * p + 1], a[:, 2 * p + 1:2 * p + 2]) for p in range(npair)], axis=1)

    dt_e = expand(dt)
    la_e = expand(la)
    la_end_e = expand(la[chunk - 1:chunk])
    cb = _qk(cm, bm)
    tri = (lax.broadcasted_iota(jnp.int32, (chunk, chunk), 0)
           >= lax.broadcasted_iota(jnp.int32, (chunk, chunk), 1))
    xdt = x * dt_e
    y_parts = []
    for p in range(npair):
        ms = []
        for hh in (2 * p, 2 * p + 1):
            seg = la[:, hh:hh + 1] - lat[hh:hh + 1, :]
            ms.append((cb * jnp.exp(jnp.where(tri, seg, -jnp.inf))).astype(BF16))
        xp = xdt[:, p * 2 * pdim:(p + 1) * 2 * pdim]
        rhs = jnp.concatenate([jnp.where(first, xp, 0.0), jnp.where(first, 0.0, xp)], axis=0).astype(BF16)
        y_parts.append(jnp.dot(jnp.concatenate(ms, axis=1), rhs, preferred_element_type=F32))
    y = jnp.concatenate(y_parts, axis=1)
    state = state_ref[...]
    y = y + jnp.dot(cm, state.astype(BF16), preferred_element_type=F32) * jnp.exp(la_e)
    to_end = jnp.exp(la_end_e - la_e) * dt_e
    upd = lax.dot_general(bm, (x * to_end).astype(BF16), (((0,), (0,)), ((), ())),
                          preferred_element_type=F32)
    state_ref[...] = state * jnp.exp(la_end_e) + upd
    y = y + d_ref[...] * x
    y = y * _silu(z_ref[...].astype(F32))
    y = y * lax.rsqrt(jnp.mean(y * y, axis=-1, keepdims=True) + eps)
    o_ref[...] = (y * g_ref[...]).astype(o_ref.dtype)


def ssd_scan(xbc, z, dt_g, la_g, lat_g, d_e, norm_g, batch, seq, d_inner, heads):
    m = xbc.shape[0]
    groups, nstate, chunk = SSM_GROUPS, SSM_STATE, SSM_CHUNK
    hpg = heads // groups
    pdim = d_inner // heads
    gw = hpg * pdim
    nc = seq // chunk
    xb = d_inner // nstate
    row = lambda b, g, c: b * nc + c
    return pl.pallas_call(
        functools.partial(_ssd_scan_kernel, hpg=hpg, pdim=pdim, eps=SSM_NORM_EPS),
        out_shape=jax.ShapeDtypeStruct((m, d_inner), BF16),
        grid=(batch, groups, nc),
        in_specs=[pl.BlockSpec((chunk, gw), lambda b, g, c: (row(b, g, c), g)),
                  pl.BlockSpec((chunk, nstate), lambda b, g, c: (row(b, g, c), xb + g)),
                  pl.BlockSpec((chunk, nstate), lambda b, g, c: (row(b, g, c), xb + groups + g)),
                  pl.BlockSpec((None, chunk, hpg), lambda b, g, c: (g, row(b, g, c), 0)),
                  pl.BlockSpec((None, chunk, hpg), lambda b, g, c: (g, row(b, g, c), 0)),
                  pl.BlockSpec((None, hpg, chunk), lambda b, g, c: (g, 0, row(b, g, c))),
                  pl.BlockSpec((chunk, gw), lambda b, g, c: (row(b, g, c), g)),
                  pl.BlockSpec((1, gw), lambda b, g, c: (0, g)),
                  pl.BlockSpec((1, gw), lambda b, g, c: (0, g))],
        out_specs=pl.BlockSpec((chunk, gw), lambda b, g, c: (row(b, g, c), g)),
        scratch_shapes=[pltpu.VMEM((nstate, gw), F32)],
        compiler_params=_params("parallel", "parallel", "arbitrary"),
        name="ssd_scan",
    )(xbc, xbc, xbc, dt_g, la_g, lat_g, z, d_e, norm_g)


def _q_colscale(ncols, nq, hd):
    return jnp.where(jnp.arange(ncols) < nq, (hd ** -0.5) * LOG2E, 1.0).astype(F32)


def in_proj_cast(kind, layer, d, fox_w_in, fox_heads, ssm_w_in, diff_w_in, diff_hd):
    if kind == 0:
        return SideCast(fox_w_in, layer, d, _q_colscale(fox_w_in.shape[2], d, d // fox_heads))
    if kind == 1:
        return SideCast(ssm_w_in, layer, d, None)
    return SideCast(diff_w_in, layer, d, _q_colscale(diff_w_in.shape[2], d, diff_hd))


def fox_mixer(h, hn, layer, w_in, b_f, w_o, batch, seq, side):
    d = hn.shape[1]
    heads = b_f.shape[0]
    hd = d // heads
    qkv, side_out = matmul(hn, w_in, out_dtype=BF16, side=side, ncols=3 * d)
    f_logit = matmul(hn, w_in, col_start=3 * d, ncols=heads)
    hp = f_logit.shape[1]
    b_pad = jnp.pad(b_f.reshape(1, heads).astype(F32), ((0, 0), (0, hp - heads)))
    cum = fox_gate_cumsum(f_logit, b_pad, seq)
    attn = fox_attention(qkv, cum, batch, seq, heads, hd)
    return matmul(attn, cast_layer(w_o, layer), res=h), side_out


def ssd_mixer(h, hn, layer, w_in, conv_w, conv_b, dt_bias, a_log, d_skip, norm_g, w_out, batch, seq, side):
    heads = a_log.shape[0]
    d_inner = w_out.shape[1]
    conv_dim = conv_w.shape[1]
    groups = SSM_GROUPS
    hpg = heads // groups
    m = hn.shape[0]
    zxbc, side_out = matmul(hn, w_in, out_dtype=BF16, side=side, ncols=d_inner + conv_dim)
    dt_raw = matmul(hn, w_in, col_start=d_inner + conv_dim, ncols=heads)[:, :heads]
    xbc = ssm_conv_silu(zxbc, d_inner, conv_w, conv_b, batch, seq)
    dt, la = ssm_dt(dt_raw, dt_bias, a_log)
    dt_g = jnp.transpose(dt.reshape(m, groups, hpg), (1, 0, 2))
    la_g = jnp.transpose(la.reshape(m, groups, hpg), (1, 0, 2))
    lat_g = jnp.transpose(la_g, (0, 2, 1))
    pdim = d_inner // heads
    d_e = jnp.repeat(d_skip.astype(F32), pdim).reshape(1, d_inner)
    y = ssd_scan(xbc, zxbc, dt_g, la_g, lat_g, d_e, norm_g.reshape(1, d_inner).astype(F32),
                 batch, seq, d_inner, heads)
    return matmul(y, cast_layer(w_out, layer), res=h), side_out


def diff_mixer(h, hn, layer, w_in, lam, subln_g, w_o, lambda_init, batch, seq, side):
    d = hn.shape[1]
    hd = lam.shape[1]
    heads = d // (2 * hd)
    qkv, side_out = matmul(hn, w_in, out_dtype=BF16, side=side)
    attn = diff_attention(qkv, lam, subln_g, batch, seq, heads, hd, lambda_init)
    return matmul(attn, cast_layer(w_o, layer), res=h), side_out


def conv_ffn(h, hn, layer, wu, conv_w, conv_b, w_down, seq, side):
    f = w_down.shape[1]
    fp = _round_up(f, FFN_TN)
    cw = conv_w.astype(F32)
    cb = conv_b.reshape(1, 2 * f).astype(F32)
    pad = lambda a: jnp.pad(a, ((0, 0), (0, fp - f)))
    act, wd = ffn_up(hn, wu, pad(cw[:, :f]), pad(cw[:, f:]), pad(cb[:, :f]), pad(cb[:, f:]), seq,
                     SideCast(w_down, layer, fp, None))
    if side is None:
        return matmul(act, wd, res=h), None
    return matmul(act, wd, res=h, side=side)


def kernel(x, mix_norm_g, ffn_norm_g, fox_w_in, fox_b_f, fox_w_o, ssm_w_in, ssm_conv_w, ssm_conv_b,
           ssm_dt_bias, ssm_a_log, ssm_d, ssm_norm_g, ssm_w_out, diff_w_in, diff_lambda, diff_subln_g,
           diff_w_o, ffn_w_up, ffn_conv_w, ffn_conv_b, ffn_w_down, final_norm_g):
    batch, seq, d = x.shape
    depth = mix_norm_g.shape[0]
    h = x.reshape(batch * seq, d)

    def in_cast(i):
        return in_proj_cast(i % N_MIXERS, i // N_MIXERS, d, fox_w_in, fox_b_f.shape[1], ssm_w_in,
                            diff_w_in, diff_lambda.shape[2])

    w_in = cast_rows(in_cast(0))
    for i in range(depth):
        kind, j = i % N_MIXERS, i // N_MIXERS
        hn = rmsnorm(h, mix_norm_g[i], NORM_EPS, BF16)
        up = SideCast(ffn_w_up, i, ffn_w_up.shape[1], None)
        if kind == 0:
            h, wu = fox_mixer(h, hn, j, w_in, fox_b_f[j], fox_w_o, batch, seq, up)
        elif kind == 1:
            h, wu = ssd_mixer(h, hn, j, w_in, ssm_conv_w[j], ssm_conv_b[j], ssm_dt_bias[j],
                              ssm_a_log[j], ssm_d[j], ssm_norm_g[j], ssm_w_out, batch, seq, up)
        else:
            lambda_init = 0.8 - 0.6 * math.exp(-0.3 * i)
            h, wu = diff_mixer(h, hn, j, w_in, diff_lambda[j], diff_subln_g[j], diff_w_o,
                               lambda_init, batch, seq, up)
        hn = rmsnorm(h, ffn_norm_g[i], NORM_EPS, BF16)
        h, w_in = conv_ffn(h, hn, i, wu, ffn_conv_w[i], ffn_conv_b[i], ffn_w_down, seq,
                           in_cast(i + 1) if i + 1 < depth else None)
    return rmsnorm(h, final_norm_g, NORM_EPS, F32).reshape(batch, seq, d)
```

```python
import collections
import functools
import math

import jax
import jax.numpy as jnp
from jax import lax
from jax.experimental import pallas as pl
from jax.experimental.pallas import tpu as pltpu

F32 = jnp.float32
BF16 = jnp.bfloat16
LOG2E = 1.4426950408889634

V7X_VMEM_LIMIT_BYTES = 56 * 1024 * 1024
LANES = 128
BF16_SUBLANES = 16

NORM_EPS = 1e-6
SSM_NORM_EPS = 1e-5
DIFF_SUBLN_EPS = 1e-5
SSM_GROUPS = 8
SSM_STATE = 128
SSM_CHUNK = 128
N_MIXERS = 3

MM_TM = 1024
MM_TN = 1024
MM_TK_MAX = 4096
MM_TK_SPLIT = 3072
FFN_TM = 1024
FFN_TN = 512
ATTN_TQ = 1024
FOX_HEADS_PER_STEP = 2
SSD_GROUPS_PER_STEP = 8
NORM_TM = 256
CONV_TS = 512
CONV_TC = 1024
CAST_STEPS = 64


def _tile(dim, pref):
    if dim <= pref:
        return dim
    t = pref
    while dim % t:
        t //= 2
    return t


def _round_up(x, m):
    return -(-x // m) * m


def _params(*sem):
    return pltpu.CompilerParams(dimension_semantics=sem, vmem_limit_bytes=V7X_VMEM_LIMIT_BYTES)


SideCast = collections.namedtuple("SideCast", "src layer out_rows colscale")


def _side_setup(side, nsteps, step_of):
    _, rows, cols = side.src.shape
    r = BF16_SUBLANES
    while side.out_rows % r or side.out_rows // r > nsteps:
        r += BF16_SUBLANES
        assert r <= side.out_rows
    nblocks = side.out_rows // r
    last_src = (rows - 1) // r
    blk = lambda *idx: jnp.minimum(step_of(*idx), nblocks - 1)
    in_specs = [pl.BlockSpec((None, r, cols), lambda *idx: (side.layer, jnp.minimum(blk(*idx), last_src), 0))]
    args = [side.src]
    if side.colscale is not None:
        in_specs.append(pl.BlockSpec((1, cols), lambda *idx: (0, 0)))
        args.append(side.colscale.reshape(1, cols).astype(F32))
    return dict(in_specs=in_specs, args=args, out_spec=pl.BlockSpec((r, cols), lambda *idx: (blk(*idx), 0)),
                out_shape=jax.ShapeDtypeStruct((side.out_rows, cols), BF16),
                static=(nblocks, rows, side.colscale is not None))


def _side_cast(in_refs, dst_ref, step, static):
    nblocks, rows, has_scale = static
    w = in_refs[0][...]
    if has_scale:
        w = w * in_refs[1][...]
    r = dst_ref.shape[0]
    if nblocks * r != rows:
        row = lax.broadcasted_iota(jnp.int32, w.shape, 0) + jnp.minimum(step, nblocks - 1) * r
        w = jnp.where(row < rows, w, 0.0)
    dst_ref[...] = w.astype(dst_ref.dtype)


def _cast_rows_kernel(*refs, static):
    _side_cast(refs[:-1], refs[-1], pl.program_id(0), static)


def cast_rows(side):
    setup = _side_setup(side, CAST_STEPS, lambda i: i)
    return pl.pallas_call(
        functools.partial(_cast_rows_kernel, static=setup["static"]),
        out_shape=setup["out_shape"],
        grid=(setup["static"][0],),
        in_specs=setup["in_specs"],
        out_specs=setup["out_spec"],
        compiler_params=_params("parallel"),
        name="cast_rows",
    )(*setup["args"])


def _rmsnorm_kernel(x_ref, g_ref, o_ref, *, eps):
    x = x_ref[...]
    ms = jnp.mean(x * x, axis=-1, keepdims=True)
    o_ref[...] = ((x * lax.rsqrt(ms + eps)) * g_ref[...]).astype(o_ref.dtype)


def rmsnorm(x, g, eps, out_dtype):
    m, d = x.shape
    tm = _tile(m, NORM_TM)
    return pl.pallas_call(
        functools.partial(_rmsnorm_kernel, eps=eps),
        out_shape=jax.ShapeDtypeStruct((m, d), out_dtype),
        grid=(m // tm,),
        in_specs=[pl.BlockSpec((tm, d), lambda i: (i, 0)),
                  pl.BlockSpec((1, d), lambda i: (0, 0))],
        out_specs=pl.BlockSpec((tm, d), lambda i: (i, 0)),
        compiler_params=_params("parallel"),
        name="rmsnorm",
    )(x, g.reshape(1, d).astype(F32))


def _mm_kernel(*refs, nk, has_res, sides, ncols):
    n_reg = 3 if has_res else 2
    n_side_in = sum(2 if s[2] else 1 for s in sides)
    step = (pl.program_id(0) * pl.num_programs(1) + pl.program_id(1)) * nk + pl.program_id(2)
    pos = n_reg
    for idx, s in enumerate(sides):
        n_in = 2 if s[2] else 1
        _side_cast(refs[pos:pos + n_in], refs[n_reg + n_side_in + 1 + idx], step, s)
        pos += n_in
    x_ref, w_ref = refs[:2]
    r_ref = refs[2] if has_res else None
    o_ref = refs[n_reg + n_side_in]
    w = w_ref[...]
    tn = w.shape[1]
    if ncols % tn:
        col = lax.broadcasted_iota(jnp.int32, w.shape, 1) + pl.program_id(1) * tn
        w = jnp.where(col < ncols, w, jnp.zeros_like(w))
    part = jnp.dot(x_ref[...], w, preferred_element_type=F32)

    def finish(acc):
        if has_res:
            acc = acc + r_ref[...]
        o_ref[...] = acc.astype(o_ref.dtype)

    if nk == 1:
        finish(part)
        return
    acc_ref = refs[-1]
    k = pl.program_id(2)

    @pl.when(k == 0)
    def _():
        acc_ref[...] = part

    @pl.when(jnp.logical_and(k > 0, k < nk - 1))
    def _():
        acc_ref[...] += part

    @pl.when(k == nk - 1)
    def _():
        finish(acc_ref[...] + part)


def matmul(x, w, res=None, out_dtype=F32, sides=(), col_start=0, ncols=None):
    m, kdim = x.shape
    ncols = w.shape[1] - col_start if ncols is None else ncols
    n = _round_up(ncols, LANES)
    tm, tn = _tile(m, MM_TM), _tile(n, MM_TN)
    assert col_start % tn == 0
    cb0 = col_start // tn
    tk = kdim
    if kdim > MM_TK_MAX:
        tk = MM_TK_SPLIT
        while kdim % tk:
            tk -= 2 * LANES
    nk = kdim // tk
    in_specs = [pl.BlockSpec((tm, tk), lambda i, j, k: (i, k)),
                pl.BlockSpec((tk, tn), lambda i, j, k: (k, cb0 + j))]
    args = [x, w]
    if res is not None:
        in_specs.append(pl.BlockSpec((tm, tn), lambda i, j, k: (i, j)))
        args.append(res)
    grid = (m // tm, n // tn, nk)
    out_shape = jax.ShapeDtypeStruct((m, n), out_dtype)
    out_specs = pl.BlockSpec((tm, tn), lambda i, j, k: (i, j))
    nj = grid[1]
    setups = [_side_setup(s, math.prod(grid), lambda i, j, k: (i * nj + j) * nk + k) for s in sides]
    for setup in setups:
        in_specs += setup["in_specs"]
        args += setup["args"]
    if sides:
        out_shape = (out_shape, *(s["out_shape"] for s in setups))
        out_specs = (out_specs, *(s["out_spec"] for s in setups))
    return pl.pallas_call(
        functools.partial(_mm_kernel, nk=nk, has_res=res is not None,
                          sides=tuple(s["static"] for s in setups), ncols=ncols),
        out_shape=out_shape,
        grid=grid,
        in_specs=in_specs,
        out_specs=out_specs,
        scratch_shapes=[pltpu.VMEM((tm, tn), F32)] if nk > 1 else [],
        compiler_params=_params("arbitrary", "arbitrary", "arbitrary"),
        name="matmul",
    )(*args)


def _causal_conv_rows(p, halo, w, b):
    kw = w.shape[0]
    hr = halo.shape[0]
    top = p[0:hr]
    u = b + w[kw - 1:kw] * p
    ut = b + w[kw - 1:kw] * top
    row = lax.broadcasted_iota(jnp.int32, top.shape, 0)
    for k in range(1, kw):
        wk = w[kw - 1 - k:kw - k]
        u = u + wk * pltpu.roll(p, k, 0)
        shifted_top = jnp.where(row < k, pltpu.roll(halo, k, 0), pltpu.roll(top, k, 0))
        ut = ut + wk * shifted_top
    return jnp.concatenate([ut, u[hr:]], axis=0)


def _silu(x):
    return x / (1.0 + jnp.exp(-x))


def _ffn_up_kernel(x_ref, wgl_ref, wgh_ref, wvl_ref, wvh_ref, cwg_ref, cwv_ref, cbg_ref, cbv_ref,
                   side_src, o_ref, side_dst, halo_g, halo_v, *, tiles_per_seq, side, last_is_half):
    j = pl.program_id(0)
    i = pl.program_id(1)
    _side_cast((side_src,), side_dst, j * pl.num_programs(1) + i, side)

    @pl.when(i % tiles_per_seq == 0)
    def _():
        halo_g[...] = jnp.zeros_like(halo_g)
        halo_v[...] = jnp.zeros_like(halo_v)

    tn = o_ref.shape[1]

    def tile(cols):
        x = x_ref[...]
        if cols == tn:
            wg = jnp.concatenate([wgl_ref[...], wgh_ref[...]], axis=1)
            wv = jnp.concatenate([wvl_ref[...], wvh_ref[...]], axis=1)
        else:
            wg, wv = wgl_ref[...], wvl_ref[...]
            o_ref[:, cols:] = jnp.zeros((o_ref.shape[0], tn - cols), o_ref.dtype)
        pg = jnp.dot(x, wg, preferred_element_type=F32)
        pv = jnp.dot(x, wv, preferred_element_type=F32)
        ug = _causal_conv_rows(pg, halo_g[:, :cols], cwg_ref[:, :cols], cbg_ref[:, :cols])
        uv = _causal_conv_rows(pv, halo_v[:, :cols], cwv_ref[:, :cols], cbv_ref[:, :cols])
        hr = halo_g.shape[0]
        halo_g[:, :cols] = pg[pg.shape[0] - hr:]
        halo_v[:, :cols] = pv[pv.shape[0] - hr:]
        o_ref[:, :cols] = (_silu(ug) * uv).astype(o_ref.dtype)

    if not last_is_half:
        tile(tn)
        return
    last = pl.num_programs(0) - 1

    @pl.when(j < last)
    def _():
        tile(tn)

    @pl.when(j == last)
    def _():
        tile(tn // 2)


def ffn_up(x, w_up, cwg, cwv, cbg, cbv, seq, side):
    m, d = x.shape
    f = w_up.shape[1] // 2
    fp = cwg.shape[1]
    tm, tn = _tile(seq, FFN_TM), _tile(fp, FFN_TN)
    half = tn // 2
    assert f % half == 0
    nb, last = f // half, 2 * f // half - 1
    kw = cwg.shape[0]
    wspec = lambda off: pl.BlockSpec((d, half), lambda j, i: (0, jnp.minimum(2 * j + off, last)))
    cspec = pl.BlockSpec((kw, tn), lambda j, i: (0, j))
    bspec = pl.BlockSpec((1, tn), lambda j, i: (0, j))
    nj, ni = fp // tn, m // tm
    assert side.colscale is None
    setup = _side_setup(side, nj * ni, lambda j, i: j * ni + i)
    return pl.pallas_call(
        functools.partial(_ffn_up_kernel, tiles_per_seq=seq // tm, side=setup["static"],
                          last_is_half=fp - f == half),
        out_shape=(jax.ShapeDtypeStruct((m, fp), BF16), setup["out_shape"]),
        grid=(nj, ni),
        in_specs=[pl.BlockSpec((tm, d), lambda j, i: (i, 0)),
                  wspec(0), wspec(1), wspec(nb), wspec(nb + 1), cspec, cspec, bspec, bspec] + setup["in_specs"],
        out_specs=(pl.BlockSpec((tm, tn), lambda j, i: (i, j)), setup["out_spec"]),
        scratch_shapes=[pltpu.VMEM((BF16_SUBLANES, tn), F32), pltpu.VMEM((BF16_SUBLANES, tn), F32)],
        compiler_params=_params("arbitrary", "arbitrary"),
        name="ffn_up",
    )(x, w_up, w_up, w_up, w_up, cwg, cwv, cbg, cbv, side.src)


def _cumsum_rows(y):
    n = y.shape[0]
    row = lax.broadcasted_iota(jnp.int32, y.shape, 0)
    shift = 1
    while shift < n:
        y = y + jnp.where(row >= shift, pltpu.roll(y, shift, 0), 0.0)
        shift *= 2
    return y


def _fox_gate_kernel(f_ref, b_ref, o_ref):
    x = f_ref[...] + b_ref[...]
    log_f = jnp.minimum(x, 0.0) - jnp.log(1.0 + jnp.exp(-jnp.abs(x)))
    o_ref[...] = _cumsum_rows(log_f)


def fox_gate_cumsum(f_logit, b_f, seq):
    m, hp = f_logit.shape
    return pl.pallas_call(
        _fox_gate_kernel,
        out_shape=jax.ShapeDtypeStruct((m, hp), F32),
        grid=(m // seq,),
        in_specs=[pl.BlockSpec((seq, hp), lambda b: (b, 0)), pl.BlockSpec((1, hp), lambda b: (0, 0))],
        out_specs=pl.BlockSpec((seq, hp), lambda b: (b, 0)),
        compiler_params=_params("parallel"),
        name="fox_gate_cumsum",
    )(f_logit, b_f)


def _bias_lanes(x, pieces_first):
    hi = x.astype(BF16).astype(F32)
    rem = x - hi
    lo = rem.astype(BF16).astype(F32)
    lo2 = rem - lo
    lane = lax.broadcasted_iota(jnp.int32, (x.shape[0], LANES), 1)
    base = 0 if pieces_first else 3
    ones = 3 if pieces_first else 0
    out = jnp.where(lane == base, hi, jnp.where(lane == base + 1, lo, jnp.where(lane == base + 2, lo2, 0.0)))
    out = jnp.where(jnp.logical_and(lane >= ones, lane < ones + 3), 1.0, out)
    return out.astype(BF16)


def _qk(q, k):
    return lax.dot_general(q, k, (((1,), (1,)), ((), ())), preferred_element_type=F32)


def _online_softmax_step(s, v, carry):
    m, l, acc = carry
    m_new = jnp.maximum(m, jnp.max(s, axis=-1, keepdims=True))
    alpha = jnp.exp2(m - m_new)
    p = jnp.exp2(s - m_new)
    l = alpha * l + jnp.sum(p, axis=-1, keepdims=True)
    acc = alpha * acc + jnp.dot(p.astype(v.dtype), v, preferred_element_type=F32)
    return m_new, l, acc


def _softmax_init(tq, ev):
    return (jnp.full((tq, 1), -jnp.inf, F32), jnp.zeros((tq, 1), F32), jnp.zeros((tq, ev), F32))


def _causal_mask(s):
    r = lax.broadcasted_iota(jnp.int32, s.shape, 0)
    c = lax.broadcasted_iota(jnp.int32, s.shape, 1)
    return jnp.where(c <= r, s, -jnp.inf)


def _fox_attn_kernel(q_ref, k_ref, v_ref, cum_ref, o_ref, kaug_ref, *, tq, hd):
    hg = pl.program_id(1)
    qi = pl.program_id(2)
    nh = k_ref.shape[-1] // hd
    nq = k_ref.shape[0] // tq

    def gate(rows, a):
        lane = lax.broadcasted_iota(jnp.int32, rows.shape, 1)
        return jnp.sum(jnp.where(lane == hg * nh + a, rows, 0.0), axis=-1, keepdims=True) * LOG2E

    @pl.when(qi == 0)
    def _():
        for c in range(nq):
            sl = slice(c * tq, (c + 1) * tq)
            rows = cum_ref[sl, :]
            for a in range(nh):
                kaug_ref[a, sl, :hd] = k_ref[sl, a * hd:(a + 1) * hd]
                kaug_ref[a, sl, hd:] = _bias_lanes(-gate(rows, a), True)

    q_start = pl.multiple_of(qi * tq, tq)
    q_rows = cum_ref[pl.ds(q_start, tq), :]
    q = q_ref[...]
    qs = [jnp.concatenate([q[:, a * hd:(a + 1) * hd], _bias_lanes(gate(q_rows, a), False)], axis=1)
          for a in range(nh)]

    def tile(j, a):
        start = pl.multiple_of(j * tq, tq)
        return _qk(qs[a], kaug_ref[a, pl.ds(start, tq), :]), v_ref[pl.ds(start, tq), a * hd:(a + 1) * hd]

    def body(j, carry):
        return tuple(_online_softmax_step(*tile(j, a), carry[a]) for a in range(nh))

    carry = lax.fori_loop(0, qi, body, (_softmax_init(tq, hd),) * nh)
    for a in range(nh):
        s, v = tile(qi, a)
        _, l, acc = _online_softmax_step(_causal_mask(s), v, carry[a])
        o_ref[:, a * hd:(a + 1) * hd] = (acc / l).astype(o_ref.dtype)


def fox_attention(qkv, cum, batch, seq, heads, hd):
    assert hd == LANES
    m = qkv.shape[0]
    tq = _tile(seq, ATTN_TQ)
    nq = seq // tq
    hp = cum.shape[1]
    nh = FOX_HEADS_PER_STEP
    hg = heads // nh
    w = nh * hd
    return pl.pallas_call(
        functools.partial(_fox_attn_kernel, tq=tq, hd=hd),
        out_shape=jax.ShapeDtypeStruct((m, heads * hd), BF16),
        grid=(batch, hg, nq),
        in_specs=[pl.BlockSpec((tq, w), lambda b, h, i: (b * nq + i, h)),
                  pl.BlockSpec((seq, w), lambda b, h, i: (b, hg + h)),
                  pl.BlockSpec((seq, w), lambda b, h, i: (b, 2 * hg + h)),
                  pl.BlockSpec((seq, hp), lambda b, h, i: (b, 0))],
        out_specs=pl.BlockSpec((tq, w), lambda b, h, i: (b * nq + i, h)),
        scratch_shapes=[pltpu.VMEM((nh, seq, hd + LANES), BF16)],
        compiler_params=_params("parallel", "parallel", "arbitrary"),
        name="fox_attention",
    )(qkv, qkv, qkv, cum)


def _diff_attn_kernel(slope_ref, q_ref, k_ref, v_ref, lam_ref, g_ref, o_ref, kaug_ref, *,
                      tq, hd, lambda_init, eps):
    h = pl.program_id(1)
    qi = pl.program_id(2)
    slope2 = slope_ref[h] * LOG2E
    nq = k_ref.shape[0] // tq

    def alibi(start):
        return slope2 * (lax.broadcasted_iota(jnp.int32, (tq, 1), 0) + start).astype(F32)

    @pl.when(qi == 0)
    def _():
        for c in range(nq):
            sl = slice(c * tq, (c + 1) * tq)
            kx = _bias_lanes(alibi(c * tq), True)
            for a in range(2):
                kaug_ref[a, sl, :hd] = k_ref[sl, a * hd:(a + 1) * hd]
                kaug_ref[a, sl, hd:] = kx

    qx = _bias_lanes(-alibi(qi * tq), False)
    q = q_ref[...]
    qs = [jnp.concatenate([q[:, a * hd:(a + 1) * hd], qx], axis=1) for a in range(2)]

    ev = v_ref.shape[-1]

    def tile(j):
        start = pl.multiple_of(j * tq, tq)
        return [_qk(qs[a], kaug_ref[a, pl.ds(start, tq), :]) for a in range(2)], v_ref[pl.ds(start, tq), :]

    def body(j, carry):
        ss, v = tile(j)
        return tuple(_online_softmax_step(ss[a], v, carry[a]) for a in range(2))

    one = _softmax_init(tq, ev)
    carry = lax.fori_loop(0, qi, body, (one, one))
    ss, v = tile(qi)
    outs = []
    for a in range(2):
        _, l, acc = _online_softmax_step(_causal_mask(ss[a]), v, carry[a])
        outs.append(acc / l)
    lam = lam_ref[...]
    lam_full = (jnp.exp(jnp.sum(lam[0:1] * lam[1:2], axis=-1, keepdims=True))
                - jnp.exp(jnp.sum(lam[2:3] * lam[3:4], axis=-1, keepdims=True)) + lambda_init)
    out = outs[0] - lam_full * outs[1]
    ms = jnp.mean(out * out, axis=-1, keepdims=True)
    y = (out * lax.rsqrt(ms + eps)) * g_ref[...]
    o_ref[...] = (y * (1.0 - lambda_init)).astype(o_ref.dtype)


def diff_attention(qkv, lam, subln_g, batch, seq, heads, hd, lambda_init):
    assert hd == LANES
    m = qkv.shape[0]
    tq = _tile(seq, ATTN_TQ)
    nq = seq // tq
    ev = 2 * hd
    slopes = jnp.exp2(-8.0 * jnp.arange(1, heads + 1, dtype=F32) / heads)
    grid_spec = pltpu.PrefetchScalarGridSpec(
        num_scalar_prefetch=1,
        grid=(batch, heads, nq),
        in_specs=[pl.BlockSpec((tq, ev), lambda b, h, i, s: (b * nq + i, h)),
                  pl.BlockSpec((seq, ev), lambda b, h, i, s: (b, heads + h)),
                  pl.BlockSpec((seq, ev), lambda b, h, i, s: (b, 2 * heads + h)),
                  pl.BlockSpec((4, hd), lambda b, h, i, s: (0, 0)),
                  pl.BlockSpec((1, ev), lambda b, h, i, s: (0, 0))],
        out_specs=pl.BlockSpec((tq, ev), lambda b, h, i, s: (b * nq + i, h)),
        scratch_shapes=[pltpu.VMEM((2, seq, hd + LANES), BF16)],
    )
    return pl.pallas_call(
        functools.partial(_diff_attn_kernel, tq=tq, hd=hd, lambda_init=lambda_init, eps=DIFF_SUBLN_EPS),
        out_shape=jax.ShapeDtypeStruct((m, heads * ev), BF16),
        grid_spec=grid_spec,
        compiler_params=_params("parallel", "parallel", "arbitrary"),
        name="diff_attention",
    )(slopes, qkv, qkv, qkv, lam.astype(F32), subln_g.reshape(1, ev).astype(F32))


def _ssm_conv_kernel(x_ref, halo_ref, w_ref, b_ref, o_ref):
    i = pl.program_id(1)
    halo = jnp.where(i == 0, 0.0, halo_ref[...].astype(F32))
    u = _causal_conv_rows(x_ref[...].astype(F32), halo, w_ref[...], b_ref[...])
    o_ref[...] = _silu(u).astype(o_ref.dtype)


def ssm_conv_silu(zxbc, col_start, w, b, batch, seq):
    m = zxbc.shape[0]
    cdim = w.shape[1]
    ts, tc = _tile(seq, CONV_TS), _tile(cdim, CONV_TC)
    assert col_start % tc == 0
    cb0 = col_start // tc
    ns = seq // ts
    hb = ts // BF16_SUBLANES
    return pl.pallas_call(
        _ssm_conv_kernel,
        out_shape=jax.ShapeDtypeStruct((m, cdim), BF16),
        grid=(batch, ns, cdim // tc),
        in_specs=[pl.BlockSpec((ts, tc), lambda bi, i, j: (bi * ns + i, cb0 + j)),
                  pl.BlockSpec((BF16_SUBLANES, tc),
                               lambda bi, i, j: (jnp.maximum((bi * ns + i) * hb - 1, 0), cb0 + j)),
                  pl.BlockSpec((w.shape[0], tc), lambda bi, i, j: (0, j)),
                  pl.BlockSpec((1, tc), lambda bi, i, j: (0, j))],
        out_specs=pl.BlockSpec((ts, tc), lambda bi, i, j: (bi * ns + i, j)),
        compiler_params=_params("parallel", "parallel", "parallel"),
        name="ssm_conv_silu",
    )(zxbc, zxbc, w.astype(F32), b.reshape(1, cdim).astype(F32))


def _ssm_dt_kernel(dt_ref, bias_ref, alog_ref, dt_out, la_out):
    x = dt_ref[...] + bias_ref[...]
    dt = jnp.maximum(x, 0.0) + jnp.log(1.0 + jnp.exp(-jnp.abs(x)))
    dt_out[...] = dt
    la_out[...] = _cumsum_rows(dt * (-jnp.exp(alog_ref[...]))) * LOG2E


def ssm_dt(dt_raw, dt_bias, a_log):
    m, h = dt_raw.shape
    spec = pl.BlockSpec((SSM_CHUNK, h), lambda i: (i, 0))
    pspec = pl.BlockSpec((1, h), lambda i: (0, 0))
    return pl.pallas_call(
        _ssm_dt_kernel,
        out_shape=(jax.ShapeDtypeStruct((m, h), F32), jax.ShapeDtypeStruct((m, h), F32)),
        grid=(m // SSM_CHUNK,),
        in_specs=[spec, pspec, pspec],
        out_specs=(spec, spec),
        compiler_params=_params("parallel"),
        name="ssm_dt",
    )(dt_raw, dt_bias.reshape(1, h).astype(F32), a_log.reshape(1, h).astype(F32))


def _ssd_scan_kernel(x_ref, b_ref, c_ref, dt_ref, la_ref, lat_ref, z_ref, d_ref, g_ref, o_ref,
                     state_ref, *, hpg, pdim, eps):
    ci = pl.program_id(2)

    @pl.when(ci == 0)
    def _():
        state_ref[...] = jnp.zeros_like(state_ref)

    gw = hpg * pdim
    nstate = b_ref.shape[1] // dt_ref.shape[0]
    for gi in range(dt_ref.shape[0]):
        cols = slice(gi * gw, (gi + 1) * gw)
        ncols = slice(gi * nstate, (gi + 1) * nstate)
        _ssd_group_chunk(x_ref.at[:, cols], b_ref.at[:, ncols], c_ref.at[:, ncols], dt_ref.at[gi],
                         la_ref.at[gi], lat_ref.at[gi], z_ref.at[:, cols], d_ref.at[:, cols],
                         g_ref.at[:, cols], o_ref.at[:, cols], state_ref.at[gi], hpg, pdim, eps)


def _ssd_group_chunk(x_ref, b_ref, c_ref, dt_ref, la_ref, lat_ref, z_ref, d_ref, g_ref, o_ref, state_ref,
                     hpg, pdim, eps):
    chunk = x_ref.shape[0]
    npair = hpg // 2
    x = x_ref[...].astype(F32)
    bm = b_ref[...]
    cm = c_ref[...]
    dt = dt_ref[...]
    la = la_ref[...]
    lat = lat_ref[...]
    lane = lax.broadcasted_iota(jnp.int32, (chunk, 2 * pdim), 1)
    first = lane < pdim

    def expand(a):
        lo = lax.broadcasted_iota(jnp.int32, (a.shape[0], 2 * pdim), 1) < pdim
        return jnp.concatenate(
            [jnp.where(lo, a[:, 2 * p:2 * p + 1], a[:, 2 * p + 1:2 * p + 2]) for p in range(npair)], axis=1)

    dt_e = expand(dt)
    la_e = expand(la)
    la_end_e = expand(la[chunk - 1:chunk])
    cb = _qk(cm, bm)
    tri = (lax.broadcasted_iota(jnp.int32, (chunk, chunk), 0)
           >= lax.broadcasted_iota(jnp.int32, (chunk, chunk), 1))
    xdt = x * dt_e
    y_parts = []
    for p in range(npair):
        ms = []
        for hh in (2 * p, 2 * p + 1):
            seg = la[:, hh:hh + 1] - lat[hh:hh + 1, :]
            ms.append((cb * jnp.exp2(jnp.where(tri, seg, -jnp.inf))).astype(BF16))
        xp = xdt[:, p * 2 * pdim:(p + 1) * 2 * pdim]
        rhs = jnp.concatenate([jnp.where(first, xp, 0.0), jnp.where(first, 0.0, xp)], axis=0).astype(BF16)
        y_parts.append(jnp.dot(jnp.concatenate(ms, axis=1), rhs, preferred_element_type=F32))
    y = jnp.concatenate(y_parts, axis=1)
    state = state_ref[...]
    y = y + jnp.dot(cm, state.astype(BF16), preferred_element_type=F32) * jnp.exp2(la_e)
    to_end = jnp.exp2(la_end_e - la_e) * dt_e
    upd = lax.dot_general(bm, (x * to_end).astype(BF16), (((0,), (0,)), ((), ())),
                          preferred_element_type=F32)
    state_ref[...] = state * jnp.exp2(la_end_e) + upd
    y = y + d_ref[...] * x
    y = y * _silu(z_ref[...].astype(F32))
    y = y * lax.rsqrt(jnp.mean(y * y, axis=-1, keepdims=True) + eps)
    o_ref[...] = (y * g_ref[...]).astype(o_ref.dtype)


def ssd_scan(xbc, z, dt_g, la_g, lat_g, d_e, norm_g, batch, seq, d_inner, heads):
    m = xbc.shape[0]
    groups, nstate, chunk = SSM_GROUPS, SSM_STATE, SSM_CHUNK
    hpg = heads // groups
    pdim = d_inner // heads
    gps = math.gcd(groups, SSD_GROUPS_PER_STEP)
    gw = gps * hpg * pdim
    nw = gps * nstate
    nc = seq // chunk
    xb = d_inner // nw
    row = lambda b, g, c: b * nc + c
    return pl.pallas_call(
        functools.partial(_ssd_scan_kernel, hpg=hpg, pdim=pdim, eps=SSM_NORM_EPS),
        out_shape=jax.ShapeDtypeStruct((m, d_inner), BF16),
        grid=(batch, groups // gps, nc),
        in_specs=[pl.BlockSpec((chunk, gw), lambda b, g, c: (row(b, g, c), g)),
                  pl.BlockSpec((chunk, nw), lambda b, g, c: (row(b, g, c), xb + g)),
                  pl.BlockSpec((chunk, nw), lambda b, g, c: (row(b, g, c), xb + groups // gps + g)),
                  pl.BlockSpec((gps, chunk, hpg), lambda b, g, c: (g, row(b, g, c), 0)),
                  pl.BlockSpec((gps, chunk, hpg), lambda b, g, c: (g, row(b, g, c), 0)),
                  pl.BlockSpec((gps, hpg, chunk), lambda b, g, c: (g, 0, row(b, g, c))),
                  pl.BlockSpec((chunk, gw), lambda b, g, c: (row(b, g, c), g)),
                  pl.BlockSpec((1, gw), lambda b, g, c: (0, g)),
                  pl.BlockSpec((1, gw), lambda b, g, c: (0, g))],
        out_specs=pl.BlockSpec((chunk, gw), lambda b, g, c: (row(b, g, c), g)),
        scratch_shapes=[pltpu.VMEM((gps, nstate, gw // gps), F32)],
        compiler_params=_params("parallel", "parallel", "arbitrary"),
        name="ssd_scan",
    )(xbc, xbc, xbc, dt_g, la_g, lat_g, z, d_e, norm_g)


def _q_colscale(ncols, nq, hd):
    return jnp.where(jnp.arange(ncols) < nq, (hd ** -0.5) * LOG2E, 1.0).astype(F32)


def in_proj_cast(kind, layer, d, fox_w_in, fox_heads, ssm_w_in, diff_w_in, diff_hd):
    if kind == 0:
        return SideCast(fox_w_in, layer, d, _q_colscale(fox_w_in.shape[2], d, d // fox_heads))
    if kind == 1:
        return SideCast(ssm_w_in, layer, d, None)
    return SideCast(diff_w_in, layer, d, _q_colscale(diff_w_in.shape[2], d, diff_hd))


def fox_mixer(h, hn, layer, w_in, b_f, w_o, batch, seq, side):
    d = hn.shape[1]
    heads = b_f.shape[0]
    hd = d // heads
    qkv, side_out, wo = matmul(hn, w_in, out_dtype=BF16, ncols=3 * d,
                               sides=(side, SideCast(w_o, layer, w_o.shape[1], None)))
    f_logit = matmul(hn, w_in, col_start=3 * d, ncols=heads)
    hp = f_logit.shape[1]
    b_pad = jnp.pad(b_f.reshape(1, heads).astype(F32), ((0, 0), (0, hp - heads)))
    cum = fox_gate_cumsum(f_logit, b_pad, seq)
    attn = fox_attention(qkv, cum, batch, seq, heads, hd)
    return matmul(attn, wo, res=h), side_out


def ssd_mixer(h, hn, layer, w_in, conv_w, conv_b, dt_bias, a_log, d_skip, norm_g, w_out, batch, seq, side):
    heads = a_log.shape[0]
    d_inner = w_out.shape[1]
    conv_dim = conv_w.shape[1]
    groups = SSM_GROUPS
    hpg = heads // groups
    m = hn.shape[0]
    zxbc, side_out, wo = matmul(hn, w_in, out_dtype=BF16, ncols=d_inner + conv_dim,
                                sides=(side, SideCast(w_out, layer, d_inner, None)))
    dt_raw = matmul(hn, w_in, col_start=d_inner + conv_dim, ncols=heads)[:, :heads]
    xbc = ssm_conv_silu(zxbc, d_inner, conv_w, conv_b, batch, seq)
    dt, la = ssm_dt(dt_raw, dt_bias, a_log)
    dt_g = jnp.transpose(dt.reshape(m, groups, hpg), (1, 0, 2))
    la_g = jnp.transpose(la.reshape(m, groups, hpg), (1, 0, 2))
    lat_g = jnp.transpose(la_g, (0, 2, 1))
    pdim = d_inner // heads
    d_e = jnp.repeat(d_skip.astype(F32), pdim).reshape(1, d_inner)
    y = ssd_scan(xbc, zxbc, dt_g, la_g, lat_g, d_e, norm_g.reshape(1, d_inner).astype(F32),
                 batch, seq, d_inner, heads)
    return matmul(y, wo, res=h), side_out


def diff_mixer(h, hn, layer, w_in, lam, subln_g, w_o, lambda_init, batch, seq, side):
    d = hn.shape[1]
    hd = lam.shape[1]
    heads = d // (2 * hd)
    qkv, side_out, wo = matmul(hn, w_in, out_dtype=BF16,
                               sides=(side, SideCast(w_o, layer, w_o.shape[1], None)))
    attn = diff_attention(qkv, lam, subln_g, batch, seq, heads, hd, lambda_init)
    return matmul(attn, wo, res=h), side_out


def conv_ffn(h, hn, layer, wu, conv_w, conv_b, w_down, seq, side):
    f = w_down.shape[1]
    fp = _round_up(f, FFN_TN)
    cw = conv_w.astype(F32)
    cb = conv_b.reshape(1, 2 * f).astype(F32)
    pad = lambda a: jnp.pad(a, ((0, 0), (0, fp - f)))
    act, wd = ffn_up(hn, wu, pad(cw[:, :f]), pad(cw[:, f:]), pad(cb[:, :f]), pad(cb[:, f:]), seq,
                     SideCast(w_down, layer, fp, None))
    if side is None:
        return matmul(act, wd, res=h), None
    return matmul(act, wd, res=h, sides=(side,))


def kernel(x, mix_norm_g, ffn_norm_g, fox_w_in, fox_b_f, fox_w_o, ssm_w_in, ssm_conv_w, ssm_conv_b,
           ssm_dt_bias, ssm_a_log, ssm_d, ssm_norm_g, ssm_w_out, diff_w_in, diff_lambda, diff_subln_g,
           diff_w_o, ffn_w_up, ffn_conv_w, ffn_conv_b, ffn_w_down, final_norm_g):
    batch, seq, d = x.shape
    depth = mix_norm_g.shape[0]
    h = x.reshape(batch * seq, d)

    def in_cast(i):
        return in_proj_cast(i % N_MIXERS, i // N_MIXERS, d, fox_w_in, fox_b_f.shape[1], ssm_w_in,
                            diff_w_in, diff_lambda.shape[2])

    w_in = cast_rows(in_cast(0))
    for i in range(depth):
        kind, j = i % N_MIXERS, i // N_MIXERS
        hn = rmsnorm(h, mix_norm_g[i], NORM_EPS, BF16)
        up = SideCast(ffn_w_up, i, ffn_w_up.shape[1], None)
        if kind == 0:
            h, wu = fox_mixer(h, hn, j, w_in, fox_b_f[j], fox_w_o, batch, seq, up)
        elif kind == 1:
            h, wu = ssd_mixer(h, hn, j, w_in, ssm_conv_w[j], ssm_conv_b[j], ssm_dt_bias[j],
                              ssm_a_log[j], ssm_d[j], ssm_norm_g[j], ssm_w_out, batch, seq, up)
        else:
            lambda_init = 0.8 - 0.6 * math.exp(-0.3 * i)
            h, wu = diff_mixer(h, hn, j, w_in, diff_lambda[j], diff_subln_g[j], diff_w_o,
                               lambda_init, batch, seq, up)
        hn = rmsnorm(h, ffn_norm_g[i], NORM_EPS, BF16)
        h, w_in = conv_ffn(h, hn, i, wu, ffn_conv_w[i], ffn_conv_b[i], ffn_w_down, seq,
                           in_cast(i + 1) if i + 1 < depth else None)
    return rmsnorm(h, final_norm_g, NORM_EPS, F32).reshape(batch, seq, d)
```

```python
import collections
import functools
import math

import jax
import jax.numpy as jnp
from jax import lax
from jax.experimental import pallas as pl
from jax.experimental.pallas import tpu as pltpu

F32 = jnp.float32
BF16 = jnp.bfloat16
LOG2E = 1.4426950408889634

V7X_VMEM_LIMIT_BYTES = 56 * 1024 * 1024
LANES = 128
BF16_SUBLANES = 16

NORM_EPS = 1e-6
SSM_NORM_EPS = 1e-5
DIFF_SUBLN_EPS = 1e-5
SSM_GROUPS = 8
SSM_STATE = 128
SSM_CHUNK = 128
N_MIXERS = 3

MM_TM = 1024
MM_TN = 1024
MM_TK_MAX = 4096
MM_TK_SPLIT = 3072
FFN_TM = 1024
FFN_TN = 512
ATTN_TQ = 1024
FOX_HEADS_PER_STEP = 2
SSD_GROUPS_PER_STEP = 8
NORM_TM = 512
CONV_TS = 512
CONV_TC = 1024
CAST_STEPS = 64


def _tile(dim, pref):
    if dim <= pref:
        return dim
    t = pref
    while dim % t:
        t //= 2
    return t


def _round_up(x, m):
    return -(-x // m) * m


def _params(*sem):
    return pltpu.CompilerParams(dimension_semantics=sem, vmem_limit_bytes=V7X_VMEM_LIMIT_BYTES)


SideCast = collections.namedtuple("SideCast", "src layer out_rows colscale")


def _side_setup(side, nsteps, step_of):
    _, rows, cols = side.src.shape
    r = BF16_SUBLANES
    while side.out_rows % r or side.out_rows // r > nsteps:
        r += BF16_SUBLANES
        assert r <= side.out_rows
    nblocks = side.out_rows // r
    last_src = (rows - 1) // r
    blk = lambda *idx: jnp.minimum(step_of(*idx), nblocks - 1)
    in_specs = [pl.BlockSpec((None, r, cols), lambda *idx: (side.layer, jnp.minimum(blk(*idx), last_src), 0))]
    args = [side.src]
    if side.colscale is not None:
        in_specs.append(pl.BlockSpec((1, cols), lambda *idx: (0, 0)))
        args.append(side.colscale.reshape(1, cols).astype(F32))
    return dict(in_specs=in_specs, args=args, out_spec=pl.BlockSpec((r, cols), lambda *idx: (blk(*idx), 0)),
                out_shape=jax.ShapeDtypeStruct((side.out_rows, cols), BF16),
                static=(nblocks, rows, side.colscale is not None))


def _side_cast(in_refs, dst_ref, step, static):
    nblocks, rows, has_scale = static
    w = in_refs[0][...]
    if has_scale:
        w = w * in_refs[1][...]
    r = dst_ref.shape[0]
    if nblocks * r != rows:
        row = lax.broadcasted_iota(jnp.int32, w.shape, 0) + jnp.minimum(step, nblocks - 1) * r
        w = jnp.where(row < rows, w, 0.0)
    dst_ref[...] = w.astype(dst_ref.dtype)


def _cast_rows_kernel(*refs, static):
    _side_cast(refs[:-1], refs[-1], pl.program_id(0), static)


def cast_rows(side):
    setup = _side_setup(side, CAST_STEPS, lambda i: i)
    return pl.pallas_call(
        functools.partial(_cast_rows_kernel, static=setup["static"]),
        out_shape=setup["out_shape"],
        grid=(setup["static"][0],),
        in_specs=setup["in_specs"],
        out_specs=setup["out_spec"],
        compiler_params=_params("parallel"),
        name="cast_rows",
    )(*setup["args"])


def _rmsnorm_kernel(x_ref, g_ref, *rest, eps, ncols):
    x = x_ref[...]
    ms = jnp.mean(x * x, axis=-1, keepdims=True)
    hn = ((x * lax.rsqrt(ms + eps)) * g_ref[...]).astype(rest[-1 if ncols is None else -2].dtype)
    if ncols is None:
        rest[0][...] = hn
        return
    w_ref, o_ref, p_ref = rest
    o_ref[...] = hn
    w = w_ref[...]
    if ncols % w.shape[1]:
        col = lax.broadcasted_iota(jnp.int32, w.shape, 1)
        w = jnp.where(col < ncols, w, jnp.zeros_like(w))
    p_ref[...] = jnp.dot(hn, w, preferred_element_type=F32)


def rmsnorm(x, g, eps, out_dtype, proj=None):
    m, d = x.shape
    tm = _tile(m, NORM_TM)
    in_specs = [pl.BlockSpec((tm, d), lambda i: (i, 0)), pl.BlockSpec((1, d), lambda i: (0, 0))]
    args = [x, g.reshape(1, d).astype(F32)]
    out_shape = jax.ShapeDtypeStruct((m, d), out_dtype)
    out_specs = pl.BlockSpec((tm, d), lambda i: (i, 0))
    ncols = None
    if proj is not None:
        w, col_start, ncols = proj
        n = _round_up(ncols, LANES)
        assert col_start % n == 0
        in_specs.append(pl.BlockSpec((d, n), lambda i: (0, col_start // n)))
        args.append(w)
        out_shape = (out_shape, jax.ShapeDtypeStruct((m, n), F32))
        out_specs = (out_specs, pl.BlockSpec((tm, n), lambda i: (i, 0)))
    return pl.pallas_call(
        functools.partial(_rmsnorm_kernel, eps=eps, ncols=ncols),
        out_shape=out_shape,
        grid=(m // tm,),
        in_specs=in_specs,
        out_specs=out_specs,
        compiler_params=_params("parallel"),
        name="rmsnorm",
    )(*args)


def _mm_kernel(*refs, nk, has_res, sides, ncols):
    n_reg = 3 if has_res else 2
    n_side_in = sum(2 if s[2] else 1 for s in sides)
    step = (pl.program_id(0) * pl.num_programs(1) + pl.program_id(1)) * nk + pl.program_id(2)
    pos = n_reg
    for idx, s in enumerate(sides):
        n_in = 2 if s[2] else 1
        _side_cast(refs[pos:pos + n_in], refs[n_reg + n_side_in + 1 + idx], step, s)
        pos += n_in
    x_ref, w_ref = refs[:2]
    r_ref = refs[2] if has_res else None
    o_ref = refs[n_reg + n_side_in]
    w = w_ref[...]
    tn = w.shape[1]
    if ncols % tn:
        col = lax.broadcasted_iota(jnp.int32, w.shape, 1) + pl.program_id(1) * tn
        w = jnp.where(col < ncols, w, jnp.zeros_like(w))
    part = jnp.dot(x_ref[...], w, preferred_element_type=F32)

    def finish(acc):
        if has_res:
            acc = acc + r_ref[...]
        o_ref[...] = acc.astype(o_ref.dtype)

    if nk == 1:
        finish(part)
        return
    acc_ref = refs[-1]
    k = pl.program_id(2)

    @pl.when(k == 0)
    def _():
        acc_ref[...] = part

    @pl.when(jnp.logical_and(k > 0, k < nk - 1))
    def _():
        acc_ref[...] += part

    @pl.when(k == nk - 1)
    def _():
        finish(acc_ref[...] + part)


def matmul(x, w, res=None, out_dtype=F32, sides=(), col_start=0, ncols=None):
    m, kdim = x.shape
    ncols = w.shape[1] - col_start if ncols is None else ncols
    n = _round_up(ncols, LANES)
    tm, tn = _tile(m, MM_TM), _tile(n, MM_TN)
    assert col_start % tn == 0
    cb0 = col_start // tn
    tk = kdim
    if kdim > MM_TK_MAX:
        tk = MM_TK_SPLIT
        while kdim % tk:
            tk -= 2 * LANES
    nk = kdim // tk
    in_specs = [pl.BlockSpec((tm, tk), lambda i, j, k: (i, k)),
                pl.BlockSpec((tk, tn), lambda i, j, k: (k, cb0 + j))]
    args = [x, w]
    if res is not None:
        in_specs.append(pl.BlockSpec((tm, tn), lambda i, j, k: (i, j)))
        args.append(res)
    grid = (m // tm, n // tn, nk)
    out_shape = jax.ShapeDtypeStruct((m, n), out_dtype)
    out_specs = pl.BlockSpec((tm, tn), lambda i, j, k: (i, j))
    nj = grid[1]
    setups = [_side_setup(s, math.prod(grid), lambda i, j, k: (i * nj + j) * nk + k) for s in sides]
    for setup in setups:
        in_specs += setup["in_specs"]
        args += setup["args"]
    if sides:
        out_shape = (out_shape, *(s["out_shape"] for s in setups))
        out_specs = (out_specs, *(s["out_spec"] for s in setups))
    return pl.pallas_call(
        functools.partial(_mm_kernel, nk=nk, has_res=res is not None,
                          sides=tuple(s["static"] for s in setups), ncols=ncols),
        out_shape=out_shape,
        grid=grid,
        in_specs=in_specs,
        out_specs=out_specs,
        scratch_shapes=[pltpu.VMEM((tm, tn), F32)] if nk > 1 else [],
        compiler_params=_params("arbitrary", "arbitrary", "arbitrary"),
        name="matmul",
    )(*args)


def _causal_conv_rows(p, halo, w, b):
    kw = w.shape[0]
    hr = halo.shape[0]
    top = p[0:hr]
    u = b + w[kw - 1:kw] * p
    ut = b + w[kw - 1:kw] * top
    row = lax.broadcasted_iota(jnp.int32, top.shape, 0)
    for k in range(1, kw):
        wk = w[kw - 1 - k:kw - k]
        u = u + wk * pltpu.roll(p, k, 0)
        shifted_top = jnp.where(row < k, pltpu.roll(halo, k, 0), pltpu.roll(top, k, 0))
        ut = ut + wk * shifted_top
    return jnp.concatenate([ut, u[hr:]], axis=0)


def _silu(x):
    return x / (1.0 + jnp.exp(-x))


def _ffn_up_kernel(x_ref, wgl_ref, wgh_ref, wvl_ref, wvh_ref, cwg_ref, cwv_ref, cbg_ref, cbv_ref,
                   side_src, o_ref, side_dst, halo_g, halo_v, *, tiles_per_seq, side, last_is_half):
    j = pl.program_id(0)
    i = pl.program_id(1)
    _side_cast((side_src,), side_dst, j * pl.num_programs(1) + i, side)

    @pl.when(i % tiles_per_seq == 0)
    def _():
        halo_g[...] = jnp.zeros_like(halo_g)
        halo_v[...] = jnp.zeros_like(halo_v)

    tn = o_ref.shape[1]

    def tile(cols):
        x = x_ref[...]
        if cols == tn:
            wg = jnp.concatenate([wgl_ref[...], wgh_ref[...]], axis=1)
            wv = jnp.concatenate([wvl_ref[...], wvh_ref[...]], axis=1)
        else:
            wg, wv = wgl_ref[...], wvl_ref[...]
            o_ref[:, cols:] = jnp.zeros((o_ref.shape[0], tn - cols), o_ref.dtype)
        pg = jnp.dot(x, wg, preferred_element_type=F32)
        pv = jnp.dot(x, wv, preferred_element_type=F32)
        ug = _causal_conv_rows(pg, halo_g[:, :cols], cwg_ref[:, :cols], cbg_ref[:, :cols])
        uv = _causal_conv_rows(pv, halo_v[:, :cols], cwv_ref[:, :cols], cbv_ref[:, :cols])
        hr = halo_g.shape[0]
        halo_g[:, :cols] = pg[pg.shape[0] - hr:]
        halo_v[:, :cols] = pv[pv.shape[0] - hr:]
        o_ref[:, :cols] = (_silu(ug) * uv).astype(o_ref.dtype)

    if not last_is_half:
        tile(tn)
        return
    last = pl.num_programs(0) - 1

    @pl.when(j < last)
    def _():
        tile(tn)

    @pl.when(j == last)
    def _():
        tile(tn // 2)


def ffn_up(x, w_up, cwg, cwv, cbg, cbv, seq, side):
    m, d = x.shape
    f = w_up.shape[1] // 2
    fp = cwg.shape[1]
    tm, tn = _tile(seq, FFN_TM), _tile(fp, FFN_TN)
    half = tn // 2
    assert f % half == 0
    nb, last = f // half, 2 * f // half - 1
    kw = cwg.shape[0]
    wspec = lambda off: pl.BlockSpec((d, half), lambda j, i: (0, jnp.minimum(2 * j + off, last)))
    cspec = pl.BlockSpec((kw, tn), lambda j, i: (0, j))
    bspec = pl.BlockSpec((1, tn), lambda j, i: (0, j))
    nj, ni = fp // tn, m // tm
    assert side.colscale is None
    setup = _side_setup(side, nj * ni, lambda j, i: j * ni + i)
    return pl.pallas_call(
        functools.partial(_ffn_up_kernel, tiles_per_seq=seq // tm, side=setup["static"],
                          last_is_half=fp - f == half),
        out_shape=(jax.ShapeDtypeStruct((m, fp), BF16), setup["out_shape"]),
        grid=(nj, ni),
        in_specs=[pl.BlockSpec((tm, d), lambda j, i: (i, 0)),
                  wspec(0), wspec(1), wspec(nb), wspec(nb + 1), cspec, cspec, bspec, bspec] + setup["in_specs"],
        out_specs=(pl.BlockSpec((tm, tn), lambda j, i: (i, j)), setup["out_spec"]),
        scratch_shapes=[pltpu.VMEM((BF16_SUBLANES, tn), F32), pltpu.VMEM((BF16_SUBLANES, tn), F32)],
        compiler_params=_params("arbitrary", "arbitrary"),
        name="ffn_up",
    )(x, w_up, w_up, w_up, w_up, cwg, cwv, cbg, cbv, side.src)


def _cumsum_rows(y):
    n = y.shape[0]
    row = lax.broadcasted_iota(jnp.int32, y.shape, 0)
    shift = 1
    while shift < n:
        y = y + jnp.where(row >= shift, pltpu.roll(y, shift, 0), 0.0)
        shift *= 2
    return y


def _fox_gate_kernel(f_ref, b_ref, o_ref):
    x = f_ref[...] + b_ref[...]
    log_f = jnp.minimum(x, 0.0) - jnp.log(1.0 + jnp.exp(-jnp.abs(x)))
    o_ref[...] = _cumsum_rows(log_f)


def fox_gate_cumsum(f_logit, b_f, seq):
    m, hp = f_logit.shape
    return pl.pallas_call(
        _fox_gate_kernel,
        out_shape=jax.ShapeDtypeStruct((m, hp), F32),
        grid=(m // seq,),
        in_specs=[pl.BlockSpec((seq, hp), lambda b: (b, 0)), pl.BlockSpec((1, hp), lambda b: (0, 0))],
        out_specs=pl.BlockSpec((seq, hp), lambda b: (b, 0)),
        compiler_params=_params("parallel"),
        name="fox_gate_cumsum",
    )(f_logit, b_f)


def _bias_lanes(x, pieces_first):
    hi = x.astype(BF16).astype(F32)
    rem = x - hi
    lo = rem.astype(BF16).astype(F32)
    lo2 = rem - lo
    lane = lax.broadcasted_iota(jnp.int32, (x.shape[0], LANES), 1)
    base = 0 if pieces_first else 3
    ones = 3 if pieces_first else 0
    out = jnp.where(lane == base, hi, jnp.where(lane == base + 1, lo, jnp.where(lane == base + 2, lo2, 0.0)))
    out = jnp.where(jnp.logical_and(lane >= ones, lane < ones + 3), 1.0, out)
    return out.astype(BF16)


def _qk(q, k):
    return lax.dot_general(q, k, (((1,), (1,)), ((), ())), preferred_element_type=F32)


def _online_softmax_step(s, v, carry):
    m, l, acc = carry
    m_new = jnp.maximum(m, jnp.max(s, axis=-1, keepdims=True))
    alpha = jnp.exp2(m - m_new)
    p = jnp.exp2(s - m_new)
    l = alpha * l + jnp.sum(p, axis=-1, keepdims=True)
    acc = alpha * acc + jnp.dot(p.astype(v.dtype), v, preferred_element_type=F32)
    return m_new, l, acc


def _softmax_init(tq, ev):
    return (jnp.full((tq, 1), -jnp.inf, F32), jnp.zeros((tq, 1), F32), jnp.zeros((tq, ev), F32))


def _causal_mask(s):
    r = lax.broadcasted_iota(jnp.int32, s.shape, 0)
    c = lax.broadcasted_iota(jnp.int32, s.shape, 1)
    return jnp.where(c <= r, s, -jnp.inf)


def _fox_attn_kernel(q_ref, k_ref, v_ref, cum_ref, o_ref, kaug_ref, *, tq, hd):
    hg = pl.program_id(1)
    qi = pl.program_id(2)
    nh = k_ref.shape[-1] // hd
    nq = k_ref.shape[0] // tq

    def gate(rows, a):
        lane = lax.broadcasted_iota(jnp.int32, rows.shape, 1)
        return jnp.sum(jnp.where(lane == hg * nh + a, rows, 0.0), axis=-1, keepdims=True) * LOG2E

    @pl.when(qi == 0)
    def _():
        for c in range(nq):
            sl = slice(c * tq, (c + 1) * tq)
            rows = cum_ref[sl, :]
            for a in range(nh):
                kaug_ref[a, sl, :hd] = k_ref[sl, a * hd:(a + 1) * hd]
                kaug_ref[a, sl, hd:] = _bias_lanes(-gate(rows, a), True)

    q_start = pl.multiple_of(qi * tq, tq)
    q_rows = cum_ref[pl.ds(q_start, tq), :]
    q = q_ref[...]
    qs = [jnp.concatenate([q[:, a * hd:(a + 1) * hd], _bias_lanes(gate(q_rows, a), False)], axis=1)
          for a in range(nh)]

    def tile(j, a):
        start = pl.multiple_of(j * tq, tq)
        return _qk(qs[a], kaug_ref[a, pl.ds(start, tq), :]), v_ref[pl.ds(start, tq), a * hd:(a + 1) * hd]

    def body(j, carry):
        return tuple(_online_softmax_step(*tile(j, a), carry[a]) for a in range(nh))

    carry = lax.fori_loop(0, qi, body, (_softmax_init(tq, hd),) * nh)
    for a in range(nh):
        s, v = tile(qi, a)
        _, l, acc = _online_softmax_step(_causal_mask(s), v, carry[a])
        o_ref[:, a * hd:(a + 1) * hd] = (acc / l).astype(o_ref.dtype)


def fox_attention(qkv, cum, batch, seq, heads, hd):
    assert hd == LANES
    m = qkv.shape[0]
    tq = _tile(seq, ATTN_TQ)
    nq = seq // tq
    hp = cum.shape[1]
    nh = FOX_HEADS_PER_STEP
    hg = heads // nh
    w = nh * hd
    return pl.pallas_call(
        functools.partial(_fox_attn_kernel, tq=tq, hd=hd),
        out_shape=jax.ShapeDtypeStruct((m, heads * hd), BF16),
        grid=(batch, hg, nq),
        in_specs=[pl.BlockSpec((tq, w), lambda b, h, i: (b * nq + i, h)),
                  pl.BlockSpec((seq, w), lambda b, h, i: (b, hg + h)),
                  pl.BlockSpec((seq, w), lambda b, h, i: (b, 2 * hg + h)),
                  pl.BlockSpec((seq, hp), lambda b, h, i: (b, 0))],
        out_specs=pl.BlockSpec((tq, w), lambda b, h, i: (b * nq + i, h)),
        scratch_shapes=[pltpu.VMEM((nh, seq, hd + LANES), BF16)],
        compiler_params=_params("parallel", "parallel", "arbitrary"),
        name="fox_attention",
    )(qkv, qkv, qkv, cum)


def _diff_attn_kernel(slope_ref, q_ref, k_ref, v_ref, lam_ref, g_ref, o_ref, kaug_ref, *,
                      tq, hd, lambda_init, eps):
    h = pl.program_id(1)
    qi = pl.program_id(2)
    slope2 = slope_ref[h] * LOG2E
    nq = k_ref.shape[0] // tq

    def alibi(start):
        return slope2 * (lax.broadcasted_iota(jnp.int32, (tq, 1), 0) + start).astype(F32)

    @pl.when(qi == 0)
    def _():
        for c in range(nq):
            sl = slice(c * tq, (c + 1) * tq)
            kx = _bias_lanes(alibi(c * tq), True)
            for a in range(2):
                kaug_ref[a, sl, :hd] = k_ref[sl, a * hd:(a + 1) * hd]
                kaug_ref[a, sl, hd:] = kx

    qx = _bias_lanes(-alibi(qi * tq), False)
    q = q_ref[...]
    qs = [jnp.concatenate([q[:, a * hd:(a + 1) * hd], qx], axis=1) for a in range(2)]

    ev = v_ref.shape[-1]

    def tile(j):
        start = pl.multiple_of(j * tq, tq)
        return [_qk(qs[a], kaug_ref[a, pl.ds(start, tq), :]) for a in range(2)], v_ref[pl.ds(start, tq), :]

    def body(j, carry):
        ss, v = tile(j)
        return tuple(_online_softmax_step(ss[a], v, carry[a]) for a in range(2))

    one = _softmax_init(tq, ev)
    carry = lax.fori_loop(0, qi, body, (one, one))
    ss, v = tile(qi)
    outs = []
    for a in range(2):
        _, l, acc = _online_softmax_step(_causal_mask(ss[a]), v, carry[a])
        outs.append(acc / l)
    lam = lam_ref[...]
    lam_full = (jnp.exp(jnp.sum(lam[0:1] * lam[1:2], axis=-1, keepdims=True))
                - jnp.exp(jnp.sum(lam[2:3] * lam[3:4], axis=-1, keepdims=True)) + lambda_init)
    out = outs[0] - lam_full * outs[1]
    ms = jnp.mean(out * out, axis=-1, keepdims=True)
    y = (out * lax.rsqrt(ms + eps)) * g_ref[...]
    o_ref[...] = (y * (1.0 - lambda_init)).astype(o_ref.dtype)


def diff_attention(qkv, lam, subln_g, batch, seq, heads, hd, lambda_init):
    assert hd == LANES
    m = qkv.shape[0]
    tq = _tile(seq, ATTN_TQ)
    nq = seq // tq
    ev = 2 * hd
    slopes = jnp.exp2(-8.0 * jnp.arange(1, heads + 1, dtype=F32) / heads)
    grid_spec = pltpu.PrefetchScalarGridSpec(
        num_scalar_prefetch=1,
        grid=(batch, heads, nq),
        in_specs=[pl.BlockSpec((tq, ev), lambda b, h, i, s: (b * nq + i, h)),
                  pl.BlockSpec((seq, ev), lambda b, h, i, s: (b, heads + h)),
                  pl.BlockSpec((seq, ev), lambda b, h, i, s: (b, 2 * heads + h)),
                  pl.BlockSpec((4, hd), lambda b, h, i, s: (0, 0)),
                  pl.BlockSpec((1, ev), lambda b, h, i, s: (0, 0))],
        out_specs=pl.BlockSpec((tq, ev), lambda b, h, i, s: (b * nq + i, h)),
        scratch_shapes=[pltpu.VMEM((2, seq, hd + LANES), BF16)],
    )
    return pl.pallas_call(
        functools.partial(_diff_attn_kernel, tq=tq, hd=hd, lambda_init=lambda_init, eps=DIFF_SUBLN_EPS),
        out_shape=jax.ShapeDtypeStruct((m, heads * ev), BF16),
        grid_spec=grid_spec,
        compiler_params=_params("parallel", "parallel", "arbitrary"),
        name="diff_attention",
    )(slopes, qkv, qkv, qkv, lam.astype(F32), subln_g.reshape(1, ev).astype(F32))


def _ssm_conv_kernel(x_ref, halo_ref, w_ref, b_ref, o_ref):
    i = pl.program_id(1)
    halo = jnp.where(i == 0, 0.0, halo_ref[...].astype(F32))
    u = _causal_conv_rows(x_ref[...].astype(F32), halo, w_ref[...], b_ref[...])
    o_ref[...] = _silu(u).astype(o_ref.dtype)


def ssm_conv_silu(zxbc, col_start, w, b, batch, seq):
    m = zxbc.shape[0]
    cdim = w.shape[1]
    ts, tc = _tile(seq, CONV_TS), _tile(cdim, CONV_TC)
    assert col_start % tc == 0
    cb0 = col_start // tc
    ns = seq // ts
    hb = ts // BF16_SUBLANES
    return pl.pallas_call(
        _ssm_conv_kernel,
        out_shape=jax.ShapeDtypeStruct((m, cdim), BF16),
        grid=(batch, ns, cdim // tc),
        in_specs=[pl.BlockSpec((ts, tc), lambda bi, i, j: (bi * ns + i, cb0 + j)),
                  pl.BlockSpec((BF16_SUBLANES, tc),
                               lambda bi, i, j: (jnp.maximum((bi * ns + i) * hb - 1, 0), cb0 + j)),
                  pl.BlockSpec((w.shape[0], tc), lambda bi, i, j: (0, j)),
                  pl.BlockSpec((1, tc), lambda bi, i, j: (0, j))],
        out_specs=pl.BlockSpec((ts, tc), lambda bi, i, j: (bi * ns + i, j)),
        compiler_params=_params("parallel", "parallel", "parallel"),
        name="ssm_conv_silu",
    )(zxbc, zxbc, w.astype(F32), b.reshape(1, cdim).astype(F32))


def _ssm_dt_kernel(dt_ref, bias_ref, alog_ref, dt_out, la_out):
    x = dt_ref[...] + bias_ref[...]
    dt = jnp.maximum(x, 0.0) + jnp.log(1.0 + jnp.exp(-jnp.abs(x)))
    dt_out[...] = dt
    la_out[...] = _cumsum_rows(dt * (-jnp.exp(alog_ref[...]))) * LOG2E


def ssm_dt(dt_raw, dt_bias, a_log):
    m, h = dt_raw.shape
    spec = pl.BlockSpec((SSM_CHUNK, h), lambda i: (i, 0))
    pspec = pl.BlockSpec((1, h), lambda i: (0, 0))
    return pl.pallas_call(
        _ssm_dt_kernel,
        out_shape=(jax.ShapeDtypeStruct((m, h), F32), jax.ShapeDtypeStruct((m, h), F32)),
        grid=(m // SSM_CHUNK,),
        in_specs=[spec, pspec, pspec],
        out_specs=(spec, spec),
        compiler_params=_params("parallel"),
        name="ssm_dt",
    )(dt_raw, dt_bias.reshape(1, h).astype(F32), a_log.reshape(1, h).astype(F32))


def _ssd_scan_kernel(x_ref, b_ref, c_ref, dt_ref, la_ref, lat_ref, z_ref, d_ref, g_ref, o_ref,
                     state_ref, *, hpg, pdim, eps):
    ci = pl.program_id(2)

    @pl.when(ci == 0)
    def _():
        state_ref[...] = jnp.zeros_like(state_ref)

    gw = hpg * pdim
    nstate = b_ref.shape[1] // dt_ref.shape[0]
    for gi in range(dt_ref.shape[0]):
        cols = slice(gi * gw, (gi + 1) * gw)
        ncols = slice(gi * nstate, (gi + 1) * nstate)
        _ssd_group_chunk(x_ref.at[:, cols], b_ref.at[:, ncols], c_ref.at[:, ncols], dt_ref.at[gi],
                         la_ref.at[gi], lat_ref.at[gi], z_ref.at[:, cols], d_ref.at[:, cols],
                         g_ref.at[:, cols], o_ref.at[:, cols], state_ref.at[gi], hpg, pdim, eps)


def _ssd_group_chunk(x_ref, b_ref, c_ref, dt_ref, la_ref, lat_ref, z_ref, d_ref, g_ref, o_ref, state_ref,
                     hpg, pdim, eps):
    chunk = x_ref.shape[0]
    npair = hpg // 2
    x = x_ref[...].astype(F32)
    bm = b_ref[...]
    cm = c_ref[...]
    dt = dt_ref[...]
    la = la_ref[...]
    lat = lat_ref[...]
    lane = lax.broadcasted_iota(jnp.int32, (chunk, 2 * pdim), 1)
    first = lane < pdim

    def expand(a):
        lo = lax.broadcasted_iota(jnp.int32, (a.shape[0], 2 * pdim), 1) < pdim
        return jnp.concatenate(
            [jnp.where(lo, a[:, 2 * p:2 * p + 1], a[:, 2 * p + 1:2 * p + 2]) for p in range(npair)], axis=1)

    dt_e = expand(dt)
    la_e = expand(la)
    la_end_e = expand(la[chunk - 1:chunk])
    cb = _qk(cm, bm)
    tri = (lax.broadcasted_iota(jnp.int32, (chunk, chunk), 0)
           >= lax.broadcasted_iota(jnp.int32, (chunk, chunk), 1))
    xdt = x * dt_e
    y_parts = []
    for p in range(npair):
        ms = []
        for hh in (2 * p, 2 * p + 1):
            seg = la[:, hh:hh + 1] - lat[hh:hh + 1, :]
            ms.append((cb * jnp.exp2(jnp.where(tri, seg, -jnp.inf))).astype(BF16))
        xp = xdt[:, p * 2 * pdim:(p + 1) * 2 * pdim]
        rhs = jnp.concatenate([jnp.where(first, xp, 0.0), jnp.where(first, 0.0, xp)], axis=0).astype(BF16)
        y_parts.append(jnp.dot(jnp.concatenate(ms, axis=1), rhs, preferred_element_type=F32))
    y = jnp.concatenate(y_parts, axis=1)
    state = state_ref[...]
    y = y + jnp.dot(cm, state.astype(BF16), preferred_element_type=F32) * jnp.exp2(la_e)
    to_end = jnp.exp2(la_end_e - la_e) * dt_e
    upd = lax.dot_general(bm, (x * to_end).astype(BF16), (((0,), (0,)), ((), ())),
                          preferred_element_type=F32)
    state_ref[...] = state * jnp.exp2(la_end_e) + upd
    y = y + d_ref[...] * x
    y = y * _silu(z_ref[...].astype(F32))
    y = y * lax.rsqrt(jnp.mean(y * y, axis=-1, keepdims=True) + eps)
    o_ref[...] = (y * g_ref[...]).astype(o_ref.dtype)


def ssd_scan(xbc, z, dt_g, la_g, lat_g, d_e, norm_g, batch, seq, d_inner, heads):
    m = xbc.shape[0]
    groups, nstate, chunk = SSM_GROUPS, SSM_STATE, SSM_CHUNK
    hpg = heads // groups
    pdim = d_inner // heads
    gps = math.gcd(groups, SSD_GROUPS_PER_STEP)
    gw = gps * hpg * pdim
    nw = gps * nstate
    nc = seq // chunk
    xb = d_inner // nw
    row = lambda b, g, c: b * nc + c
    return pl.pallas_call(
        functools.partial(_ssd_scan_kernel, hpg=hpg, pdim=pdim, eps=SSM_NORM_EPS),
        out_shape=jax.ShapeDtypeStruct((m, d_inner), BF16),
        grid=(batch, groups // gps, nc),
        in_specs=[pl.BlockSpec((chunk, gw), lambda b, g, c: (row(b, g, c), g)),
                  pl.BlockSpec((chunk, nw), lambda b, g, c: (row(b, g, c), xb + g)),
                  pl.BlockSpec((chunk, nw), lambda b, g, c: (row(b, g, c), xb + groups // gps + g)),
                  pl.BlockSpec((gps, chunk, hpg), lambda b, g, c: (g, row(b, g, c), 0)),
                  pl.BlockSpec((gps, chunk, hpg), lambda b, g, c: (g, row(b, g, c), 0)),
                  pl.BlockSpec((gps, hpg, chunk), lambda b, g, c: (g, 0, row(b, g, c))),
                  pl.BlockSpec((chunk, gw), lambda b, g, c: (row(b, g, c), g)),
                  pl.BlockSpec((1, gw), lambda b, g, c: (0, g)),
                  pl.BlockSpec((1, gw), lambda b, g, c: (0, g))],
        out_specs=pl.BlockSpec((chunk, gw), lambda b, g, c: (row(b, g, c), g)),
        scratch_shapes=[pltpu.VMEM((gps, nstate, gw // gps), F32)],
        compiler_params=_params("parallel", "parallel", "arbitrary"),
        name="ssd_scan",
    )(xbc, xbc, xbc, dt_g, la_g, lat_g, z, d_e, norm_g)


def _q_colscale(ncols, nq, hd):
    return jnp.where(jnp.arange(ncols) < nq, (hd ** -0.5) * LOG2E, 1.0).astype(F32)


def in_proj_cast(kind, layer, d, fox_w_in, fox_heads, ssm_w_in, diff_w_in, diff_hd):
    if kind == 0:
        return SideCast(fox_w_in, layer, d, _q_colscale(fox_w_in.shape[2], d, d // fox_heads))
    if kind == 1:
        return SideCast(ssm_w_in, layer, d, None)
    return SideCast(diff_w_in, layer, d, _q_colscale(diff_w_in.shape[2], d, diff_hd))


def fox_mixer(h, hn, f_logit, layer, w_in, b_f, w_o, batch, seq, side):
    d = hn.shape[1]
    heads = b_f.shape[0]
    hd = d // heads
    qkv, side_out, wo = matmul(hn, w_in, out_dtype=BF16, ncols=3 * d,
                               sides=(side, SideCast(w_o, layer, w_o.shape[1], None)))
    hp = f_logit.shape[1]
    b_pad = jnp.pad(b_f.reshape(1, heads).astype(F32), ((0, 0), (0, hp - heads)))
    cum = fox_gate_cumsum(f_logit, b_pad, seq)
    attn = fox_attention(qkv, cum, batch, seq, heads, hd)
    return matmul(attn, wo, res=h), side_out


def ssd_mixer(h, hn, dt_raw, layer, w_in, conv_w, conv_b, dt_bias, a_log, d_skip, norm_g, w_out, batch, seq,
              side):
    heads = a_log.shape[0]
    d_inner = w_out.shape[1]
    conv_dim = conv_w.shape[1]
    groups = SSM_GROUPS
    hpg = heads // groups
    m = hn.shape[0]
    zxbc, side_out, wo = matmul(hn, w_in, out_dtype=BF16, ncols=d_inner + conv_dim,
                                sides=(side, SideCast(w_out, layer, d_inner, None)))
    dt_raw = dt_raw[:, :heads]
    xbc = ssm_conv_silu(zxbc, d_inner, conv_w, conv_b, batch, seq)
    dt, la = ssm_dt(dt_raw, dt_bias, a_log)
    dt_g = jnp.transpose(dt.reshape(m, groups, hpg), (1, 0, 2))
    la_g = jnp.transpose(la.reshape(m, groups, hpg), (1, 0, 2))
    lat_g = jnp.transpose(la_g, (0, 2, 1))
    pdim = d_inner // heads
    d_e = jnp.repeat(d_skip.astype(F32), pdim).reshape(1, d_inner)
    y = ssd_scan(xbc, zxbc, dt_g, la_g, lat_g, d_e, norm_g.reshape(1, d_inner).astype(F32),
                 batch, seq, d_inner, heads)
    return matmul(y, wo, res=h), side_out


def diff_mixer(h, hn, layer, w_in, lam, subln_g, w_o, lambda_init, batch, seq, side):
    d = hn.shape[1]
    hd = lam.shape[1]
    heads = d // (2 * hd)
    qkv, side_out, wo = matmul(hn, w_in, out_dtype=BF16,
                               sides=(side, SideCast(w_o, layer, w_o.shape[1], None)))
    attn = diff_attention(qkv, lam, subln_g, batch, seq, heads, hd, lambda_init)
    return matmul(attn, wo, res=h), side_out


def conv_ffn(h, hn, layer, wu, conv_w, conv_b, w_down, seq, side):
    f = w_down.shape[1]
    fp = _round_up(f, FFN_TN)
    cw = conv_w.astype(F32)
    cb = conv_b.reshape(1, 2 * f).astype(F32)
    pad = lambda a: jnp.pad(a, ((0, 0), (0, fp - f)))
    act, wd = ffn_up(hn, wu, pad(cw[:, :f]), pad(cw[:, f:]), pad(cb[:, :f]), pad(cb[:, f:]), seq,
                     SideCast(w_down, layer, fp, None))
    if side is None:
        return matmul(act, wd, res=h), None
    return matmul(act, wd, res=h, sides=(side,))


def kernel(x, mix_norm_g, ffn_norm_g, fox_w_in, fox_b_f, fox_w_o, ssm_w_in, ssm_conv_w, ssm_conv_b,
           ssm_dt_bias, ssm_a_log, ssm_d, ssm_norm_g, ssm_w_out, diff_w_in, diff_lambda, diff_subln_g,
           diff_w_o, ffn_w_up, ffn_conv_w, ffn_conv_b, ffn_w_down, final_norm_g):
    batch, seq, d = x.shape
    depth = mix_norm_g.shape[0]
    h = x.reshape(batch * seq, d)

    def in_cast(i):
        return in_proj_cast(i % N_MIXERS, i // N_MIXERS, d, fox_w_in, fox_b_f.shape[1], ssm_w_in,
                            diff_w_in, diff_lambda.shape[2])

    w_in = cast_rows(in_cast(0))
    for i in range(depth):
        kind, j = i % N_MIXERS, i // N_MIXERS
        up = SideCast(ffn_w_up, i, ffn_w_up.shape[1], None)
        if kind == 0:
            hn, f_logit = rmsnorm(h, mix_norm_g[i], NORM_EPS, BF16, proj=(w_in, 3 * d, fox_b_f.shape[1]))
            h, wu = fox_mixer(h, hn, f_logit, j, w_in, fox_b_f[j], fox_w_o, batch, seq, up)
        elif kind == 1:
            dt_start = ssm_w_out.shape[1] + ssm_conv_w.shape[2]
            hn, dt_raw = rmsnorm(h, mix_norm_g[i], NORM_EPS, BF16, proj=(w_in, dt_start, ssm_a_log.shape[1]))
            h, wu = ssd_mixer(h, hn, dt_raw, j, w_in, ssm_conv_w[j], ssm_conv_b[j], ssm_dt_bias[j],
                              ssm_a_log[j], ssm_d[j], ssm_norm_g[j], ssm_w_out, batch, seq, up)
        else:
            hn = rmsnorm(h, mix_norm_g[i], NORM_EPS, BF16)
            lambda_init = 0.8 - 0.6 * math.exp(-0.3 * i)
            h, wu = diff_mixer(h, hn, j, w_in, diff_lambda[j], diff_subln_g[j], diff_w_o,
                               lambda_init, batch, seq, up)
        hn = rmsnorm(h, ffn_norm_g[i], NORM_EPS, BF16)
        h, w_in = conv_ffn(h, hn, i, wu, ffn_conv_w[i], ffn_conv_b[i], ffn_w_down, seq,
                           in_cast(i + 1) if i + 1 < depth else None)
    return rmsnorm(h, final_norm_g, NORM_EPS, F32).reshape(batch, seq, d)
```

```python
import collections
import functools
import math

import jax
import jax.numpy as jnp
from jax import lax
from jax.experimental import pallas as pl
from jax.experimental.pallas import tpu as pltpu

F32 = jnp.float32
BF16 = jnp.bfloat16
LOG2E = 1.4426950408889634

V7X_VMEM_LIMIT_BYTES = 56 * 1024 * 1024
LANES = 128
BF16_SUBLANES = 16

NORM_EPS = 1e-6
SSM_NORM_EPS = 1e-5
DIFF_SUBLN_EPS = 1e-5
SSM_GROUPS = 8
SSM_STATE = 128
SSM_CHUNK = 128
N_MIXERS = 3

MM_TM = 1024
MM_TN = 1024
MM_TK_MAX = 4096
MM_TK_SPLIT = 3072
FFN_TM = 1024
FFN_TN = 512
ATTN_TQ = 1024
FOX_HEADS_PER_STEP = 2
SSD_GROUPS_PER_STEP = 8
NORM_TM = 512
CONV_TS = 512
CONV_TC = 1024
CAST_STEPS = 64


def _tile(dim, pref):
    if dim <= pref:
        return dim
    t = pref
    while dim % t:
        t //= 2
    return t


def _round_up(x, m):
    return -(-x // m) * m


def _params(*sem):
    return pltpu.CompilerParams(dimension_semantics=sem, vmem_limit_bytes=V7X_VMEM_LIMIT_BYTES)


SideCast = collections.namedtuple("SideCast", "src layer out_rows qscale")


def _side_setup(side, nsteps, step_of):
    _, rows, cols = side.src.shape
    r = BF16_SUBLANES
    while side.out_rows % r or side.out_rows // r > nsteps:
        r += BF16_SUBLANES
        assert r <= side.out_rows
    nblocks = side.out_rows // r
    last_src = (rows - 1) // r
    blk = lambda *idx: jnp.minimum(step_of(*idx), nblocks - 1)
    in_spec = pl.BlockSpec((None, r, cols), lambda *idx: (side.layer, jnp.minimum(blk(*idx), last_src), 0))
    return dict(in_specs=[in_spec], args=[side.src],
                out_spec=pl.BlockSpec((r, cols), lambda *idx: (blk(*idx), 0)),
                out_shape=jax.ShapeDtypeStruct((side.out_rows, cols), BF16),
                static=(nblocks, rows, side.qscale))


def _side_cast(src_ref, dst_ref, step, static):
    nblocks, rows, qscale = static
    w = src_ref[...]
    r = dst_ref.shape[0]
    row = lax.broadcasted_iota(jnp.int32, w.shape, 0) + jnp.minimum(step, nblocks - 1) * r
    if qscale is not None:
        axis, n, c = qscale
        idx = row if axis == 0 else lax.broadcasted_iota(jnp.int32, w.shape, 1)
        w = jnp.where(idx < n, w * c, w)
    if nblocks * r != rows:
        w = jnp.where(row < rows, w, 0.0)
    dst_ref[...] = w.astype(dst_ref.dtype)


def _cast_rows_kernel(src_ref, dst_ref, *, static):
    _side_cast(src_ref, dst_ref, pl.program_id(0), static)


def cast_rows(side):
    setup = _side_setup(side, CAST_STEPS, lambda i: i)
    return pl.pallas_call(
        functools.partial(_cast_rows_kernel, static=setup["static"]),
        out_shape=setup["out_shape"],
        grid=(setup["static"][0],),
        in_specs=setup["in_specs"],
        out_specs=setup["out_spec"],
        compiler_params=_params("parallel"),
        name="cast_rows",
    )(*setup["args"])


def _rmsnorm_kernel(x_ref, g_ref, *rest, eps, ncols, w_t):
    x = x_ref[...]
    ms = jnp.mean(x * x, axis=-1, keepdims=True)
    hn = ((x * lax.rsqrt(ms + eps)) * g_ref[...]).astype(rest[-1 if ncols is None else -2].dtype)
    if ncols is None:
        rest[0][...] = hn
        return
    w_ref, o_ref, p_ref = rest
    o_ref[...] = hn
    p_ref[...] = _dot(hn, _masked_weight(w_ref[...], ncols, 0, w_t), w_t)


def rmsnorm(x, g, eps, out_dtype, proj=None):
    m, d = x.shape
    tm = _tile(m, NORM_TM)
    in_specs = [pl.BlockSpec((tm, d), lambda i: (i, 0)), pl.BlockSpec((1, d), lambda i: (0, 0))]
    args = [x, g.reshape(1, d).astype(F32)]
    out_shape = jax.ShapeDtypeStruct((m, d), out_dtype)
    out_specs = pl.BlockSpec((tm, d), lambda i: (i, 0))
    ncols, w_t = None, False
    if proj is not None:
        w, col_start, ncols, w_t = proj
        n = _round_up(ncols, LANES)
        assert col_start % n == 0
        in_specs.append(pl.BlockSpec((n, d), lambda i: (col_start // n, 0)) if w_t
                        else pl.BlockSpec((d, n), lambda i: (0, col_start // n)))
        args.append(w)
        out_shape = (out_shape, jax.ShapeDtypeStruct((m, n), F32))
        out_specs = (out_specs, pl.BlockSpec((tm, n), lambda i: (i, 0)))
    return pl.pallas_call(
        functools.partial(_rmsnorm_kernel, eps=eps, ncols=ncols, w_t=w_t),
        out_shape=out_shape,
        grid=(m // tm,),
        in_specs=in_specs,
        out_specs=out_specs,
        compiler_params=_params("parallel"),
        name="rmsnorm",
    )(*args)


def _masked_weight(w, ncols, start, w_t):
    axis = 0 if w_t else 1
    if ncols % w.shape[axis] == 0:
        return w
    out_col = lax.broadcasted_iota(jnp.int32, w.shape, axis) + start
    return jnp.where(out_col < ncols, w, jnp.zeros_like(w))


def _dot(x, w, w_t):
    dims = (((1,), (1,)), ((), ())) if w_t else (((1,), (0,)), ((), ()))
    return lax.dot_general(x, w, dims, preferred_element_type=F32)


def _mm_kernel(*refs, nk, has_res, sides, ncols, w_t):
    n_reg = 3 if has_res else 2
    step = (pl.program_id(0) * pl.num_programs(1) + pl.program_id(1)) * nk + pl.program_id(2)
    for idx, s in enumerate(sides):
        _side_cast(refs[n_reg + idx], refs[n_reg + len(sides) + 1 + idx], step, s)
    x_ref, w_ref = refs[:2]
    r_ref = refs[2] if has_res else None
    o_ref = refs[n_reg + len(sides)]
    tn = o_ref.shape[1]
    part = _dot(x_ref[...], _masked_weight(w_ref[...], ncols, pl.program_id(1) * tn, w_t), w_t)

    def finish(acc):
        if has_res:
            acc = acc + r_ref[...]
        o_ref[...] = acc.astype(o_ref.dtype)

    if nk == 1:
        finish(part)
        return
    acc_ref = refs[-1]
    k = pl.program_id(2)

    @pl.when(k == 0)
    def _():
        acc_ref[...] = part

    @pl.when(jnp.logical_and(k > 0, k < nk - 1))
    def _():
        acc_ref[...] += part

    @pl.when(k == nk - 1)
    def _():
        finish(acc_ref[...] + part)


def matmul(x, w, res=None, out_dtype=F32, sides=(), col_start=0, ncols=None, w_t=False):
    m, kdim = x.shape
    ncols = w.shape[0 if w_t else 1] - col_start if ncols is None else ncols
    n = _round_up(ncols, LANES)
    tm, tn = _tile(m, MM_TM), _tile(n, MM_TN)
    assert col_start % tn == 0
    cb0 = col_start // tn
    tk = kdim
    if kdim > MM_TK_MAX:
        tk = MM_TK_SPLIT
        while kdim % tk:
            tk -= 2 * LANES
    nk = kdim // tk
    w_spec = (pl.BlockSpec((tn, tk), lambda i, j, k: (cb0 + j, k)) if w_t
              else pl.BlockSpec((tk, tn), lambda i, j, k: (k, cb0 + j)))
    in_specs = [pl.BlockSpec((tm, tk), lambda i, j, k: (i, k)), w_spec]
    args = [x, w]
    if res is not None:
        in_specs.append(pl.BlockSpec((tm, tn), lambda i, j, k: (i, j)))
        args.append(res)
    grid = (m // tm, n // tn, nk)
    out_shape = jax.ShapeDtypeStruct((m, n), out_dtype)
    out_specs = pl.BlockSpec((tm, tn), lambda i, j, k: (i, j))
    nj = grid[1]
    setups = [_side_setup(s, math.prod(grid), lambda i, j, k: (i * nj + j) * nk + k) for s in sides]
    for setup in setups:
        in_specs += setup["in_specs"]
        args += setup["args"]
    if sides:
        out_shape = (out_shape, *(s["out_shape"] for s in setups))
        out_specs = (out_specs, *(s["out_spec"] for s in setups))
    return pl.pallas_call(
        functools.partial(_mm_kernel, nk=nk, has_res=res is not None,
                          sides=tuple(s["static"] for s in setups), ncols=ncols, w_t=w_t),
        out_shape=out_shape,
        grid=grid,
        in_specs=in_specs,
        out_specs=out_specs,
        scratch_shapes=[pltpu.VMEM((tm, tn), F32)] if nk > 1 else [],
        compiler_params=_params("arbitrary", "arbitrary", "arbitrary"),
        name="matmul",
    )(*args)


def _causal_conv_rows(p, halo, w, b):
    kw = w.shape[0]
    hr = halo.shape[0]
    top = p[0:hr]
    u = b + w[kw - 1:kw] * p
    ut = b + w[kw - 1:kw] * top
    row = lax.broadcasted_iota(jnp.int32, top.shape, 0)
    for k in range(1, kw):
        wk = w[kw - 1 - k:kw - k]
        u = u + wk * pltpu.roll(p, k, 0)
        shifted_top = jnp.where(row < k, pltpu.roll(halo, k, 0), pltpu.roll(top, k, 0))
        ut = ut + wk * shifted_top
    return jnp.concatenate([ut, u[hr:]], axis=0)


def _silu(x):
    return x / (1.0 + jnp.exp(-x))


def _ffn_up_kernel(x_ref, wgl_ref, wgh_ref, wvl_ref, wvh_ref, cwg_ref, cwv_ref, cbg_ref, cbv_ref,
                   side_src, o_ref, side_dst, halo_g, halo_v, *, tiles_per_seq, side, last_is_half):
    j = pl.program_id(0)
    i = pl.program_id(1)
    _side_cast(side_src, side_dst, j * pl.num_programs(1) + i, side)

    @pl.when(i % tiles_per_seq == 0)
    def _():
        halo_g[...] = jnp.zeros_like(halo_g)
        halo_v[...] = jnp.zeros_like(halo_v)

    tn = o_ref.shape[1]

    def tile(cols):
        x = x_ref[...]
        if cols == tn:
            wg = jnp.concatenate([wgl_ref[...], wgh_ref[...]], axis=1)
            wv = jnp.concatenate([wvl_ref[...], wvh_ref[...]], axis=1)
        else:
            wg, wv = wgl_ref[...], wvl_ref[...]
            o_ref[:, cols:] = jnp.zeros((o_ref.shape[0], tn - cols), o_ref.dtype)
        pg = jnp.dot(x, wg, preferred_element_type=F32)
        pv = jnp.dot(x, wv, preferred_element_type=F32)
        ug = _causal_conv_rows(pg, halo_g[:, :cols], cwg_ref[:, :cols], cbg_ref[:, :cols])
        uv = _causal_conv_rows(pv, halo_v[:, :cols], cwv_ref[:, :cols], cbv_ref[:, :cols])
        hr = halo_g.shape[0]
        halo_g[:, :cols] = pg[pg.shape[0] - hr:]
        halo_v[:, :cols] = pv[pv.shape[0] - hr:]
        o_ref[:, :cols] = (_silu(ug) * uv).astype(o_ref.dtype)

    if not last_is_half:
        tile(tn)
        return
    last = pl.num_programs(0) - 1

    @pl.when(j < last)
    def _():
        tile(tn)

    @pl.when(j == last)
    def _():
        tile(tn // 2)


def ffn_up(x, w_up, cwg, cwv, cbg, cbv, seq, side):
    m, d = x.shape
    f = w_up.shape[1] // 2
    fp = cwg.shape[1]
    tm, tn = _tile(seq, FFN_TM), _tile(fp, FFN_TN)
    half = tn // 2
    assert f % half == 0
    nb, last = f // half, 2 * f // half - 1
    kw = cwg.shape[0]
    wspec = lambda off: pl.BlockSpec((d, half), lambda j, i: (0, jnp.minimum(2 * j + off, last)))
    cspec = pl.BlockSpec((kw, tn), lambda j, i: (0, j))
    bspec = pl.BlockSpec((1, tn), lambda j, i: (0, j))
    nj, ni = fp // tn, m // tm
    setup = _side_setup(side, nj * ni, lambda j, i: j * ni + i)
    return pl.pallas_call(
        functools.partial(_ffn_up_kernel, tiles_per_seq=seq // tm, side=setup["static"],
                          last_is_half=fp - f == half),
        out_shape=(jax.ShapeDtypeStruct((m, fp), BF16), setup["out_shape"]),
        grid=(nj, ni),
        in_specs=[pl.BlockSpec((tm, d), lambda j, i: (i, 0)),
                  wspec(0), wspec(1), wspec(nb), wspec(nb + 1), cspec, cspec, bspec, bspec] + setup["in_specs"],
        out_specs=(pl.BlockSpec((tm, tn), lambda j, i: (i, j)), setup["out_spec"]),
        scratch_shapes=[pltpu.VMEM((BF16_SUBLANES, tn), F32), pltpu.VMEM((BF16_SUBLANES, tn), F32)],
        compiler_params=_params("arbitrary", "arbitrary"),
        name="ffn_up",
    )(x, w_up, w_up, w_up, w_up, cwg, cwv, cbg, cbv, side.src)


def _cumsum_rows(y):
    n = y.shape[0]
    row = lax.broadcasted_iota(jnp.int32, y.shape, 0)
    shift = 1
    while shift < n:
        y = y + jnp.where(row >= shift, pltpu.roll(y, shift, 0), 0.0)
        shift *= 2
    return y


def _fox_gate_kernel(f_ref, b_ref, o_ref):
    x = f_ref[...] + b_ref[...]
    log_f = jnp.minimum(x, 0.0) - jnp.log(1.0 + jnp.exp(-jnp.abs(x)))
    o_ref[...] = _cumsum_rows(log_f)


def fox_gate_cumsum(f_logit, b_f, seq):
    m, hp = f_logit.shape
    return pl.pallas_call(
        _fox_gate_kernel,
        out_shape=jax.ShapeDtypeStruct((m, hp), F32),
        grid=(m // seq,),
        in_specs=[pl.BlockSpec((seq, hp), lambda b: (b, 0)), pl.BlockSpec((1, hp), lambda b: (0, 0))],
        out_specs=pl.BlockSpec((seq, hp), lambda b: (b, 0)),
        compiler_params=_params("parallel"),
        name="fox_gate_cumsum",
    )(f_logit, b_f)


def _bias_lanes(x, pieces_first):
    hi = x.astype(BF16).astype(F32)
    rem = x - hi
    lo = rem.astype(BF16).astype(F32)
    lo2 = rem - lo
    lane = lax.broadcasted_iota(jnp.int32, (x.shape[0], LANES), 1)
    base = 0 if pieces_first else 3
    ones = 3 if pieces_first else 0
    out = jnp.where(lane == base, hi, jnp.where(lane == base + 1, lo, jnp.where(lane == base + 2, lo2, 0.0)))
    out = jnp.where(jnp.logical_and(lane >= ones, lane < ones + 3), 1.0, out)
    return out.astype(BF16)


def _qk(q, k):
    return lax.dot_general(q, k, (((1,), (1,)), ((), ())), preferred_element_type=F32)


def _online_softmax_step(s, v, carry):
    m, l, acc = carry
    m_new = jnp.maximum(m, jnp.max(s, axis=-1, keepdims=True))
    alpha = jnp.exp2(m - m_new)
    p = jnp.exp2(s - m_new)
    l = alpha * l + jnp.sum(p, axis=-1, keepdims=True)
    acc = alpha * acc + jnp.dot(p.astype(v.dtype), v, preferred_element_type=F32)
    return m_new, l, acc


def _softmax_init(tq, ev):
    return (jnp.full((tq, 1), -jnp.inf, F32), jnp.zeros((tq, 1), F32), jnp.zeros((tq, ev), F32))


def _causal_mask(s):
    r = lax.broadcasted_iota(jnp.int32, s.shape, 0)
    c = lax.broadcasted_iota(jnp.int32, s.shape, 1)
    return jnp.where(c <= r, s, -jnp.inf)


def _fox_attn_kernel(q_ref, k_ref, v_ref, cum_ref, o_ref, kaug_ref, *, tq, hd):
    hg = pl.program_id(1)
    qi = pl.program_id(2)
    nh = k_ref.shape[-1] // hd
    nq = k_ref.shape[0] // tq

    def gate(rows, a):
        lane = lax.broadcasted_iota(jnp.int32, rows.shape, 1)
        return jnp.sum(jnp.where(lane == hg * nh + a, rows, 0.0), axis=-1, keepdims=True) * LOG2E

    @pl.when(qi == 0)
    def _():
        for c in range(nq):
            sl = slice(c * tq, (c + 1) * tq)
            rows = cum_ref[sl, :]
            for a in range(nh):
                kaug_ref[a, sl, :hd] = k_ref[sl, a * hd:(a + 1) * hd]
                kaug_ref[a, sl, hd:] = _bias_lanes(-gate(rows, a), True)

    q_start = pl.multiple_of(qi * tq, tq)
    q_rows = cum_ref[pl.ds(q_start, tq), :]
    q = q_ref[...]
    qs = [jnp.concatenate([q[:, a * hd:(a + 1) * hd], _bias_lanes(gate(q_rows, a), False)], axis=1)
          for a in range(nh)]

    def tile(j, a):
        start = pl.multiple_of(j * tq, tq)
        return _qk(qs[a], kaug_ref[a, pl.ds(start, tq), :]), v_ref[pl.ds(start, tq), a * hd:(a + 1) * hd]

    def body(j, carry):
        return tuple(_online_softmax_step(*tile(j, a), carry[a]) for a in range(nh))

    carry = lax.fori_loop(0, qi, body, (_softmax_init(tq, hd),) * nh)
    for a in range(nh):
        s, v = tile(qi, a)
        _, l, acc = _online_softmax_step(_causal_mask(s), v, carry[a])
        o_ref[:, a * hd:(a + 1) * hd] = (acc / l).astype(o_ref.dtype)


def fox_attention(qkv, cum, batch, seq, heads, hd):
    assert hd == LANES
    m = qkv.shape[0]
    tq = _tile(seq, ATTN_TQ)
    nq = seq // tq
    hp = cum.shape[1]
    nh = FOX_HEADS_PER_STEP
    hg = heads // nh
    w = nh * hd
    return pl.pallas_call(
        functools.partial(_fox_attn_kernel, tq=tq, hd=hd),
        out_shape=jax.ShapeDtypeStruct((m, heads * hd), BF16),
        grid=(batch, hg, nq),
        in_specs=[pl.BlockSpec((tq, w), lambda b, h, i: (b * nq + i, h)),
                  pl.BlockSpec((seq, w), lambda b, h, i: (b, hg + h)),
                  pl.BlockSpec((seq, w), lambda b, h, i: (b, 2 * hg + h)),
                  pl.BlockSpec((seq, hp), lambda b, h, i: (b, 0))],
        out_specs=pl.BlockSpec((tq, w), lambda b, h, i: (b * nq + i, h)),
        scratch_shapes=[pltpu.VMEM((nh, seq, hd + LANES), BF16)],
        compiler_params=_params("parallel", "parallel", "arbitrary"),
        name="fox_attention",
    )(qkv, qkv, qkv, cum)


def _diff_attn_kernel(slope_ref, q_ref, k_ref, v_ref, lam_ref, g_ref, o_ref, kaug_ref, *,
                      tq, hd, lambda_init, eps):
    h = pl.program_id(1)
    qi = pl.program_id(2)
    slope2 = slope_ref[h] * LOG2E
    nq = k_ref.shape[0] // tq

    def alibi(start):
        return slope2 * (lax.broadcasted_iota(jnp.int32, (tq, 1), 0) + start).astype(F32)

    @pl.when(qi == 0)
    def _():
        for c in range(nq):
            sl = slice(c * tq, (c + 1) * tq)
            kx = _bias_lanes(alibi(c * tq), True)
            for a in range(2):
                kaug_ref[a, sl, :hd] = k_ref[sl, a * hd:(a + 1) * hd]
                kaug_ref[a, sl, hd:] = kx

    qx = _bias_lanes(-alibi(qi * tq), False)
    q = q_ref[...]
    qs = [jnp.concatenate([q[:, a * hd:(a + 1) * hd], qx], axis=1) for a in range(2)]

    ev = v_ref.shape[-1]

    def tile(j):
        start = pl.multiple_of(j * tq, tq)
        return [_qk(qs[a], kaug_ref[a, pl.ds(start, tq), :]) for a in range(2)], v_ref[pl.ds(start, tq), :]

    def body(j, carry):
        ss, v = tile(j)
        return tuple(_online_softmax_step(ss[a], v, carry[a]) for a in range(2))

    one = _softmax_init(tq, ev)
    carry = lax.fori_loop(0, qi, body, (one, one))
    ss, v = tile(qi)
    outs = []
    for a in range(2):
        _, l, acc = _online_softmax_step(_causal_mask(ss[a]), v, carry[a])
        outs.append(acc / l)
    lam = lam_ref[...]
    lam_full = (jnp.exp(jnp.sum(lam[0:1] * lam[1:2], axis=-1, keepdims=True))
                - jnp.exp(jnp.sum(lam[2:3] * lam[3:4], axis=-1, keepdims=True)) + lambda_init)
    out = outs[0] - lam_full * outs[1]
    ms = jnp.mean(out * out, axis=-1, keepdims=True)
    y = (out * lax.rsqrt(ms + eps)) * g_ref[...]
    o_ref[...] = (y * (1.0 - lambda_init)).astype(o_ref.dtype)


def diff_attention(qkv, lam, subln_g, batch, seq, heads, hd, lambda_init):
    assert hd == LANES
    m = qkv.shape[0]
    tq = _tile(seq, ATTN_TQ)
    nq = seq // tq
    ev = 2 * hd
    slopes = jnp.exp2(-8.0 * jnp.arange(1, heads + 1, dtype=F32) / heads)
    grid_spec = pltpu.PrefetchScalarGridSpec(
        num_scalar_prefetch=1,
        grid=(batch, heads, nq),
        in_specs=[pl.BlockSpec((tq, ev), lambda b, h, i, s: (b * nq + i, h)),
                  pl.BlockSpec((seq, ev), lambda b, h, i, s: (b, heads + h)),
                  pl.BlockSpec((seq, ev), lambda b, h, i, s: (b, 2 * heads + h)),
                  pl.BlockSpec((4, hd), lambda b, h, i, s: (0, 0)),
                  pl.BlockSpec((1, ev), lambda b, h, i, s: (0, 0))],
        out_specs=pl.BlockSpec((tq, ev), lambda b, h, i, s: (b * nq + i, h)),
        scratch_shapes=[pltpu.VMEM((2, seq, hd + LANES), BF16)],
    )
    return pl.pallas_call(
        functools.partial(_diff_attn_kernel, tq=tq, hd=hd, lambda_init=lambda_init, eps=DIFF_SUBLN_EPS),
        out_shape=jax.ShapeDtypeStruct((m, heads * ev), BF16),
        grid_spec=grid_spec,
        compiler_params=_params("parallel", "parallel", "arbitrary"),
        name="diff_attention",
    )(slopes, qkv, qkv, qkv, lam.astype(F32), subln_g.reshape(1, ev).astype(F32))


def _ssm_conv_kernel(x_ref, halo_ref, w_ref, b_ref, o_ref):
    i = pl.program_id(1)
    halo = jnp.where(i == 0, 0.0, halo_ref[...].astype(F32))
    u = _causal_conv_rows(x_ref[...].astype(F32), halo, w_ref[...], b_ref[...])
    o_ref[...] = _silu(u).astype(o_ref.dtype)


def ssm_conv_silu(zxbc, col_start, w, b, batch, seq):
    m = zxbc.shape[0]
    cdim = w.shape[1]
    ts, tc = _tile(seq, CONV_TS), _tile(cdim, CONV_TC)
    assert col_start % tc == 0
    cb0 = col_start // tc
    ns = seq // ts
    hb = ts // BF16_SUBLANES
    return pl.pallas_call(
        _ssm_conv_kernel,
        out_shape=jax.ShapeDtypeStruct((m, cdim), BF16),
        grid=(batch, ns, cdim // tc),
        in_specs=[pl.BlockSpec((ts, tc), lambda bi, i, j: (bi * ns + i, cb0 + j)),
                  pl.BlockSpec((BF16_SUBLANES, tc),
                               lambda bi, i, j: (jnp.maximum((bi * ns + i) * hb - 1, 0), cb0 + j)),
                  pl.BlockSpec((w.shape[0], tc), lambda bi, i, j: (0, j)),
                  pl.BlockSpec((1, tc), lambda bi, i, j: (0, j))],
        out_specs=pl.BlockSpec((ts, tc), lambda bi, i, j: (bi * ns + i, j)),
        compiler_params=_params("parallel", "parallel", "parallel"),
        name="ssm_conv_silu",
    )(zxbc, zxbc, w.astype(F32), b.reshape(1, cdim).astype(F32))


def _ssm_dt_kernel(dt_ref, bias_ref, alog_ref, dt_out, la_out):
    x = dt_ref[...] + bias_ref[...]
    dt = jnp.maximum(x, 0.0) + jnp.log(1.0 + jnp.exp(-jnp.abs(x)))
    dt_out[...] = dt
    la_out[...] = _cumsum_rows(dt * (-jnp.exp(alog_ref[...]))) * LOG2E


def ssm_dt(dt_raw, dt_bias, a_log):
    m, h = dt_raw.shape
    spec = pl.BlockSpec((SSM_CHUNK, h), lambda i: (i, 0))
    pspec = pl.BlockSpec((1, h), lambda i: (0, 0))
    return pl.pallas_call(
        _ssm_dt_kernel,
        out_shape=(jax.ShapeDtypeStruct((m, h), F32), jax.ShapeDtypeStruct((m, h), F32)),
        grid=(m // SSM_CHUNK,),
        in_specs=[spec, pspec, pspec],
        out_specs=(spec, spec),
        compiler_params=_params("parallel"),
        name="ssm_dt",
    )(dt_raw, dt_bias.reshape(1, h).astype(F32), a_log.reshape(1, h).astype(F32))


def _ssd_scan_kernel(x_ref, b_ref, c_ref, dt_ref, la_ref, lat_ref, z_ref, d_ref, g_ref, o_ref,
                     state_ref, *, hpg, pdim, eps):
    ci = pl.program_id(2)

    @pl.when(ci == 0)
    def _():
        state_ref[...] = jnp.zeros_like(state_ref)

    gw = hpg * pdim
    nstate = b_ref.shape[1] // dt_ref.shape[0]
    for gi in range(dt_ref.shape[0]):
        cols = slice(gi * gw, (gi + 1) * gw)
        ncols = slice(gi * nstate, (gi + 1) * nstate)
        _ssd_group_chunk(x_ref.at[:, cols], b_ref.at[:, ncols], c_ref.at[:, ncols], dt_ref.at[gi],
                         la_ref.at[gi], lat_ref.at[gi], z_ref.at[:, cols], d_ref.at[:, cols],
                         g_ref.at[:, cols], o_ref.at[:, cols], state_ref.at[gi], hpg, pdim, eps)


def _ssd_group_chunk(x_ref, b_ref, c_ref, dt_ref, la_ref, lat_ref, z_ref, d_ref, g_ref, o_ref, state_ref,
                     hpg, pdim, eps):
    chunk = x_ref.shape[0]
    npair = hpg // 2
    x = x_ref[...].astype(F32)
    bm = b_ref[...]
    cm = c_ref[...]
    dt = dt_ref[...]
    la = la_ref[...]
    lat = lat_ref[...]
    lane = lax.broadcasted_iota(jnp.int32, (chunk, 2 * pdim), 1)
    first = lane < pdim

    def expand(a):
        lo = lax.broadcasted_iota(jnp.int32, (a.shape[0], 2 * pdim), 1) < pdim
        return jnp.concatenate(
            [jnp.where(lo, a[:, 2 * p:2 * p + 1], a[:, 2 * p + 1:2 * p + 2]) for p in range(npair)], axis=1)

    dt_e = expand(dt)
    la_e = expand(la)
    la_end_e = expand(la[chunk - 1:chunk])
    cb = _qk(cm, bm)
    tri = (lax.broadcasted_iota(jnp.int32, (chunk, chunk), 0)
           >= lax.broadcasted_iota(jnp.int32, (chunk, chunk), 1))
    xdt = x * dt_e
    y_parts = []
    for p in range(npair):
        ms = []
        for hh in (2 * p, 2 * p + 1):
            seg = la[:, hh:hh + 1] - lat[hh:hh + 1, :]
            ms.append((cb * jnp.exp2(jnp.where(tri, seg, -jnp.inf))).astype(BF16))
        xp = xdt[:, p * 2 * pdim:(p + 1) * 2 * pdim]
        rhs = jnp.concatenate([jnp.where(first, xp, 0.0), jnp.where(first, 0.0, xp)], axis=0).astype(BF16)
        y_parts.append(jnp.dot(jnp.concatenate(ms, axis=1), rhs, preferred_element_type=F32))
    y = jnp.concatenate(y_parts, axis=1)
    state = state_ref[...]
    y = y + jnp.dot(cm, state.astype(BF16), preferred_element_type=F32) * jnp.exp2(la_e)
    to_end = jnp.exp2(la_end_e - la_e) * dt_e
    upd = lax.dot_general(bm, (x * to_end).astype(BF16), (((0,), (0,)), ((), ())),
                          preferred_element_type=F32)
    state_ref[...] = state * jnp.exp2(la_end_e) + upd
    y = y + d_ref[...] * x
    y = y * _silu(z_ref[...].astype(F32))
    y = y * lax.rsqrt(jnp.mean(y * y, axis=-1, keepdims=True) + eps)
    o_ref[...] = (y * g_ref[...]).astype(o_ref.dtype)


def ssd_scan(xbc, z, dt_g, la_g, lat_g, d_e, norm_g, batch, seq, d_inner, heads):
    m = xbc.shape[0]
    groups, nstate, chunk = SSM_GROUPS, SSM_STATE, SSM_CHUNK
    hpg = heads // groups
    pdim = d_inner // heads
    gps = math.gcd(groups, SSD_GROUPS_PER_STEP)
    gw = gps * hpg * pdim
    nw = gps * nstate
    nc = seq // chunk
    xb = d_inner // nw
    row = lambda b, g, c: b * nc + c
    return pl.pallas_call(
        functools.partial(_ssd_scan_kernel, hpg=hpg, pdim=pdim, eps=SSM_NORM_EPS),
        out_shape=jax.ShapeDtypeStruct((m, d_inner), BF16),
        grid=(batch, groups // gps, nc),
        in_specs=[pl.BlockSpec((chunk, gw), lambda b, g, c: (row(b, g, c), g)),
                  pl.BlockSpec((chunk, nw), lambda b, g, c: (row(b, g, c), xb + g)),
                  pl.BlockSpec((chunk, nw), lambda b, g, c: (row(b, g, c), xb + groups // gps + g)),
                  pl.BlockSpec((gps, chunk, hpg), lambda b, g, c: (g, row(b, g, c), 0)),
                  pl.BlockSpec((gps, chunk, hpg), lambda b, g, c: (g, row(b, g, c), 0)),
                  pl.BlockSpec((gps, hpg, chunk), lambda b, g, c: (g, 0, row(b, g, c))),
                  pl.BlockSpec((chunk, gw), lambda b, g, c: (row(b, g, c), g)),
                  pl.BlockSpec((1, gw), lambda b, g, c: (0, g)),
                  pl.BlockSpec((1, gw), lambda b, g, c: (0, g))],
        out_specs=pl.BlockSpec((chunk, gw), lambda b, g, c: (row(b, g, c), g)),
        scratch_shapes=[pltpu.VMEM((gps, nstate, gw // gps), F32)],
        compiler_params=_params("parallel", "parallel", "arbitrary"),
        name="ssd_scan",
    )(xbc, xbc, xbc, dt_g, la_g, lat_g, z, d_e, norm_g)


def in_proj_cast(kind, layer, d, fox_w_in_t, fox_heads, ssm_w_in, diff_w_in, diff_hd):
    q_scale = lambda hd: (hd ** -0.5) * LOG2E
    if kind == 0:
        return SideCast(fox_w_in_t, layer, _round_up(fox_w_in_t.shape[1], BF16_SUBLANES),
                        (0, d, q_scale(d // fox_heads)))
    if kind == 1:
        return SideCast(ssm_w_in, layer, d, None)
    return SideCast(diff_w_in, layer, d, (1, d, q_scale(diff_hd)))


def fox_mixer(h, hn, f_logit, layer, w_in, b_f, w_o, batch, seq, side):
    d = hn.shape[1]
    heads = b_f.shape[0]
    hd = d // heads
    qkv, side_out, wo = matmul(hn, w_in, out_dtype=BF16, ncols=3 * d, w_t=True,
                               sides=(side, SideCast(w_o, layer, w_o.shape[1], None)))
    hp = f_logit.shape[1]
    b_pad = jnp.pad(b_f.reshape(1, heads).astype(F32), ((0, 0), (0, hp - heads)))
    cum = fox_gate_cumsum(f_logit, b_pad, seq)
    attn = fox_attention(qkv, cum, batch, seq, heads, hd)
    return matmul(attn, wo, res=h), side_out


def ssd_mixer(h, hn, dt_raw, layer, w_in, conv_w, conv_b, dt_bias, a_log, d_skip, norm_g, w_out, batch, seq,
              side):
    heads = a_log.shape[0]
    d_inner = w_out.shape[1]
    conv_dim = conv_w.shape[1]
    groups = SSM_GROUPS
    hpg = heads // groups
    m = hn.shape[0]
    zxbc, side_out, wo = matmul(hn, w_in, out_dtype=BF16, ncols=d_inner + conv_dim,
                                sides=(side, SideCast(w_out, layer, d_inner, None)))
    dt_raw = dt_raw[:, :heads]
    xbc = ssm_conv_silu(zxbc, d_inner, conv_w, conv_b, batch, seq)
    dt, la = ssm_dt(dt_raw, dt_bias, a_log)
    dt_g = jnp.transpose(dt.reshape(m, groups, hpg), (1, 0, 2))
    la_g = jnp.transpose(la.reshape(m, groups, hpg), (1, 0, 2))
    lat_g = jnp.transpose(la_g, (0, 2, 1))
    pdim = d_inner // heads
    d_e = jnp.repeat(d_skip.astype(F32), pdim).reshape(1, d_inner)
    y = ssd_scan(xbc, zxbc, dt_g, la_g, lat_g, d_e, norm_g.reshape(1, d_inner).astype(F32),
                 batch, seq, d_inner, heads)
    return matmul(y, wo, res=h), side_out


def diff_mixer(h, hn, layer, w_in, lam, subln_g, w_o, lambda_init, batch, seq, side):
    d = hn.shape[1]
    hd = lam.shape[1]
    heads = d // (2 * hd)
    qkv, side_out, wo = matmul(hn, w_in, out_dtype=BF16,
                               sides=(side, SideCast(w_o, layer, w_o.shape[1], None)))
    attn = diff_attention(qkv, lam, subln_g, batch, seq, heads, hd, lambda_init)
    return matmul(attn, wo, res=h), side_out


def conv_ffn(h, hn, layer, wu, conv_w, conv_b, w_down, seq, side):
    f = w_down.shape[1]
    fp = _round_up(f, FFN_TN)
    cw = conv_w.astype(F32)
    cb = conv_b.reshape(1, 2 * f).astype(F32)
    pad = lambda a: jnp.pad(a, ((0, 0), (0, fp - f)))
    act, wd = ffn_up(hn, wu, pad(cw[:, :f]), pad(cw[:, f:]), pad(cb[:, :f]), pad(cb[:, f:]), seq,
                     SideCast(w_down, layer, fp, None))
    if side is None:
        return matmul(act, wd, res=h), None
    return matmul(act, wd, res=h, sides=(side,))


def kernel(x, mix_norm_g, ffn_norm_g, fox_w_in, fox_b_f, fox_w_o, ssm_w_in, ssm_conv_w, ssm_conv_b,
           ssm_dt_bias, ssm_a_log, ssm_d, ssm_norm_g, ssm_w_out, diff_w_in, diff_lambda, diff_subln_g,
           diff_w_o, ffn_w_up, ffn_conv_w, ffn_conv_b, ffn_w_down, final_norm_g):
    batch, seq, d = x.shape
    depth = mix_norm_g.shape[0]
    h = x.reshape(batch * seq, d)

    fox_w_in_t = jnp.swapaxes(fox_w_in, 1, 2)

    def in_cast(i):
        return in_proj_cast(i % N_MIXERS, i // N_MIXERS, d, fox_w_in_t, fox_b_f.shape[1], ssm_w_in,
                            diff_w_in, diff_lambda.shape[2])

    w_in = cast_rows(in_cast(0))
    for i in range(depth):
        kind, j = i % N_MIXERS, i // N_MIXERS
        up = SideCast(ffn_w_up, i, ffn_w_up.shape[1], None)
        if kind == 0:
            hn, f_logit = rmsnorm(h, mix_norm_g[i], NORM_EPS, BF16, proj=(w_in, 3 * d, fox_b_f.shape[1], True))
            h, wu = fox_mixer(h, hn, f_logit, j, w_in, fox_b_f[j], fox_w_o, batch, seq, up)
        elif kind == 1:
            dt_start = ssm_w_out.shape[1] + ssm_conv_w.shape[2]
            hn, dt_raw = rmsnorm(h, mix_norm_g[i], NORM_EPS, BF16, proj=(w_in, dt_start, ssm_a_log.shape[1], False))
            h, wu = ssd_mixer(h, hn, dt_raw, j, w_in, ssm_conv_w[j], ssm_conv_b[j], ssm_dt_bias[j],
                              ssm_a_log[j], ssm_d[j], ssm_norm_g[j], ssm_w_out, batch, seq, up)
        else:
            hn = rmsnorm(h, mix_norm_g[i], NORM_EPS, BF16)
            lambda_init = 0.8 - 0.6 * math.exp(-0.3 * i)
            h, wu = diff_mixer(h, hn, j, w_in, diff_lambda[j], diff_subln_g[j], diff_w_o,
                               lambda_init, batch, seq, up)
        hn = rmsnorm(h, ffn_norm_g[i], NORM_EPS, BF16)
        h, w_in = conv_ffn(h, hn, i, wu, ffn_conv_w[i], ffn_conv_b[i], ffn_w_down, seq,
                           in_cast(i + 1) if i + 1 < depth else None)
    return rmsnorm(h, final_norm_g, NORM_EPS, F32).reshape(batch, seq, d)
```

```python
import collections
import functools
import math

import jax
import jax.numpy as jnp
from jax import lax
from jax.experimental import pallas as pl
from jax.experimental.pallas import tpu as pltpu

F32 = jnp.float32
BF16 = jnp.bfloat16
LOG2E = 1.4426950408889634

V7X_VMEM_LIMIT_BYTES = 56 * 1024 * 1024
LANES = 128
BF16_SUBLANES = 16

NORM_EPS = 1e-6
SSM_NORM_EPS = 1e-5
DIFF_SUBLN_EPS = 1e-5
SSM_GROUPS = 8
SSM_STATE = 128
SSM_CHUNK = 128
N_MIXERS = 3

MM_TM = 1024
MM_TN = 1024
MM_TK_MAX = 4096
MM_TK_SPLIT = 3072
FFN_TM = 1024
FFN_TN = 512
ATTN_TQ = 1024
FOX_HEADS_PER_STEP = 2
SSD_GROUPS_PER_STEP = 8
NORM_TM = 512
CONV_TS = 512
CONV_TC = 1024
CAST_STEPS = 64


def _tile(dim, pref):
    if dim <= pref:
        return dim
    t = pref
    while dim % t:
        t //= 2
    return t


def _round_up(x, m):
    return -(-x // m) * m


def _params(*sem):
    return pltpu.CompilerParams(dimension_semantics=sem, vmem_limit_bytes=V7X_VMEM_LIMIT_BYTES)


SideCast = collections.namedtuple("SideCast", "src layer out_rows qscale")


def _side_setup(side, nsteps, step_of):
    _, rows, cols = side.src.shape
    r = BF16_SUBLANES
    while side.out_rows % r or side.out_rows // r > nsteps:
        r += BF16_SUBLANES
        assert r <= side.out_rows
    nblocks = side.out_rows // r
    last_src = (rows - 1) // r
    blk = lambda *idx: jnp.minimum(step_of(*idx), nblocks - 1)
    in_spec = pl.BlockSpec((None, r, cols), lambda *idx: (side.layer, jnp.minimum(blk(*idx), last_src), 0))
    return dict(in_specs=[in_spec], args=[side.src],
                out_spec=pl.BlockSpec((r, cols), lambda *idx: (blk(*idx), 0)),
                out_shape=jax.ShapeDtypeStruct((side.out_rows, cols), BF16),
                static=(nblocks, rows, side.qscale))


def _side_cast(src_ref, dst_ref, step, static):
    nblocks, rows, qscale = static
    w = src_ref[...]
    r = dst_ref.shape[0]
    row = lax.broadcasted_iota(jnp.int32, w.shape, 0) + jnp.minimum(step, nblocks - 1) * r
    if qscale is not None:
        axis, n, c = qscale
        idx = row if axis == 0 else lax.broadcasted_iota(jnp.int32, w.shape, 1)
        w = jnp.where(idx < n, w * c, w)
    if nblocks * r != rows:
        w = jnp.where(row < rows, w, 0.0)
    dst_ref[...] = w.astype(dst_ref.dtype)


def _cast_rows_kernel(src_ref, dst_ref, *, static):
    _side_cast(src_ref, dst_ref, pl.program_id(0), static)


def cast_rows(side):
    setup = _side_setup(side, CAST_STEPS, lambda i: i)
    return pl.pallas_call(
        functools.partial(_cast_rows_kernel, static=setup["static"]),
        out_shape=setup["out_shape"],
        grid=(setup["static"][0],),
        in_specs=setup["in_specs"],
        out_specs=setup["out_spec"],
        compiler_params=_params("parallel"),
        name="cast_rows",
    )(*setup["args"])


def _rmsnorm_kernel(x_ref, g_ref, *rest, eps, ncols, w_t):
    x = x_ref[...]
    ms = jnp.mean(x * x, axis=-1, keepdims=True)
    hn = ((x * lax.rsqrt(ms + eps)) * g_ref[...]).astype(rest[-1 if ncols is None else -2].dtype)
    if ncols is None:
        rest[0][...] = hn
        return
    w_ref, o_ref, p_ref = rest
    o_ref[...] = hn
    p_ref[...] = _dot(hn, _masked_weight(w_ref[...], ncols, 0, w_t), w_t)


def rmsnorm(x, g, eps, out_dtype, proj=None):
    m, d = x.shape
    tm = _tile(m, NORM_TM)
    in_specs = [pl.BlockSpec((tm, d), lambda i: (i, 0)), pl.BlockSpec((1, d), lambda i: (0, 0))]
    args = [x, g.reshape(1, d).astype(F32)]
    out_shape = jax.ShapeDtypeStruct((m, d), out_dtype)
    out_specs = pl.BlockSpec((tm, d), lambda i: (i, 0))
    ncols, w_t = None, False
    if proj is not None:
        w, col_start, ncols, w_t = proj
        n = _round_up(ncols, LANES)
        assert col_start % n == 0
        in_specs.append(pl.BlockSpec((n, d), lambda i: (col_start // n, 0)) if w_t
                        else pl.BlockSpec((d, n), lambda i: (0, col_start // n)))
        args.append(w)
        out_shape = (out_shape, jax.ShapeDtypeStruct((m, n), F32))
        out_specs = (out_specs, pl.BlockSpec((tm, n), lambda i: (i, 0)))
    return pl.pallas_call(
        functools.partial(_rmsnorm_kernel, eps=eps, ncols=ncols, w_t=w_t),
        out_shape=out_shape,
        grid=(m // tm,),
        in_specs=in_specs,
        out_specs=out_specs,
        compiler_params=_params("parallel"),
        name="rmsnorm",
    )(*args)


def _masked_weight(w, ncols, start, w_t):
    axis = 0 if w_t else 1
    if ncols % w.shape[axis] == 0:
        return w
    out_col = lax.broadcasted_iota(jnp.int32, w.shape, axis) + start
    return jnp.where(out_col < ncols, w, jnp.zeros_like(w))


def _dot(x, w, w_t):
    dims = (((1,), (1,)), ((), ())) if w_t else (((1,), (0,)), ((), ()))
    return lax.dot_general(x, w, dims, preferred_element_type=F32)


def _mm_kernel(*refs, nk, has_res, sides, ncols, w_t):
    n_reg = 3 if has_res else 2
    step = (pl.program_id(0) * pl.num_programs(1) + pl.program_id(1)) * nk + pl.program_id(2)
    for idx, s in enumerate(sides):
        _side_cast(refs[n_reg + idx], refs[n_reg + len(sides) + 1 + idx], step, s)
    x_ref, w_ref = refs[:2]
    r_ref = refs[2] if has_res else None
    o_ref = refs[n_reg + len(sides)]
    tn = o_ref.shape[1]
    part = _dot(x_ref[...], _masked_weight(w_ref[...], ncols, pl.program_id(1) * tn, w_t), w_t)

    def finish(acc):
        if has_res:
            acc = acc + r_ref[...]
        o_ref[...] = acc.astype(o_ref.dtype)

    if nk == 1:
        finish(part)
        return
    acc_ref = refs[-1]
    k = pl.program_id(2)

    @pl.when(k == 0)
    def _():
        acc_ref[...] = part

    @pl.when(jnp.logical_and(k > 0, k < nk - 1))
    def _():
        acc_ref[...] += part

    @pl.when(k == nk - 1)
    def _():
        finish(acc_ref[...] + part)


def matmul(x, w, res=None, out_dtype=F32, sides=(), col_start=0, ncols=None, w_t=False):
    m, kdim = x.shape
    ncols = w.shape[0 if w_t else 1] - col_start if ncols is None else ncols
    n = _round_up(ncols, LANES)
    tm, tn = _tile(m, MM_TM), _tile(n, MM_TN)
    assert col_start % tn == 0
    cb0 = col_start // tn
    tk = kdim
    if kdim > MM_TK_MAX:
        tk = MM_TK_SPLIT
        while kdim % tk:
            tk -= 2 * LANES
    nk = kdim // tk
    w_spec = (pl.BlockSpec((tn, tk), lambda i, j, k: (cb0 + j, k)) if w_t
              else pl.BlockSpec((tk, tn), lambda i, j, k: (k, cb0 + j)))
    in_specs = [pl.BlockSpec((tm, tk), lambda i, j, k: (i, k)), w_spec]
    args = [x, w]
    if res is not None:
        in_specs.append(pl.BlockSpec((tm, tn), lambda i, j, k: (i, j)))
        args.append(res)
    grid = (m // tm, n // tn, nk)
    out_shape = jax.ShapeDtypeStruct((m, n), out_dtype)
    out_specs = pl.BlockSpec((tm, tn), lambda i, j, k: (i, j))
    nj = grid[1]
    setups = [_side_setup(s, math.prod(grid), lambda i, j, k: (i * nj + j) * nk + k) for s in sides]
    for setup in setups:
        in_specs += setup["in_specs"]
        args += setup["args"]
    if sides:
        out_shape = (out_shape, *(s["out_shape"] for s in setups))
        out_specs = (out_specs, *(s["out_spec"] for s in setups))
    return pl.pallas_call(
        functools.partial(_mm_kernel, nk=nk, has_res=res is not None,
                          sides=tuple(s["static"] for s in setups), ncols=ncols, w_t=w_t),
        out_shape=out_shape,
        grid=grid,
        in_specs=in_specs,
        out_specs=out_specs,
        scratch_shapes=[pltpu.VMEM((tm, tn), F32)] if nk > 1 else [],
        compiler_params=_params("arbitrary", "arbitrary", "arbitrary"),
        name="matmul",
    )(*args)


def _causal_conv_rows(p, halo, w, b):
    kw = w.shape[0]
    hr = halo.shape[0]
    top = p[0:hr]
    u = b + w[kw - 1:kw] * p
    ut = b + w[kw - 1:kw] * top
    row = lax.broadcasted_iota(jnp.int32, top.shape, 0)
    for k in range(1, kw):
        wk = w[kw - 1 - k:kw - k]
        u = u + wk * pltpu.roll(p, k, 0)
        shifted_top = jnp.where(row < k, pltpu.roll(halo, k, 0), pltpu.roll(top, k, 0))
        ut = ut + wk * shifted_top
    return jnp.concatenate([ut, u[hr:]], axis=0)


def _silu(x):
    return x / (1.0 + jnp.exp(-x))


def _ffn_up_kernel(x_ref, wgl_ref, wgh_ref, wvl_ref, wvh_ref, cwg_ref, cwv_ref, cbg_ref, cbv_ref,
                   side_src, o_ref, side_dst, halo_g, halo_v, *, tiles_per_seq, side, last_is_half):
    j = pl.program_id(0)
    i = pl.program_id(1)
    _side_cast(side_src, side_dst, j * pl.num_programs(1) + i, side)

    @pl.when(i % tiles_per_seq == 0)
    def _():
        halo_g[...] = jnp.zeros_like(halo_g)
        halo_v[...] = jnp.zeros_like(halo_v)

    tn = o_ref.shape[1]

    def tile(cols):
        x = x_ref[...]
        if cols == tn:
            wg = jnp.concatenate([wgl_ref[...], wgh_ref[...]], axis=1)
            wv = jnp.concatenate([wvl_ref[...], wvh_ref[...]], axis=1)
        else:
            wg, wv = wgl_ref[...], wvl_ref[...]
            o_ref[:, cols:] = jnp.zeros((o_ref.shape[0], tn - cols), o_ref.dtype)
        pg = jnp.dot(x, wg, preferred_element_type=F32)
        pv = jnp.dot(x, wv, preferred_element_type=F32)
        ug = _causal_conv_rows(pg, halo_g[:, :cols], cwg_ref[:, :cols], cbg_ref[:, :cols])
        uv = _causal_conv_rows(pv, halo_v[:, :cols], cwv_ref[:, :cols], cbv_ref[:, :cols])
        hr = halo_g.shape[0]
        halo_g[:, :cols] = pg[pg.shape[0] - hr:]
        halo_v[:, :cols] = pv[pv.shape[0] - hr:]
        o_ref[:, :cols] = (_silu(ug) * uv).astype(o_ref.dtype)

    if not last_is_half:
        tile(tn)
        return
    last = pl.num_programs(0) - 1

    @pl.when(j < last)
    def _():
        tile(tn)

    @pl.when(j == last)
    def _():
        tile(tn // 2)


def ffn_up(x, w_up, cwg, cwv, cbg, cbv, seq, side):
    m, d = x.shape
    f = w_up.shape[1] // 2
    fp = cwg.shape[1]
    tm, tn = _tile(seq, FFN_TM), _tile(fp, FFN_TN)
    half = tn // 2
    assert f % half == 0
    nb, last = f // half, 2 * f // half - 1
    kw = cwg.shape[0]
    wspec = lambda off: pl.BlockSpec((d, half), lambda j, i: (0, jnp.minimum(2 * j + off, last)))
    cspec = pl.BlockSpec((kw, tn), lambda j, i: (0, j))
    bspec = pl.BlockSpec((1, tn), lambda j, i: (0, j))
    nj, ni = fp // tn, m // tm
    setup = _side_setup(side, nj * ni, lambda j, i: j * ni + i)
    return pl.pallas_call(
        functools.partial(_ffn_up_kernel, tiles_per_seq=seq // tm, side=setup["static"],
                          last_is_half=fp - f == half),
        out_shape=(jax.ShapeDtypeStruct((m, fp), BF16), setup["out_shape"]),
        grid=(nj, ni),
        in_specs=[pl.BlockSpec((tm, d), lambda j, i: (i, 0)),
                  wspec(0), wspec(1), wspec(nb), wspec(nb + 1), cspec, cspec, bspec, bspec] + setup["in_specs"],
        out_specs=(pl.BlockSpec((tm, tn), lambda j, i: (i, j)), setup["out_spec"]),
        scratch_shapes=[pltpu.VMEM((BF16_SUBLANES, tn), F32), pltpu.VMEM((BF16_SUBLANES, tn), F32)],
        compiler_params=_params("arbitrary", "arbitrary"),
        name="ffn_up",
    )(x, w_up, w_up, w_up, w_up, cwg, cwv, cbg, cbv, side.src)


def _cumsum_rows(y):
    n = y.shape[0]
    row = lax.broadcasted_iota(jnp.int32, y.shape, 0)
    shift = 1
    while shift < n:
        y = y + jnp.where(row >= shift, pltpu.roll(y, shift, 0), 0.0)
        shift *= 2
    return y


def _fox_gate_kernel(f_ref, b_ref, o_ref):
    x = f_ref[...] + b_ref[...]
    log_f = jnp.minimum(x, 0.0) - jnp.log(1.0 + jnp.exp(-jnp.abs(x)))
    o_ref[...] = _cumsum_rows(log_f)


def fox_gate_cumsum(f_logit, b_f, seq):
    m, hp = f_logit.shape
    return pl.pallas_call(
        _fox_gate_kernel,
        out_shape=jax.ShapeDtypeStruct((m, hp), F32),
        grid=(m // seq,),
        in_specs=[pl.BlockSpec((seq, hp), lambda b: (b, 0)), pl.BlockSpec((1, hp), lambda b: (0, 0))],
        out_specs=pl.BlockSpec((seq, hp), lambda b: (b, 0)),
        compiler_params=_params("parallel"),
        name="fox_gate_cumsum",
    )(f_logit, b_f)


def _bias_lanes(x, pieces_first):
    hi = x.astype(BF16).astype(F32)
    rem = x - hi
    lo = rem.astype(BF16).astype(F32)
    lo2 = rem - lo
    lane = lax.broadcasted_iota(jnp.int32, (x.shape[0], LANES), 1)
    base = 0 if pieces_first else 3
    ones = 3 if pieces_first else 0
    out = jnp.where(lane == base, hi, jnp.where(lane == base + 1, lo, jnp.where(lane == base + 2, lo2, 0.0)))
    out = jnp.where(jnp.logical_and(lane >= ones, lane < ones + 3), 1.0, out)
    return out.astype(BF16)


def _qk(q, k):
    return lax.dot_general(q, k, (((1,), (1,)), ((), ())), preferred_element_type=F32)


def _online_softmax_step(s, v, carry):
    m, l, acc = carry
    m_new = jnp.maximum(m, jnp.max(s, axis=-1, keepdims=True))
    alpha = jnp.exp2(m - m_new)
    p = jnp.exp2(s - m_new)
    l = alpha * l + jnp.sum(p, axis=-1, keepdims=True)
    acc = alpha * acc + jnp.dot(p.astype(v.dtype), v, preferred_element_type=F32)
    return m_new, l, acc


def _softmax_init(tq, ev):
    return (jnp.full((tq, 1), -jnp.inf, F32), jnp.zeros((tq, 1), F32), jnp.zeros((tq, ev), F32))


def _causal_mask(s):
    r = lax.broadcasted_iota(jnp.int32, s.shape, 0)
    c = lax.broadcasted_iota(jnp.int32, s.shape, 1)
    return jnp.where(c <= r, s, -jnp.inf)


def _fox_attn_kernel(q_ref, k_ref, v_ref, cum_ref, o_ref, kaug_ref, *, tq, hd):
    hg = pl.program_id(1)
    qi = pl.program_id(2)
    nh = k_ref.shape[-1] // hd
    nq = k_ref.shape[0] // tq

    def gate(rows, a):
        lane = lax.broadcasted_iota(jnp.int32, rows.shape, 1)
        return jnp.sum(jnp.where(lane == hg * nh + a, rows, 0.0), axis=-1, keepdims=True) * LOG2E

    @pl.when(qi == 0)
    def _():
        for c in range(nq):
            sl = slice(c * tq, (c + 1) * tq)
            rows = cum_ref[sl, :]
            for a in range(nh):
                kaug_ref[a, sl, :hd] = k_ref[sl, a * hd:(a + 1) * hd]
                kaug_ref[a, sl, hd:] = _bias_lanes(-gate(rows, a), True)

    q_start = pl.multiple_of(qi * tq, tq)
    q_rows = cum_ref[pl.ds(q_start, tq), :]
    q = q_ref[...]
    qs = [jnp.concatenate([q[:, a * hd:(a + 1) * hd], _bias_lanes(gate(q_rows, a), False)], axis=1)
          for a in range(nh)]

    def tile(j, a):
        start = pl.multiple_of(j * tq, tq)
        return _qk(qs[a], kaug_ref[a, pl.ds(start, tq), :]), v_ref[pl.ds(start, tq), a * hd:(a + 1) * hd]

    def body(j, carry):
        return tuple(_online_softmax_step(*tile(j, a), carry[a]) for a in range(nh))

    carry = lax.fori_loop(0, qi, body, (_softmax_init(tq, hd),) * nh)
    for a in range(nh):
        s, v = tile(qi, a)
        _, l, acc = _online_softmax_step(_causal_mask(s), v, carry[a])
        o_ref[:, a * hd:(a + 1) * hd] = (acc / l).astype(o_ref.dtype)


def fox_attention(qkv, cum, batch, seq, heads, hd):
    assert hd == LANES
    m = qkv.shape[0]
    tq = _tile(seq, ATTN_TQ)
    nq = seq // tq
    hp = cum.shape[1]
    nh = FOX_HEADS_PER_STEP
    hg = heads // nh
    w = nh * hd
    return pl.pallas_call(
        functools.partial(_fox_attn_kernel, tq=tq, hd=hd),
        out_shape=jax.ShapeDtypeStruct((m, heads * hd), BF16),
        grid=(batch, hg, nq),
        in_specs=[pl.BlockSpec((tq, w), lambda b, h, i: (b * nq + i, h)),
                  pl.BlockSpec((seq, w), lambda b, h, i: (b, hg + h)),
                  pl.BlockSpec((seq, w), lambda b, h, i: (b, 2 * hg + h)),
                  pl.BlockSpec((seq, hp), lambda b, h, i: (b, 0))],
        out_specs=pl.BlockSpec((tq, w), lambda b, h, i: (b * nq + i, h)),
        scratch_shapes=[pltpu.VMEM((nh, seq, hd + LANES), BF16)],
        compiler_params=_params("parallel", "parallel", "arbitrary"),
        name="fox_attention",
    )(qkv, qkv, qkv, cum)


def _diff_attn_kernel(slope_ref, q_ref, k_ref, v_ref, lam_ref, g_ref, o_ref, kaug_ref, *,
                      tq, hd, lambda_init, eps):
    h = pl.program_id(1)
    qi = pl.program_id(2)
    slope2 = slope_ref[h] * LOG2E
    nq = k_ref.shape[0] // tq

    def alibi(start):
        return slope2 * (lax.broadcasted_iota(jnp.int32, (tq, 1), 0) + start).astype(F32)

    @pl.when(qi == 0)
    def _():
        for c in range(nq):
            sl = slice(c * tq, (c + 1) * tq)
            kx = _bias_lanes(alibi(c * tq), True)
            for a in range(2):
                kaug_ref[a, sl, :hd] = k_ref[sl, a * hd:(a + 1) * hd]
                kaug_ref[a, sl, hd:] = kx

    qx = _bias_lanes(-alibi(qi * tq), False)
    q = q_ref[...]
    qs = [jnp.concatenate([q[:, a * hd:(a + 1) * hd], qx], axis=1) for a in range(2)]

    ev = v_ref.shape[-1]

    def tile(j):
        start = pl.multiple_of(j * tq, tq)
        return [_qk(qs[a], kaug_ref[a, pl.ds(start, tq), :]) for a in range(2)], v_ref[pl.ds(start, tq), :]

    def body(j, carry):
        ss, v = tile(j)
        stats = []
        for a in range(2):
            m, l, _ = carry[a]
            m_new = jnp.maximum(m, jnp.max(ss[a], axis=-1, keepdims=True))
            alpha = jnp.exp2(m - m_new)
            p = jnp.exp2(ss[a] - m_new)
            stats.append((m_new, alpha * l + jnp.sum(p, axis=-1, keepdims=True), alpha, p.astype(v.dtype)))
        pv = jnp.dot(jnp.concatenate([stats[0][3], stats[1][3]], axis=0), v, preferred_element_type=F32)
        return tuple((stats[a][0], stats[a][1], stats[a][2] * carry[a][2] + pv[a * tq:(a + 1) * tq])
                     for a in range(2))

    one = _softmax_init(tq, ev)
    carry = lax.fori_loop(0, qi, body, (one, one))
    ss, v = tile(qi)
    outs = []
    for a in range(2):
        _, l, acc = _online_softmax_step(_causal_mask(ss[a]), v, carry[a])
        outs.append(acc / l)
    lam = lam_ref[...]
    lam_full = (jnp.exp(jnp.sum(lam[0:1] * lam[1:2], axis=-1, keepdims=True))
                - jnp.exp(jnp.sum(lam[2:3] * lam[3:4], axis=-1, keepdims=True)) + lambda_init)
    out = outs[0] - lam_full * outs[1]
    ms = jnp.mean(out * out, axis=-1, keepdims=True)
    y = (out * lax.rsqrt(ms + eps)) * g_ref[...]
    o_ref[...] = (y * (1.0 - lambda_init)).astype(o_ref.dtype)


def diff_attention(qkv, lam, subln_g, batch, seq, heads, hd, lambda_init):
    assert hd == LANES
    m = qkv.shape[0]
    tq = _tile(seq, ATTN_TQ)
    nq = seq // tq
    ev = 2 * hd
    slopes = jnp.exp2(-8.0 * jnp.arange(1, heads + 1, dtype=F32) / heads)
    grid_spec = pltpu.PrefetchScalarGridSpec(
        num_scalar_prefetch=1,
        grid=(batch, heads, nq),
        in_specs=[pl.BlockSpec((tq, ev), lambda b, h, i, s: (b * nq + i, h)),
                  pl.BlockSpec((seq, ev), lambda b, h, i, s: (b, heads + h)),
                  pl.BlockSpec((seq, ev), lambda b, h, i, s: (b, 2 * heads + h)),
                  pl.BlockSpec((4, hd), lambda b, h, i, s: (0, 0)),
                  pl.BlockSpec((1, ev), lambda b, h, i, s: (0, 0))],
        out_specs=pl.BlockSpec((tq, ev), lambda b, h, i, s: (b * nq + i, h)),
        scratch_shapes=[pltpu.VMEM((2, seq, hd + LANES), BF16)],
    )
    return pl.pallas_call(
        functools.partial(_diff_attn_kernel, tq=tq, hd=hd, lambda_init=lambda_init, eps=DIFF_SUBLN_EPS),
        out_shape=jax.ShapeDtypeStruct((m, heads * ev), BF16),
        grid_spec=grid_spec,
        compiler_params=_params("parallel", "parallel", "arbitrary"),
        name="diff_attention",
    )(slopes, qkv, qkv, qkv, lam.astype(F32), subln_g.reshape(1, ev).astype(F32))


def _ssm_conv_kernel(x_ref, halo_ref, w_ref, b_ref, o_ref):
    i = pl.program_id(1)
    halo = jnp.where(i == 0, 0.0, halo_ref[...].astype(F32))
    u = _causal_conv_rows(x_ref[...].astype(F32), halo, w_ref[...], b_ref[...])
    o_ref[...] = _silu(u).astype(o_ref.dtype)


def ssm_conv_silu(zxbc, col_start, w, b, batch, seq):
    m = zxbc.shape[0]
    cdim = w.shape[1]
    ts, tc = _tile(seq, CONV_TS), _tile(cdim, CONV_TC)
    assert col_start % tc == 0
    cb0 = col_start // tc
    ns = seq // ts
    hb = ts // BF16_SUBLANES
    return pl.pallas_call(
        _ssm_conv_kernel,
        out_shape=jax.ShapeDtypeStruct((m, cdim), BF16),
        grid=(batch, ns, cdim // tc),
        in_specs=[pl.BlockSpec((ts, tc), lambda bi, i, j: (bi * ns + i, cb0 + j)),
                  pl.BlockSpec((BF16_SUBLANES, tc),
                               lambda bi, i, j: (jnp.maximum((bi * ns + i) * hb - 1, 0), cb0 + j)),
                  pl.BlockSpec((w.shape[0], tc), lambda bi, i, j: (0, j)),
                  pl.BlockSpec((1, tc), lambda bi, i, j: (0, j))],
        out_specs=pl.BlockSpec((ts, tc), lambda bi, i, j: (bi * ns + i, j)),
        compiler_params=_params("parallel", "parallel", "parallel"),
        name="ssm_conv_silu",
    )(zxbc, zxbc, w.astype(F32), b.reshape(1, cdim).astype(F32))


def _ssm_dt_kernel(dt_ref, bias_ref, alog_ref, dt_out, la_out):
    x = dt_ref[...] + bias_ref[...]
    dt = jnp.maximum(x, 0.0) + jnp.log(1.0 + jnp.exp(-jnp.abs(x)))
    dt_out[...] = dt
    la_out[...] = _cumsum_rows(dt * (-jnp.exp(alog_ref[...]))) * LOG2E


def ssm_dt(dt_raw, dt_bias, a_log):
    m, h = dt_raw.shape
    spec = pl.BlockSpec((SSM_CHUNK, h), lambda i: (i, 0))
    pspec = pl.BlockSpec((1, h), lambda i: (0, 0))
    return pl.pallas_call(
        _ssm_dt_kernel,
        out_shape=(jax.ShapeDtypeStruct((m, h), F32), jax.ShapeDtypeStruct((m, h), F32)),
        grid=(m // SSM_CHUNK,),
        in_specs=[spec, pspec, pspec],
        out_specs=(spec, spec),
        compiler_params=_params("parallel"),
        name="ssm_dt",
    )(dt_raw, dt_bias.reshape(1, h).astype(F32), a_log.reshape(1, h).astype(F32))


def _ssd_scan_kernel(x_ref, b_ref, c_ref, dt_ref, la_ref, lat_ref, z_ref, d_ref, g_ref, o_ref,
                     state_ref, *, hpg, pdim, eps):
    ci = pl.program_id(2)

    @pl.when(ci == 0)
    def _():
        state_ref[...] = jnp.zeros_like(state_ref)

    gw = hpg * pdim
    nstate = b_ref.shape[1] // dt_ref.shape[0]
    for gi in range(dt_ref.shape[0]):
        cols = slice(gi * gw, (gi + 1) * gw)
        ncols = slice(gi * nstate, (gi + 1) * nstate)
        _ssd_group_chunk(x_ref.at[:, cols], b_ref.at[:, ncols], c_ref.at[:, ncols], dt_ref.at[gi],
                         la_ref.at[gi], lat_ref.at[gi], z_ref.at[:, cols], d_ref.at[:, cols],
                         g_ref.at[:, cols], o_ref.at[:, cols], state_ref.at[gi], hpg, pdim, eps)


def _ssd_group_chunk(x_ref, b_ref, c_ref, dt_ref, la_ref, lat_ref, z_ref, d_ref, g_ref, o_ref, state_ref,
                     hpg, pdim, eps):
    chunk = x_ref.shape[0]
    npair = hpg // 2
    x = x_ref[...].astype(F32)
    bm = b_ref[...]
    cm = c_ref[...]
    dt = dt_ref[...]
    la = la_ref[...]
    lat = lat_ref[...]
    lane = lax.broadcasted_iota(jnp.int32, (chunk, 2 * pdim), 1)
    first = lane < pdim

    def expand(a):
        lo = lax.broadcasted_iota(jnp.int32, (a.shape[0], 2 * pdim), 1) < pdim
        return jnp.concatenate(
            [jnp.where(lo, a[:, 2 * p:2 * p + 1], a[:, 2 * p + 1:2 * p + 2]) for p in range(npair)], axis=1)

    dt_e = expand(dt)
    la_e = expand(la)
    la_end_e = expand(la[chunk - 1:chunk])
    cb = _qk(cm, bm)
    tri = (lax.broadcasted_iota(jnp.int32, (chunk, chunk), 0)
           >= lax.broadcasted_iota(jnp.int32, (chunk, chunk), 1))
    xdt = x * dt_e
    y_parts = []
    for p in range(npair):
        ms = []
        for hh in (2 * p, 2 * p + 1):
            seg = la[:, hh:hh + 1] - lat[hh:hh + 1, :]
            ms.append((cb * jnp.exp2(jnp.where(tri, seg, -jnp.inf))).astype(BF16))
        xp = xdt[:, p * 2 * pdim:(p + 1) * 2 * pdim]
        rhs = jnp.concatenate([jnp.where(first, xp, 0.0), jnp.where(first, 0.0, xp)], axis=0).astype(BF16)
        y_parts.append(jnp.dot(jnp.concatenate(ms, axis=1), rhs, preferred_element_type=F32))
    y = jnp.concatenate(y_parts, axis=1)
    state = state_ref[...]
    y = y + jnp.dot(cm, state.astype(BF16), preferred_element_type=F32) * jnp.exp2(la_e)
    to_end = jnp.exp2(la_end_e - la_e) * dt_e
    upd = lax.dot_general(bm, (x * to_end).astype(BF16), (((0,), (0,)), ((), ())),
                          preferred_element_type=F32)
    state_ref[...] = state * jnp.exp2(la_end_e) + upd
    y = y + d_ref[...] * x
    y = y * _silu(z_ref[...].astype(F32))
    y = y * lax.rsqrt(jnp.mean(y * y, axis=-1, keepdims=True) + eps)
    o_ref[...] = (y * g_ref[...]).astype(o_ref.dtype)


def ssd_scan(xbc, z, dt_g, la_g, lat_g, d_e, norm_g, batch, seq, d_inner, heads):
    m = xbc.shape[0]
    groups, nstate, chunk = SSM_GROUPS, SSM_STATE, SSM_CHUNK
    hpg = heads // groups
    pdim = d_inner // heads
    gps = math.gcd(groups, SSD_GROUPS_PER_STEP)
    gw = gps * hpg * pdim
    nw = gps * nstate
    nc = seq // chunk
    xb = d_inner // nw
    row = lambda b, g, c: b * nc + c
    return pl.pallas_call(
        functools.partial(_ssd_scan_kernel, hpg=hpg, pdim=pdim, eps=SSM_NORM_EPS),
        out_shape=jax.ShapeDtypeStruct((m, d_inner), BF16),
        grid=(batch, groups // gps, nc),
        in_specs=[pl.BlockSpec((chunk, gw), lambda b, g, c: (row(b, g, c), g)),
                  pl.BlockSpec((chunk, nw), lambda b, g, c: (row(b, g, c), xb + g)),
                  pl.BlockSpec((chunk, nw), lambda b, g, c: (row(b, g, c), xb + groups // gps + g)),
                  pl.BlockSpec((gps, chunk, hpg), lambda b, g, c: (g, row(b, g, c), 0)),
                  pl.BlockSpec((gps, chunk, hpg), lambda b, g, c: (g, row(b, g, c), 0)),
                  pl.BlockSpec((gps, hpg, chunk), lambda b, g, c: (g, 0, row(b, g, c))),
                  pl.BlockSpec((chunk, gw), lambda b, g, c: (row(b, g, c), g)),
                  pl.BlockSpec((1, gw), lambda b, g, c: (0, g)),
                  pl.BlockSpec((1, gw), lambda b, g, c: (0, g))],
        out_specs=pl.BlockSpec((chunk, gw), lambda b, g, c: (row(b, g, c), g)),
        scratch_shapes=[pltpu.VMEM((gps, nstate, gw // gps), F32)],
        compiler_params=_params("parallel", "parallel", "arbitrary"),
        name="ssd_scan",
    )(xbc, xbc, xbc, dt_g, la_g, lat_g, z, d_e, norm_g)


def in_proj_cast(kind, layer, d, fox_w_in_t, fox_heads, ssm_w_in, diff_w_in, diff_hd):
    q_scale = lambda hd: (hd ** -0.5) * LOG2E
    if kind == 0:
        return SideCast(fox_w_in_t, layer, _round_up(fox_w_in_t.shape[1], BF16_SUBLANES),
                        (0, d, q_scale(d // fox_heads)))
    if kind == 1:
        return SideCast(ssm_w_in, layer, d, None)
    return SideCast(diff_w_in, layer, d, (1, d, q_scale(diff_hd)))


def fox_mixer(h, hn, f_logit, layer, w_in, b_f, w_o, batch, seq, side):
    d = hn.shape[1]
    heads = b_f.shape[0]
    hd = d // heads
    qkv, side_out, wo = matmul(hn, w_in, out_dtype=BF16, ncols=3 * d, w_t=True,
                               sides=(side, SideCast(w_o, layer, w_o.shape[1], None)))
    hp = f_logit.shape[1]
    b_pad = jnp.pad(b_f.reshape(1, heads).astype(F32), ((0, 0), (0, hp - heads)))
    cum = fox_gate_cumsum(f_logit, b_pad, seq)
    attn = fox_attention(qkv, cum, batch, seq, heads, hd)
    return matmul(attn, wo, res=h), side_out


def ssd_mixer(h, hn, dt_raw, layer, w_in, conv_w, conv_b, dt_bias, a_log, d_skip, norm_g, w_out, batch, seq,
              side):
    heads = a_log.shape[0]
    d_inner = w_out.shape[1]
    conv_dim = conv_w.shape[1]
    groups = SSM_GROUPS
    hpg = heads // groups
    m = hn.shape[0]
    zxbc, side_out, wo = matmul(hn, w_in, out_dtype=BF16, ncols=d_inner + conv_dim,
                                sides=(side, SideCast(w_out, layer, d_inner, None)))
    dt_raw = dt_raw[:, :heads]
    xbc = ssm_conv_silu(zxbc, d_inner, conv_w, conv_b, batch, seq)
    dt, la = ssm_dt(dt_raw, dt_bias, a_log)
    dt_g = jnp.transpose(dt.reshape(m, groups, hpg), (1, 0, 2))
    la_g = jnp.transpose(la.reshape(m, groups, hpg), (1, 0, 2))
    lat_g = jnp.transpose(la_g, (0, 2, 1))
    pdim = d_inner // heads
    d_e = jnp.repeat(d_skip.astype(F32), pdim).reshape(1, d_inner)
    y = ssd_scan(xbc, zxbc, dt_g, la_g, lat_g, d_e, norm_g.reshape(1, d_inner).astype(F32),
                 batch, seq, d_inner, heads)
    return matmul(y, wo, res=h), side_out


def diff_mixer(h, hn, layer, w_in, lam, subln_g, w_o, lambda_init, batch, seq, side):
    d = hn.shape[1]
    hd = lam.shape[1]
    heads = d // (2 * hd)
    qkv, side_out, wo = matmul(hn, w_in, out_dtype=BF16,
                               sides=(side, SideCast(w_o, layer, w_o.shape[1], None)))
    attn = diff_attention(qkv, lam, subln_g, batch, seq, heads, hd, lambda_init)
    return matmul(attn, wo, res=h), side_out


def conv_ffn(h, hn, layer, wu, conv_w, conv_b, w_down, seq, side):
    f = w_down.shape[1]
    fp = _round_up(f, FFN_TN)
    cw = conv_w.astype(F32)
    cb = conv_b.reshape(1, 2 * f).astype(F32)
    pad = lambda a: jnp.pad(a, ((0, 0), (0, fp - f)))
    act, wd = ffn_up(hn, wu, pad(cw[:, :f]), pad(cw[:, f:]), pad(cb[:, :f]), pad(cb[:, f:]), seq,
                     SideCast(w_down, layer, fp, None))
    if side is None:
        return matmul(act, wd, res=h), None
    return matmul(act, wd, res=h, sides=(side,))


def kernel(x, mix_norm_g, ffn_norm_g, fox_w_in, fox_b_f, fox_w_o, ssm_w_in, ssm_conv_w, ssm_conv_b,
           ssm_dt_bias, ssm_a_log, ssm_d, ssm_norm_g, ssm_w_out, diff_w_in, diff_lambda, diff_subln_g,
           diff_w_o, ffn_w_up, ffn_conv_w, ffn_conv_b, ffn_w_down, final_norm_g):
    batch, seq, d = x.shape
    depth = mix_norm_g.shape[0]
    h = x.reshape(batch * seq, d)

    fox_w_in_t = jnp.swapaxes(fox_w_in, 1, 2)

    def in_cast(i):
        return in_proj_cast(i % N_MIXERS, i // N_MIXERS, d, fox_w_in_t, fox_b_f.shape[1], ssm_w_in,
                            diff_w_in, diff_lambda.shape[2])

    w_in = cast_rows(in_cast(0))
    for i in range(depth):
        kind, j = i % N_MIXERS, i // N_MIXERS
        up = SideCast(ffn_w_up, i, ffn_w_up.shape[1], None)
        if kind == 0:
            hn, f_logit = rmsnorm(h, mix_norm_g[i], NORM_EPS, BF16, proj=(w_in, 3 * d, fox_b_f.shape[1], True))
            h, wu = fox_mixer(h, hn, f_logit, j, w_in, fox_b_f[j], fox_w_o, batch, seq, up)
        elif kind == 1:
            dt_start = ssm_w_out.shape[1] + ssm_conv_w.shape[2]
            hn, dt_raw = rmsnorm(h, mix_norm_g[i], NORM_EPS, BF16, proj=(w_in, dt_start, ssm_a_log.shape[1], False))
            h, wu = ssd_mixer(h, hn, dt_raw, j, w_in, ssm_conv_w[j], ssm_conv_b[j], ssm_dt_bias[j],
                              ssm_a_log[j], ssm_d[j], ssm_norm_g[j], ssm_w_out, batch, seq, up)
        else:
            hn = rmsnorm(h, mix_norm_g[i], NORM_EPS, BF16)
            lambda_init = 0.8 - 0.6 * math.exp(-0.3 * i)
            h, wu = diff_mixer(h, hn, j, w_in, diff_lambda[j], diff_subln_g[j], diff_w_o,
                               lambda_init, batch, seq, up)
        hn = rmsnorm(h, ffn_norm_g[i], NORM_EPS, BF16)
        h, w_in = conv_ffn(h, hn, i, wu, ffn_conv_w[i], ffn_conv_b[i], ffn_w_down, seq,
                           in_cast(i + 1) if i + 1 < depth else None)
    return rmsnorm(h, final_norm_g, NORM_EPS, F32).reshape(batch, seq, d)
```

```python
import collections
import functools
import math

import jax
import jax.numpy as jnp
from jax import lax
from jax.experimental import pallas as pl
from jax.experimental.pallas import tpu as pltpu

F32 = jnp.float32
BF16 = jnp.bfloat16
LOG2E = 1.4426950408889634

V7X_VMEM_LIMIT_BYTES = 56 * 1024 * 1024
LANES = 128
BF16_SUBLANES = 16

NORM_EPS = 1e-6
SSM_NORM_EPS = 1e-5
DIFF_SUBLN_EPS = 1e-5
SSM_GROUPS = 8
SSM_STATE = 128
SSM_CHUNK = 128
N_MIXERS = 3

MM_TM = 1024
MM_TN = 1024
MM_TK_MAX = 4096
MM_TK_SPLIT = 3072
FFN_TM = 1024
FFN_TN = 512
ATTN_TQ = 1024
FOX_HEADS_PER_STEP = 2
SSD_GROUPS_PER_STEP = 8
NORM_TM = 512
CONV_TS = 512
CONV_TC = 1024
CAST_STEPS = 64


def _tile(dim, pref):
    if dim <= pref:
        return dim
    t = pref
    while dim % t:
        t //= 2
    return t


def _round_up(x, m):
    return -(-x // m) * m


def _params(*sem):
    return pltpu.CompilerParams(dimension_semantics=sem, vmem_limit_bytes=V7X_VMEM_LIMIT_BYTES)


SideCast = collections.namedtuple("SideCast", "src layer out_rows qscale")


def _side_setup(side, nsteps, step_of):
    _, rows, cols = side.src.shape
    r = BF16_SUBLANES
    while side.out_rows % r or side.out_rows // r > nsteps:
        r += BF16_SUBLANES
        assert r <= side.out_rows
    nblocks = side.out_rows // r
    last_src = (rows - 1) // r
    blk = lambda *idx: jnp.minimum(step_of(*idx), nblocks - 1)
    in_spec = pl.BlockSpec((None, r, cols), lambda *idx: (side.layer, jnp.minimum(blk(*idx), last_src), 0))
    return dict(in_specs=[in_spec], args=[side.src],
                out_spec=pl.BlockSpec((r, cols), lambda *idx: (blk(*idx), 0)),
                out_shape=jax.ShapeDtypeStruct((side.out_rows, cols), BF16),
                static=(nblocks, rows, side.qscale))


def _side_cast(src_ref, dst_ref, step, static):
    nblocks, rows, qscale = static
    w = src_ref[...]
    r = dst_ref.shape[0]
    row = lax.broadcasted_iota(jnp.int32, w.shape, 0) + jnp.minimum(step, nblocks - 1) * r
    if qscale is not None:
        axis, n, c = qscale
        idx = row if axis == 0 else lax.broadcasted_iota(jnp.int32, w.shape, 1)
        w = jnp.where(idx < n, w * c, w)
    if nblocks * r != rows:
        w = jnp.where(row < rows, w, 0.0)
    dst_ref[...] = w.astype(dst_ref.dtype)


def _cast_rows_kernel(src_ref, dst_ref, *, static):
    _side_cast(src_ref, dst_ref, pl.program_id(0), static)


def cast_rows(side):
    setup = _side_setup(side, CAST_STEPS, lambda i: i)
    return pl.pallas_call(
        functools.partial(_cast_rows_kernel, static=setup["static"]),
        out_shape=setup["out_shape"],
        grid=(setup["static"][0],),
        in_specs=setup["in_specs"],
        out_specs=setup["out_spec"],
        compiler_params=_params("parallel"),
        name="cast_rows",
    )(*setup["args"])


def _rmsnorm_kernel(x_ref, g_ref, *rest, eps, ncols, w_t):
    x = x_ref[...]
    ms = jnp.mean(x * x, axis=-1, keepdims=True)
    hn = ((x * lax.rsqrt(ms + eps)) * g_ref[...]).astype(rest[-1 if ncols is None else -2].dtype)
    if ncols is None:
        rest[0][...] = hn
        return
    w_ref, o_ref, p_ref = rest
    o_ref[...] = hn
    p_ref[...] = _dot(hn, _masked_weight(w_ref[...], ncols, 0, w_t), w_t)


def rmsnorm(x, g, eps, out_dtype, proj=None):
    m, d = x.shape
    tm = _tile(m, NORM_TM)
    in_specs = [pl.BlockSpec((tm, d), lambda i: (i, 0)), pl.BlockSpec((1, d), lambda i: (0, 0))]
    args = [x, g.reshape(1, d).astype(F32)]
    out_shape = jax.ShapeDtypeStruct((m, d), out_dtype)
    out_specs = pl.BlockSpec((tm, d), lambda i: (i, 0))
    ncols, w_t = None, False
    if proj is not None:
        w, col_start, ncols, w_t = proj
        n = _round_up(ncols, LANES)
        assert col_start % n == 0
        in_specs.append(pl.BlockSpec((n, d), lambda i: (col_start // n, 0)) if w_t
                        else pl.BlockSpec((d, n), lambda i: (0, col_start // n)))
        args.append(w)
        out_shape = (out_shape, jax.ShapeDtypeStruct((m, n), F32))
        out_specs = (out_specs, pl.BlockSpec((tm, n), lambda i: (i, 0)))
    return pl.pallas_call(
        functools.partial(_rmsnorm_kernel, eps=eps, ncols=ncols, w_t=w_t),
        out_shape=out_shape,
        grid=(m // tm,),
        in_specs=in_specs,
        out_specs=out_specs,
        compiler_params=_params("parallel"),
        name="rmsnorm",
    )(*args)


def _masked_weight(w, ncols, start, w_t):
    axis = 0 if w_t else 1
    if ncols % w.shape[axis] == 0:
        return w
    out_col = lax.broadcasted_iota(jnp.int32, w.shape, axis) + start
    return jnp.where(out_col < ncols, w, jnp.zeros_like(w))


def _dot(x, w, w_t):
    dims = (((1,), (1,)), ((), ())) if w_t else (((1,), (0,)), ((), ()))
    return lax.dot_general(x, w, dims, preferred_element_type=F32)


def _mm_kernel(*refs, nk, has_res, sides, ncols, w_t):
    n_reg = 3 if has_res else 2
    step = (pl.program_id(0) * pl.num_programs(1) + pl.program_id(1)) * nk + pl.program_id(2)
    for idx, s in enumerate(sides):
        _side_cast(refs[n_reg + idx], refs[n_reg + len(sides) + 1 + idx], step, s)
    x_ref, w_ref = refs[:2]
    r_ref = refs[2] if has_res else None
    o_ref = refs[n_reg + len(sides)]
    tn = o_ref.shape[1]
    part = _dot(x_ref[...], _masked_weight(w_ref[...], ncols, pl.program_id(1) * tn, w_t), w_t)

    def finish(acc):
        if has_res:
            acc = acc + r_ref[...]
        o_ref[...] = acc.astype(o_ref.dtype)

    if nk == 1:
        finish(part)
        return
    acc_ref = refs[-1]
    k = pl.program_id(2)

    @pl.when(k == 0)
    def _():
        acc_ref[...] = part

    @pl.when(jnp.logical_and(k > 0, k < nk - 1))
    def _():
        acc_ref[...] += part

    @pl.when(k == nk - 1)
    def _():
        finish(acc_ref[...] + part)


def matmul(x, w, res=None, out_dtype=F32, sides=(), col_start=0, ncols=None, w_t=False):
    m, kdim = x.shape
    ncols = w.shape[0 if w_t else 1] - col_start if ncols is None else ncols
    n = _round_up(ncols, LANES)
    tm, tn = _tile(m, MM_TM), _tile(n, MM_TN)
    assert col_start % tn == 0
    cb0 = col_start // tn
    tk = kdim
    if kdim > MM_TK_MAX:
        tk = MM_TK_SPLIT
        while kdim % tk:
            tk -= 2 * LANES
    nk = kdim // tk
    w_spec = (pl.BlockSpec((tn, tk), lambda i, j, k: (cb0 + j, k)) if w_t
              else pl.BlockSpec((tk, tn), lambda i, j, k: (k, cb0 + j)))
    in_specs = [pl.BlockSpec((tm, tk), lambda i, j, k: (i, k)), w_spec]
    args = [x, w]
    if res is not None:
        in_specs.append(pl.BlockSpec((tm, tn), lambda i, j, k: (i, j)))
        args.append(res)
    grid = (m // tm, n // tn, nk)
    out_shape = jax.ShapeDtypeStruct((m, n), out_dtype)
    out_specs = pl.BlockSpec((tm, tn), lambda i, j, k: (i, j))
    nj = grid[1]
    setups = [_side_setup(s, math.prod(grid), lambda i, j, k: (i * nj + j) * nk + k) for s in sides]
    for setup in setups:
        in_specs += setup["in_specs"]
        args += setup["args"]
    if sides:
        out_shape = (out_shape, *(s["out_shape"] for s in setups))
        out_specs = (out_specs, *(s["out_spec"] for s in setups))
    return pl.pallas_call(
        functools.partial(_mm_kernel, nk=nk, has_res=res is not None,
                          sides=tuple(s["static"] for s in setups), ncols=ncols, w_t=w_t),
        out_shape=out_shape,
        grid=grid,
        in_specs=in_specs,
        out_specs=out_specs,
        scratch_shapes=[pltpu.VMEM((tm, tn), F32)] if nk > 1 else [],
        compiler_params=_params("arbitrary", "arbitrary", "arbitrary"),
        name="matmul",
    )(*args)


def _causal_conv_rows(p, halo, w, b):
    kw = w.shape[0]
    hr = halo.shape[0]
    top = p[0:hr]
    u = b + w[kw - 1:kw] * p
    ut = b + w[kw - 1:kw] * top
    row = lax.broadcasted_iota(jnp.int32, top.shape, 0)
    for k in range(1, kw):
        wk = w[kw - 1 - k:kw - k]
        u = u + wk * pltpu.roll(p, k, 0)
        shifted_top = jnp.where(row < k, pltpu.roll(halo, k, 0), pltpu.roll(top, k, 0))
        ut = ut + wk * shifted_top
    return jnp.concatenate([ut, u[hr:]], axis=0)


def _silu(x):
    return x / (1.0 + jnp.exp(-x))


def _ffn_up_kernel(x_ref, wgl_ref, wgh_ref, wvl_ref, wvh_ref, cwg_ref, cwv_ref, cbg_ref, cbv_ref,
                   side_src, o_ref, side_dst, halo_g, halo_v, *, tiles_per_seq, side, last_is_half):
    j = pl.program_id(0)
    i = pl.program_id(1)
    _side_cast(side_src, side_dst, j * pl.num_programs(1) + i, side)

    @pl.when(i % tiles_per_seq == 0)
    def _():
        halo_g[...] = jnp.zeros_like(halo_g)
        halo_v[...] = jnp.zeros_like(halo_v)

    tn = o_ref.shape[1]

    def tile(cols):
        x = x_ref[...]
        if cols == tn:
            wg = jnp.concatenate([wgl_ref[...], wgh_ref[...]], axis=1)
            wv = jnp.concatenate([wvl_ref[...], wvh_ref[...]], axis=1)
        else:
            wg, wv = wgl_ref[...], wvl_ref[...]
            o_ref[:, cols:] = jnp.zeros((o_ref.shape[0], tn - cols), o_ref.dtype)
        pg = jnp.dot(x, wg, preferred_element_type=F32)
        pv = jnp.dot(x, wv, preferred_element_type=F32)
        ug = _causal_conv_rows(pg, halo_g[:, :cols], cwg_ref[:, :cols], cbg_ref[:, :cols])
        uv = _causal_conv_rows(pv, halo_v[:, :cols], cwv_ref[:, :cols], cbv_ref[:, :cols])
        hr = halo_g.shape[0]
        halo_g[:, :cols] = pg[pg.shape[0] - hr:]
        halo_v[:, :cols] = pv[pv.shape[0] - hr:]
        o_ref[:, :cols] = (_silu(ug) * uv).astype(o_ref.dtype)

    if not last_is_half:
        tile(tn)
        return
    last = pl.num_programs(0) - 1

    @pl.when(j < last)
    def _():
        tile(tn)

    @pl.when(j == last)
    def _():
        tile(tn // 2)


def ffn_up(x, w_up, cwg, cwv, cbg, cbv, seq, side):
    m, d = x.shape
    f = w_up.shape[1] // 2
    fp = cwg.shape[1]
    tm, tn = _tile(seq, FFN_TM), _tile(fp, FFN_TN)
    half = tn // 2
    assert f % half == 0
    nb, last = f // half, 2 * f // half - 1
    kw = cwg.shape[0]
    wspec = lambda off: pl.BlockSpec((d, half), lambda j, i: (0, jnp.minimum(2 * j + off, last)))
    cspec = pl.BlockSpec((kw, tn), lambda j, i: (0, j))
    bspec = pl.BlockSpec((1, tn), lambda j, i: (0, j))
    nj, ni = fp // tn, m // tm
    setup = _side_setup(side, nj * ni, lambda j, i: j * ni + i)
    return pl.pallas_call(
        functools.partial(_ffn_up_kernel, tiles_per_seq=seq // tm, side=setup["static"],
                          last_is_half=fp - f == half),
        out_shape=(jax.ShapeDtypeStruct((m, fp), BF16), setup["out_shape"]),
        grid=(nj, ni),
        in_specs=[pl.BlockSpec((tm, d), lambda j, i: (i, 0)),
                  wspec(0), wspec(1), wspec(nb), wspec(nb + 1), cspec, cspec, bspec, bspec] + setup["in_specs"],
        out_specs=(pl.BlockSpec((tm, tn), lambda j, i: (i, j)), setup["out_spec"]),
        scratch_shapes=[pltpu.VMEM((BF16_SUBLANES, tn), F32), pltpu.VMEM((BF16_SUBLANES, tn), F32)],
        compiler_params=_params("arbitrary", "arbitrary"),
        name="ffn_up",
    )(x, w_up, w_up, w_up, w_up, cwg, cwv, cbg, cbv, side.src)


def _cumsum_rows(y):
    n = y.shape[0]
    row = lax.broadcasted_iota(jnp.int32, y.shape, 0)
    shift = 1
    while shift < n:
        y = y + jnp.where(row >= shift, pltpu.roll(y, shift, 0), 0.0)
        shift *= 2
    return y


def _fox_gate_kernel(f_ref, b_ref, o_ref):
    x = f_ref[...] + b_ref[...]
    log_f = jnp.minimum(x, 0.0) - jnp.log(1.0 + jnp.exp(-jnp.abs(x)))
    o_ref[...] = _cumsum_rows(log_f)


def fox_gate_cumsum(f_logit, b_f, seq):
    m, hp = f_logit.shape
    return pl.pallas_call(
        _fox_gate_kernel,
        out_shape=jax.ShapeDtypeStruct((m, hp), F32),
        grid=(m // seq,),
        in_specs=[pl.BlockSpec((seq, hp), lambda b: (b, 0)), pl.BlockSpec((1, hp), lambda b: (0, 0))],
        out_specs=pl.BlockSpec((seq, hp), lambda b: (b, 0)),
        compiler_params=_params("parallel"),
        name="fox_gate_cumsum",
    )(f_logit, b_f)


def _bias_lanes(x, pieces_first):
    hi = x.astype(BF16).astype(F32)
    rem = x - hi
    lo = rem.astype(BF16).astype(F32)
    lo2 = rem - lo
    lane = lax.broadcasted_iota(jnp.int32, (x.shape[0], LANES), 1)
    base = 0 if pieces_first else 3
    ones = 3 if pieces_first else 0
    out = jnp.where(lane == base, hi, jnp.where(lane == base + 1, lo, jnp.where(lane == base + 2, lo2, 0.0)))
    out = jnp.where(jnp.logical_and(lane >= ones, lane < ones + 3), 1.0, out)
    return out.astype(BF16)


def _qk(q, k):
    return lax.dot_general(q, k, (((1,), (1,)), ((), ())), preferred_element_type=F32)


def _online_softmax_step(s, v, carry):
    m, l, acc = carry
    m_new = jnp.maximum(m, jnp.max(s, axis=-1, keepdims=True))
    alpha = jnp.exp2(m - m_new)
    p = jnp.exp2(s - m_new)
    l = alpha * l + jnp.sum(p, axis=-1, keepdims=True)
    acc = alpha * acc + jnp.dot(p.astype(v.dtype), v, preferred_element_type=F32)
    return m_new, l, acc


def _softmax_init(tq, ev):
    return (jnp.full((tq, 1), -jnp.inf, F32), jnp.zeros((tq, 1), F32), jnp.zeros((tq, ev), F32))


def _causal_mask(s):
    r = lax.broadcasted_iota(jnp.int32, s.shape, 0)
    c = lax.broadcasted_iota(jnp.int32, s.shape, 1)
    return jnp.where(c <= r, s, -jnp.inf)


def _fox_attn_kernel(q_ref, k_ref, v_ref, cum_ref, o_ref, kaug_ref, *, tq, hd):
    hg = pl.program_id(1)
    qi = pl.program_id(2)
    nh = k_ref.shape[-1] // hd
    nq = k_ref.shape[0] // tq

    def gate(rows, a):
        lane = lax.broadcasted_iota(jnp.int32, rows.shape, 1)
        return jnp.sum(jnp.where(lane == hg * nh + a, rows, 0.0), axis=-1, keepdims=True) * LOG2E

    @pl.when(qi == 0)
    def _():
        for c in range(nq):
            sl = slice(c * tq, (c + 1) * tq)
            rows = cum_ref[sl, :]
            for a in range(nh):
                kaug_ref[a, sl, :hd] = k_ref[sl, a * hd:(a + 1) * hd]
                kaug_ref[a, sl, hd:] = _bias_lanes(-gate(rows, a), True)

    q_start = pl.multiple_of(qi * tq, tq)
    q_rows = cum_ref[pl.ds(q_start, tq), :]
    q = q_ref[...]
    qs = [jnp.concatenate([q[:, a * hd:(a + 1) * hd], _bias_lanes(gate(q_rows, a), False)], axis=1)
          for a in range(nh)]

    def tile(j, a):
        start = pl.multiple_of(j * tq, tq)
        return _qk(qs[a], kaug_ref[a, pl.ds(start, tq), :]), v_ref[pl.ds(start, tq), a * hd:(a + 1) * hd]

    def body(j, carry):
        return tuple(_online_softmax_step(*tile(j, a), carry[a]) for a in range(nh))

    carry = lax.fori_loop(0, qi, body, (_softmax_init(tq, hd),) * nh)
    for a in range(nh):
        s, v = tile(qi, a)
        _, l, acc = _online_softmax_step(_causal_mask(s), v, carry[a])
        o_ref[:, a * hd:(a + 1) * hd] = (acc / l).astype(o_ref.dtype)


def fox_attention(qkv, cum, batch, seq, heads, hd):
    assert hd == LANES
    m = qkv.shape[0]
    tq = _tile(seq, ATTN_TQ)
    nq = seq // tq
    hp = cum.shape[1]
    nh = FOX_HEADS_PER_STEP
    hg = heads // nh
    w = nh * hd
    return pl.pallas_call(
        functools.partial(_fox_attn_kernel, tq=tq, hd=hd),
        out_shape=jax.ShapeDtypeStruct((m, heads * hd), BF16),
        grid=(batch, hg, nq),
        in_specs=[pl.BlockSpec((tq, w), lambda b, h, i: (b * nq + i, h)),
                  pl.BlockSpec((seq, w), lambda b, h, i: (b, hg + h)),
                  pl.BlockSpec((seq, w), lambda b, h, i: (b, 2 * hg + h)),
                  pl.BlockSpec((seq, hp), lambda b, h, i: (b, 0))],
        out_specs=pl.BlockSpec((tq, w), lambda b, h, i: (b * nq + i, h)),
        scratch_shapes=[pltpu.VMEM((nh, seq, hd + LANES), BF16)],
        compiler_params=_params("parallel", "parallel", "arbitrary"),
        name="fox_attention",
    )(qkv, qkv, qkv, cum)


def _diff_attn_kernel(slope_ref, q_ref, k_ref, v_ref, lam_ref, g_ref, o_ref, kaug_ref, *,
                      tq, hd, lambda_init, eps):
    h = pl.program_id(1)
    qi = pl.program_id(2)
    slope2 = slope_ref[h] * LOG2E
    nq = k_ref.shape[0] // tq

    def alibi(start):
        return slope2 * (lax.broadcasted_iota(jnp.int32, (tq, 1), 0) + start).astype(F32)

    @pl.when(qi == 0)
    def _():
        for c in range(nq):
            sl = slice(c * tq, (c + 1) * tq)
            kx = _bias_lanes(alibi(c * tq), True)
            for a in range(2):
                kaug_ref[a, sl, :hd] = k_ref[sl, a * hd:(a + 1) * hd]
                kaug_ref[a, sl, hd:] = kx

    qx = _bias_lanes(-alibi(qi * tq), False)
    q = q_ref[...]
    qs = [jnp.concatenate([q[:, a * hd:(a + 1) * hd], qx], axis=1) for a in range(2)]

    ev = v_ref.shape[-1]

    def tile(j):
        start = pl.multiple_of(j * tq, tq)
        return [_qk(qs[a], kaug_ref[a, pl.ds(start, tq), :]) for a in range(2)], v_ref[pl.ds(start, tq), :]

    def step(ss, v, carry):
        stats = []
        for a in range(2):
            m, l, _ = carry[a]
            m_new = jnp.maximum(m, jnp.max(ss[a], axis=-1, keepdims=True))
            alpha = jnp.exp2(m - m_new)
            p = jnp.exp2(ss[a] - m_new)
            stats.append((m_new, alpha * l + jnp.sum(p, axis=-1, keepdims=True), alpha, p.astype(v.dtype)))
        pv = jnp.dot(jnp.concatenate([stats[0][3], stats[1][3]], axis=0), v, preferred_element_type=F32)
        return tuple((stats[a][0], stats[a][1], stats[a][2] * carry[a][2] + pv[a * tq:(a + 1) * tq])
                     for a in range(2))

    one = _softmax_init(tq, ev)
    carry = lax.fori_loop(0, qi, lambda j, c: step(*tile(j), c), (one, one))
    ss, v = tile(qi)
    carry = step([_causal_mask(s) for s in ss], v, carry)
    outs = [acc / l for _, l, acc in carry]
    lam = lam_ref[...]
    lam_full = (jnp.exp(jnp.sum(lam[0:1] * lam[1:2], axis=-1, keepdims=True))
                - jnp.exp(jnp.sum(lam[2:3] * lam[3:4], axis=-1, keepdims=True)) + lambda_init)
    out = outs[0] - lam_full * outs[1]
    ms = jnp.mean(out * out, axis=-1, keepdims=True)
    y = (out * lax.rsqrt(ms + eps)) * g_ref[...]
    o_ref[...] = (y * (1.0 - lambda_init)).astype(o_ref.dtype)


def diff_attention(qkv, lam, subln_g, batch, seq, heads, hd, lambda_init):
    assert hd == LANES
    m = qkv.shape[0]
    tq = _tile(seq, ATTN_TQ)
    nq = seq // tq
    ev = 2 * hd
    slopes = jnp.exp2(-8.0 * jnp.arange(1, heads + 1, dtype=F32) / heads)
    grid_spec = pltpu.PrefetchScalarGridSpec(
        num_scalar_prefetch=1,
        grid=(batch, heads, nq),
        in_specs=[pl.BlockSpec((tq, ev), lambda b, h, i, s: (b * nq + i, h)),
                  pl.BlockSpec((seq, ev), lambda b, h, i, s: (b, heads + h)),
                  pl.BlockSpec((seq, ev), lambda b, h, i, s: (b, 2 * heads + h)),
                  pl.BlockSpec((4, hd), lambda b, h, i, s: (0, 0)),
                  pl.BlockSpec((1, ev), lambda b, h, i, s: (0, 0))],
        out_specs=pl.BlockSpec((tq, ev), lambda b, h, i, s: (b * nq + i, h)),
        scratch_shapes=[pltpu.VMEM((2, seq, hd + LANES), BF16)],
    )
    return pl.pallas_call(
        functools.partial(_diff_attn_kernel, tq=tq, hd=hd, lambda_init=lambda_init, eps=DIFF_SUBLN_EPS),
        out_shape=jax.ShapeDtypeStruct((m, heads * ev), BF16),
        grid_spec=grid_spec,
        compiler_params=_params("parallel", "parallel", "arbitrary"),
        name="diff_attention",
    )(slopes, qkv, qkv, qkv, lam.astype(F32), subln_g.reshape(1, ev).astype(F32))


def _ssm_conv_kernel(x_ref, halo_ref, w_ref, b_ref, o_ref):
    i = pl.program_id(1)
    halo = jnp.where(i == 0, 0.0, halo_ref[...].astype(F32))
    u = _causal_conv_rows(x_ref[...].astype(F32), halo, w_ref[...], b_ref[...])
    o_ref[...] = _silu(u).astype(o_ref.dtype)


def ssm_conv_silu(zxbc, col_start, w, b, batch, seq):
    m = zxbc.shape[0]
    cdim = w.shape[1]
    ts, tc = _tile(seq, CONV_TS), _tile(cdim, CONV_TC)
    assert col_start % tc == 0
    cb0 = col_start // tc
    ns = seq // ts
    hb = ts // BF16_SUBLANES
    return pl.pallas_call(
        _ssm_conv_kernel,
        out_shape=jax.ShapeDtypeStruct((m, cdim), BF16),
        grid=(batch, ns, cdim // tc),
        in_specs=[pl.BlockSpec((ts, tc), lambda bi, i, j: (bi * ns + i, cb0 + j)),
                  pl.BlockSpec((BF16_SUBLANES, tc),
                               lambda bi, i, j: (jnp.maximum((bi * ns + i) * hb - 1, 0), cb0 + j)),
                  pl.BlockSpec((w.shape[0], tc), lambda bi, i, j: (0, j)),
                  pl.BlockSpec((1, tc), lambda bi, i, j: (0, j))],
        out_specs=pl.BlockSpec((ts, tc), lambda bi, i, j: (bi * ns + i, j)),
        compiler_params=_params("parallel", "parallel", "parallel"),
        name="ssm_conv_silu",
    )(zxbc, zxbc, w.astype(F32), b.reshape(1, cdim).astype(F32))


def _ssm_dt_kernel(dt_ref, bias_ref, alog_ref, dt_out, la_out):
    x = dt_ref[...] + bias_ref[...]
    dt = jnp.maximum(x, 0.0) + jnp.log(1.0 + jnp.exp(-jnp.abs(x)))
    dt_out[...] = dt
    la_out[...] = _cumsum_rows(dt * (-jnp.exp(alog_ref[...]))) * LOG2E


def ssm_dt(dt_raw, dt_bias, a_log):
    m, h = dt_raw.shape
    spec = pl.BlockSpec((SSM_CHUNK, h), lambda i: (i, 0))
    pspec = pl.BlockSpec((1, h), lambda i: (0, 0))
    return pl.pallas_call(
        _ssm_dt_kernel,
        out_shape=(jax.ShapeDtypeStruct((m, h), F32), jax.ShapeDtypeStruct((m, h), F32)),
        grid=(m // SSM_CHUNK,),
        in_specs=[spec, pspec, pspec],
        out_specs=(spec, spec),
        compiler_params=_params("parallel"),
        name="ssm_dt",
    )(dt_raw, dt_bias.reshape(1, h).astype(F32), a_log.reshape(1, h).astype(F32))


def _ssd_scan_kernel(x_ref, b_ref, c_ref, dt_ref, la_ref, lat_ref, z_ref, d_ref, g_ref, o_ref,
                     state_ref, *, hpg, pdim, eps):
    ci = pl.program_id(2)

    @pl.when(ci == 0)
    def _():
        state_ref[...] = jnp.zeros_like(state_ref)

    gw = hpg * pdim
    nstate = b_ref.shape[1] // dt_ref.shape[0]
    for gi in range(dt_ref.shape[0]):
        cols = slice(gi * gw, (gi + 1) * gw)
        ncols = slice(gi * nstate, (gi + 1) * nstate)
        _ssd_group_chunk(x_ref.at[:, cols], b_ref.at[:, ncols], c_ref.at[:, ncols], dt_ref.at[gi],
                         la_ref.at[gi], lat_ref.at[gi], z_ref.at[:, cols], d_ref.at[:, cols],
                         g_ref.at[:, cols], o_ref.at[:, cols], state_ref.at[gi], hpg, pdim, eps)


def _ssd_group_chunk(x_ref, b_ref, c_ref, dt_ref, la_ref, lat_ref, z_ref, d_ref, g_ref, o_ref, state_ref,
                     hpg, pdim, eps):
    chunk = x_ref.shape[0]
    npair = hpg // 2
    x = x_ref[...].astype(F32)
    bm = b_ref[...]
    cm = c_ref[...]
    dt = dt_ref[...]
    la = la_ref[...]
    lat = lat_ref[...]
    lane = lax.broadcasted_iota(jnp.int32, (chunk, 2 * pdim), 1)
    first = lane < pdim

    def expand(a):
        lo = lax.broadcasted_iota(jnp.int32, (a.shape[0], 2 * pdim), 1) < pdim
        return jnp.concatenate(
            [jnp.where(lo, a[:, 2 * p:2 * p + 1], a[:, 2 * p + 1:2 * p + 2]) for p in range(npair)], axis=1)

    dt_e = expand(dt)
    la_e = expand(la)
    la_end_e = expand(la[chunk - 1:chunk])
    cb = _qk(cm, bm)
    tri = (lax.broadcasted_iota(jnp.int32, (chunk, chunk), 0)
           >= lax.broadcasted_iota(jnp.int32, (chunk, chunk), 1))
    xdt = x * dt_e
    y_parts = []
    for p in range(npair):
        ms = []
        for hh in (2 * p, 2 * p + 1):
            seg = la[:, hh:hh + 1] - lat[hh:hh + 1, :]
            ms.append((cb * jnp.exp2(jnp.where(tri, seg, -jnp.inf))).astype(BF16))
        xp = xdt[:, p * 2 * pdim:(p + 1) * 2 * pdim]
        rhs = jnp.concatenate([jnp.where(first, xp, 0.0), jnp.where(first, 0.0, xp)], axis=0).astype(BF16)
        y_parts.append(jnp.dot(jnp.concatenate(ms, axis=1), rhs, preferred_element_type=F32))
    y = jnp.concatenate(y_parts, axis=1)
    state = state_ref[...]
    y = y + jnp.dot(cm, state.astype(BF16), preferred_element_type=F32) * jnp.exp2(la_e)
    to_end = jnp.exp2(la_end_e - la_e) * dt_e
    upd = lax.dot_general(bm, (x * to_end).astype(BF16), (((0,), (0,)), ((), ())),
                          preferred_element_type=F32)
    state_ref[...] = state * jnp.exp2(la_end_e) + upd
    y = y + d_ref[...] * x
    y = y * _silu(z_ref[...].astype(F32))
    y = y * lax.rsqrt(jnp.mean(y * y, axis=-1, keepdims=True) + eps)
    o_ref[...] = (y * g_ref[...]).astype(o_ref.dtype)


def ssd_scan(xbc, z, dt_g, la_g, lat_g, d_e, norm_g, batch, seq, d_inner, heads):
    m = xbc.shape[0]
    groups, nstate, chunk = SSM_GROUPS, SSM_STATE, SSM_CHUNK
    hpg = heads // groups
    pdim = d_inner // heads
    gps = math.gcd(groups, SSD_GROUPS_PER_STEP)
    gw = gps * hpg * pdim
    nw = gps * nstate
    nc = seq // chunk
    xb = d_inner // nw
    row = lambda b, g, c: b * nc + c
    return pl.pallas_call(
        functools.partial(_ssd_scan_kernel, hpg=hpg, pdim=pdim, eps=SSM_NORM_EPS),
        out_shape=jax.ShapeDtypeStruct((m, d_inner), BF16),
        grid=(batch, groups // gps, nc),
        in_specs=[pl.BlockSpec((chunk, gw), lambda b, g, c: (row(b, g, c), g)),
                  pl.BlockSpec((chunk, nw), lambda b, g, c: (row(b, g, c), xb + g)),
                  pl.BlockSpec((chunk, nw), lambda b, g, c: (row(b, g, c), xb + groups // gps + g)),
                  pl.BlockSpec((gps, chunk, hpg), lambda b, g, c: (g, row(b, g, c), 0)),
                  pl.BlockSpec((gps, chunk, hpg), lambda b, g, c: (g, row(b, g, c), 0)),
                  pl.BlockSpec((gps, hpg, chunk), lambda b, g, c: (g, 0, row(b, g, c))),
                  pl.BlockSpec((chunk, gw), lambda b, g, c: (row(b, g, c), g)),
                  pl.BlockSpec((1, gw), lambda b, g, c: (0, g)),
                  pl.BlockSpec((1, gw), lambda b, g, c: (0, g))],
        out_specs=pl.BlockSpec((chunk, gw), lambda b, g, c: (row(b, g, c), g)),
        scratch_shapes=[pltpu.VMEM((gps, nstate, gw // gps), F32)],
        compiler_params=_params("parallel", "parallel", "arbitrary"),
        name="ssd_scan",
    )(xbc, xbc, xbc, dt_g, la_g, lat_g, z, d_e, norm_g)


def in_proj_cast(kind, layer, d, fox_w_in_t, fox_heads, ssm_w_in, diff_w_in, diff_hd):
    q_scale = lambda hd: (hd ** -0.5) * LOG2E
    if kind == 0:
        return SideCast(fox_w_in_t, layer, _round_up(fox_w_in_t.shape[1], BF16_SUBLANES),
                        (0, d, q_scale(d // fox_heads)))
    if kind == 1:
        return SideCast(ssm_w_in, layer, d, None)
    return SideCast(diff_w_in, layer, d, (1, d, q_scale(diff_hd)))


def fox_mixer(h, hn, f_logit, layer, w_in, b_f, w_o, batch, seq, side):
    d = hn.shape[1]
    heads = b_f.shape[0]
    hd = d // heads
    qkv, side_out, wo = matmul(hn, w_in, out_dtype=BF16, ncols=3 * d, w_t=True,
                               sides=(side, SideCast(w_o, layer, w_o.shape[1], None)))
    hp = f_logit.shape[1]
    b_pad = jnp.pad(b_f.reshape(1, heads).astype(F32), ((0, 0), (0, hp - heads)))
    cum = fox_gate_cumsum(f_logit, b_pad, seq)
    attn = fox_attention(qkv, cum, batch, seq, heads, hd)
    return matmul(attn, wo, res=h), side_out


def ssd_mixer(h, hn, dt_raw, layer, w_in, conv_w, conv_b, dt_bias, a_log, d_skip, norm_g, w_out, batch, seq,
              side):
    heads = a_log.shape[0]
    d_inner = w_out.shape[1]
    conv_dim = conv_w.shape[1]
    groups = SSM_GROUPS
    hpg = heads // groups
    m = hn.shape[0]
    zxbc, side_out, wo = matmul(hn, w_in, out_dtype=BF16, ncols=d_inner + conv_dim,
                                sides=(side, SideCast(w_out, layer, d_inner, None)))
    dt_raw = dt_raw[:, :heads]
    xbc = ssm_conv_silu(zxbc, d_inner, conv_w, conv_b, batch, seq)
    dt, la = ssm_dt(dt_raw, dt_bias, a_log)
    dt_g = jnp.transpose(dt.reshape(m, groups, hpg), (1, 0, 2))
    la_g = jnp.transpose(la.reshape(m, groups, hpg), (1, 0, 2))
    lat_g = jnp.transpose(la_g, (0, 2, 1))
    pdim = d_inner // heads
    d_e = jnp.repeat(d_skip.astype(F32), pdim).reshape(1, d_inner)
    y = ssd_scan(xbc, zxbc, dt_g, la_g, lat_g, d_e, norm_g.reshape(1, d_inner).astype(F32),
                 batch, seq, d_inner, heads)
    return matmul(y, wo, res=h), side_out


def diff_mixer(h, hn, layer, w_in, lam, subln_g, w_o, lambda_init, batch, seq, side):
    d = hn.shape[1]
    hd = lam.shape[1]
    heads = d // (2 * hd)
    qkv, side_out, wo = matmul(hn, w_in, out_dtype=BF16,
                               sides=(side, SideCast(w_o, layer, w_o.shape[1], None)))
    attn = diff_attention(qkv, lam, subln_g, batch, seq, heads, hd, lambda_init)
    return matmul(attn, wo, res=h), side_out


def conv_ffn(h, hn, layer, wu, conv_w, conv_b, w_down, seq, side):
    f = w_down.shape[1]
    fp = _round_up(f, FFN_TN)
    cw = conv_w.astype(F32)
    cb = conv_b.reshape(1, 2 * f).astype(F32)
    pad = lambda a: jnp.pad(a, ((0, 0), (0, fp - f)))
    act, wd = ffn_up(hn, wu, pad(cw[:, :f]), pad(cw[:, f:]), pad(cb[:, :f]), pad(cb[:, f:]), seq,
                     SideCast(w_down, layer, fp, None))
    if side is None:
        return matmul(act, wd, res=h), None
    return matmul(act, wd, res=h, sides=(side,))


def kernel(x, mix_norm_g, ffn_norm_g, fox_w_in, fox_b_f, fox_w_o, ssm_w_in, ssm_conv_w, ssm_conv_b,
           ssm_dt_bias, ssm_a_log, ssm_d, ssm_norm_g, ssm_w_out, diff_w_in, diff_lambda, diff_subln_g,
           diff_w_o, ffn_w_up, ffn_conv_w, ffn_conv_b, ffn_w_down, final_norm_g):
    batch, seq, d = x.shape
    depth = mix_norm_g.shape[0]
    h = x.reshape(batch * seq, d)

    fox_w_in_t = jnp.swapaxes(fox_w_in, 1, 2)

    def in_cast(i):
        return in_proj_cast(i % N_MIXERS, i // N_MIXERS, d, fox_w_in_t, fox_b_f.shape[1], ssm_w_in,
                            diff_w_in, diff_lambda.shape[2])

    w_in = cast_rows(in_cast(0))
    for i in range(depth):
        kind, j = i % N_MIXERS, i // N_MIXERS
        up = SideCast(ffn_w_up, i, ffn_w_up.shape[1], None)
        if kind == 0:
            hn, f_logit = rmsnorm(h, mix_norm_g[i], NORM_EPS, BF16, proj=(w_in, 3 * d, fox_b_f.shape[1], True))
            h, wu = fox_mixer(h, hn, f_logit, j, w_in, fox_b_f[j], fox_w_o, batch, seq, up)
        elif kind == 1:
            dt_start = ssm_w_out.shape[1] + ssm_conv_w.shape[2]
            hn, dt_raw = rmsnorm(h, mix_norm_g[i], NORM_EPS, BF16, proj=(w_in, dt_start, ssm_a_log.shape[1], False))
            h, wu = ssd_mixer(h, hn, dt_raw, j, w_in, ssm_conv_w[j], ssm_conv_b[j], ssm_dt_bias[j],
                              ssm_a_log[j], ssm_d[j], ssm_norm_g[j], ssm_w_out, batch, seq, up)
        else:
            hn = rmsnorm(h, mix_norm_g[i], NORM_EPS, BF16)
            lambda_init = 0.8 - 0.6 * math.exp(-0.3 * i)
            h, wu = diff_mixer(h, hn, j, w_in, diff_lambda[j], diff_subln_g[j], diff_w_o,
                               lambda_init, batch, seq, up)
        hn = rmsnorm(h, ffn_norm_g[i], NORM_EPS, BF16)
        h, w_in = conv_ffn(h, hn, i, wu, ffn_conv_w[i], ffn_conv_b[i], ffn_w_down, seq,
                           in_cast(i + 1) if i + 1 < depth else None)
    return rmsnorm(h, final_norm_g, NORM_EPS, F32).reshape(batch, seq, d)
```
